```python
import math
import jax
import jax.numpy as jnp
from jax import lax
import numpy as np

D_MODEL = 4096
BATCH = 2
SEQ = 8192
DEPTH = 4

Q_BLOCK = 128
NEG_INF = -1e30
EPS = 1e-6

N_MIX_HEADS = D_MODEL // 128
A_HEADS = N_MIX_HEADS // 4
A_QK_DIM = 64
A_V_DIM = 2 * A_QK_DIM
NUM_BUCKETS = 32
MAX_DISTANCE = 128
B_HEADS = N_MIX_HEADS // 4
B_DIM = 128
C_HEADS = N_MIX_HEADS // 2
C_Q_LORA = D_MODEL // 4
C_KV_LORA = 512
C_NOPE_DIM = 128
C_ROPE_DIM = 64
C_V_DIM = 128
ROPE_BASE = 10000.0
MIX_WIDTH = A_HEADS * A_V_DIM + B_HEADS * B_DIM + C_HEADS * C_V_DIM
IN_SIZES = (
    A_HEADS * 2 * A_QK_DIM,
    A_HEADS * 2 * A_QK_DIM,
    A_HEADS * A_V_DIM,
    B_HEADS * B_DIM,
    B_HEADS * B_DIM,
    B_HEADS * B_DIM,
    B_HEADS,
    C_Q_LORA,
    C_KV_LORA,
    C_ROPE_DIM,
)
IN_COLS = sum(IN_SIZES)
D_FF = 2 * D_MODEL
N_EXPERTS = 8
TOP_K = 2
EXPERT_FF = D_MODEL // 2
N_DENSE = (DEPTH + 1) // 2
N_MOE = DEPTH // 2

kernel_name = 'hybrid_diff_fox_mla_moe_trunk'


def rms_norm(x, gain):
    xf = x.astype(jnp.float32)
    y = xf * lax.rsqrt(jnp.mean(xf * xf, axis=-1, keepdims=True) + EPS)
    return (y * gain.astype(jnp.float32)).astype(x.dtype)


def rope(x, positions):
    half = x.shape[-1] // 2
    inv_freq = ROPE_BASE ** (-jnp.arange(half, dtype=jnp.float32) / half)
    ang = positions.astype(jnp.float32)[..., None] * inv_freq
    ang = ang.reshape(ang.shape[:2] + (1,) * (x.ndim - 3) + (half,))
    cos, sin = jnp.cos(ang), jnp.sin(ang)
    x1, x2 = x[..., :half], x[..., half:]
    return jnp.concatenate([x1 * cos - x2 * sin, x1 * sin + x2 * cos], axis=-1).astype(x.dtype)


def t5_bucket(rel):
    n = jnp.maximum(rel, 0)
    max_exact = NUM_BUCKETS // 2
    large = max_exact + (
        jnp.log(jnp.maximum(n, max_exact).astype(jnp.float32) / max_exact)
        / math.log(MAX_DISTANCE / max_exact) * (NUM_BUCKETS - max_exact)
    ).astype(jnp.int32)
    large = jnp.minimum(large, NUM_BUCKETS - 1)
    return jnp.where(n < max_exact, n, large)


def causal_block_mask(start, seq_len):
    q_idx = start + jnp.arange(Q_BLOCK)
    k_idx = jnp.arange(seq_len)
    return k_idx[None, :] <= q_idx[:, None]


def masked_softmax(scores, mask):
    return jax.nn.softmax(jnp.where(mask, scores, NEG_INF), axis=-1)


def sweep_query_blocks(block_fn, seq_len):
    out = lax.map(block_fn, jnp.arange(seq_len // Q_BLOCK))
    out = jnp.moveaxis(out, 0, 1)
    return out.reshape((out.shape[0], seq_len) + out.shape[3:])


def split_columns(proj):
    points = np.cumsum(IN_SIZES)[:-1].tolist()
    return jnp.split(proj, points, axis=-1)


def differential_attention(a_q, a_k, a_v, positions, qk_norm, lam_params, head_norm, rel_bias, lambda_init):
    bsz, seq, _ = a_q.shape
    q = rms_norm(a_q.reshape(bsz, seq, A_HEADS, 2, A_QK_DIM), qk_norm[0])
    k = rms_norm(a_k.reshape(bsz, seq, A_HEADS, 2, A_QK_DIM), qk_norm[1])
    v = a_v.reshape(bsz, seq, A_HEADS, A_V_DIM)
    lp = lam_params.astype(jnp.float32)
    lam = jnp.exp(jnp.sum(lp[0] * lp[1])) - jnp.exp(jnp.sum(lp[2] * lp[3])) + lambda_init
    scale = A_QK_DIM ** -0.5

    def block(i):
        start = i * Q_BLOCK
        qb = lax.dynamic_slice_in_dim(q, start, Q_BLOCK, axis=1)
        pos_q = lax.dynamic_slice_in_dim(positions, start, Q_BLOCK, axis=1)
        s = jnp.einsum('bqhmd,bkhmd->bhmqk', qb, k, preferred_element_type=jnp.float32) * scale
        bias = rel_bias[t5_bucket(pos_q[:, :, None] - positions[:, None, :])]
        s = s + jnp.transpose(bias, (0, 3, 1, 2))[:, :, None].astype(jnp.float32)
        p = masked_softmax(s, causal_block_mask(start, seq))
        attn = p[:, :, 0] - lam * p[:, :, 1]
        return jnp.einsum('bhqk,bkhd->bqhd', attn.astype(v.dtype), v)

    o = sweep_query_blocks(block, seq)
    o = rms_norm(o, head_norm) * (1.0 - lambda_init)
    return o.reshape(bsz, seq, A_HEADS * A_V_DIM)


def forgetting_attention(b_q, b_k, b_v, f_logit, b_forget, qk_norm):
    bsz, seq, _ = b_q.shape
    q = rms_norm(b_q.reshape(bsz, seq, B_HEADS, B_DIM), qk_norm[0])
    k = rms_norm(b_k.reshape(bsz, seq, B_HEADS, B_DIM), qk_norm[1])
    v = b_v.reshape(bsz, seq, B_HEADS, B_DIM)
    log_f = jax.nn.log_sigmoid(f_logit.astype(jnp.float32) + b_forget.astype(jnp.float32))
    cum = jnp.transpose(lax.cumsum(log_f, axis=1), (0, 2, 1))
    scale = B_DIM ** -0.5

    def block(i):
        start = i * Q_BLOCK
        qb = lax.dynamic_slice_in_dim(q, start, Q_BLOCK, axis=1)
        cum_q = lax.dynamic_slice_in_dim(cum, start, Q_BLOCK, axis=2)
        s = jnp.einsum('bqhd,bkhd->bhqk', qb, k, preferred_element_type=jnp.float32) * scale
        s = s + (cum_q[..., :, None] - cum[..., None, :])
        p = masked_softmax(s, causal_block_mask(start, seq))
        return jnp.einsum('bhqk,bkhd->bqhd', p.astype(v.dtype), v)

    o = sweep_query_blocks(block, seq)
    return o.reshape(bsz, seq, B_HEADS * B_DIM)


def latent_attention(c_q, c_kv, c_kpe, positions, q_a_norm, kv_a_norm, w_uq, w_ukv, norm_nope, norm_rope):
    bsz, seq, _ = c_q.shape
    q = jnp.einsum('bsr,rc->bsc', rms_norm(c_q, q_a_norm), w_uq)
    q = q.reshape(bsz, seq, C_HEADS, C_NOPE_DIM + C_ROPE_DIM)
    q_nope = rms_norm(q[..., :C_NOPE_DIM], norm_nope[0])
    q_pe = rope(rms_norm(q[..., C_NOPE_DIM:], norm_rope[0]), positions)
    kv = jnp.einsum('bsr,rc->bsc', rms_norm(c_kv, kv_a_norm), w_ukv)
    kv = kv.reshape(bsz, seq, C_HEADS, C_NOPE_DIM + C_V_DIM)
    k_nope = rms_norm(kv[..., :C_NOPE_DIM], norm_nope[1])
    v = kv[..., C_NOPE_DIM:]
    k_pe = rope(rms_norm(c_kpe, norm_rope[1]), positions)
    scale = (C_NOPE_DIM + C_ROPE_DIM) ** -0.5

    def block(i):
        start = i * Q_BLOCK
        qn = lax.dynamic_slice_in_dim(q_nope, start, Q_BLOCK, axis=1)
        qp = lax.dynamic_slice_in_dim(q_pe, start, Q_BLOCK, axis=1)
        s = (jnp.einsum('bqhd,bkhd->bhqk', qn, k_nope, preferred_element_type=jnp.float32)
             + jnp.einsum('bqhr,bkr->bhqk', qp, k_pe, preferred_element_type=jnp.float32)) * scale
        p = masked_softmax(s, causal_block_mask(start, seq))
        return jnp.einsum('bhqk,bkhd->bqhd', p.astype(v.dtype), v)

    o = sweep_query_blocks(block, seq)
    return o.reshape(bsz, seq, C_HEADS * C_V_DIM)


def swiglu(h, w_gate, w_up, w_down):
    g = jnp.einsum('bsd,df->bsf', h, w_gate)
    u = jnp.einsum('bsd,df->bsf', h, w_up)
    return jnp.einsum('bsf,fd->bsd', jax.nn.silu(g) * u, w_down)


def moe_swiglu(h, router, w_gate, w_up, w_down):
    logits = jnp.einsum('bsd,de->bse', h, router, preferred_element_type=jnp.float32)
    top_vals, top_idx = lax.top_k(logits, TOP_K)
    gates = jax.nn.softmax(top_vals, axis=-1)
    combine = jnp.einsum('bske,bsk->bse', jax.nn.one_hot(top_idx, N_EXPERTS, dtype=jnp.float32), gates)
    y = jnp.zeros_like(h)
    for e in range(N_EXPERTS):
        y = y + combine[..., e:e + 1].astype(h.dtype) * swiglu(h, w_gate[e], w_up[e], w_down[e])
    return y


def setup_inputs(seed: int = 0) -> dict:
    key = jax.random.key(seed)
    ks = jax.random.split(key, 25)
    f32 = jnp.float32

    def normal(k, shape, scale):
        return jax.random.normal(k, shape, f32) * scale

    def gain(k, shape):
        return 1.0 + 0.1 * jax.random.normal(k, shape, f32)

    return {
        'x': normal(ks[0], (BATCH, SEQ, D_MODEL), 1.0),
        'positions': jnp.arange(SEQ, dtype=jnp.int32)[None, :]
        + jax.random.randint(ks[1], (BATCH, 1), 0, 4096, dtype=jnp.int32),
        'attn_norm': gain(ks[2], (DEPTH, D_MODEL)),
        'w_in': normal(ks[3], (DEPTH, D_MODEL, IN_COLS), D_MODEL ** -0.5),
        'b_forget': jax.random.uniform(ks[4], (DEPTH, B_HEADS), f32, 1.0, 3.0),
        'a_qk_norm': gain(ks[5], (DEPTH, 2, A_QK_DIM)),
        'a_lambda': normal(ks[6], (DEPTH, 4, A_QK_DIM), 0.1),
        'a_head_norm': gain(ks[7], (DEPTH, A_V_DIM)),
        'rel_bias': normal(ks[8], (NUM_BUCKETS, A_HEADS), 0.5),
        'b_qk_norm': gain(ks[9], (DEPTH, 2, B_DIM)),
        'c_q_a_norm': gain(ks[10], (DEPTH, C_Q_LORA)),
        'c_kv_a_norm': gain(ks[11], (DEPTH, C_KV_LORA)),
        'c_w_uq': normal(ks[12], (DEPTH, C_Q_LORA, C_HEADS * (C_NOPE_DIM + C_ROPE_DIM)), C_Q_LORA ** -0.5),
        'c_w_ukv': normal(ks[13], (DEPTH, C_KV_LORA, C_HEADS * (C_NOPE_DIM + C_V_DIM)), C_KV_LORA ** -0.5),
        'c_qk_norm_nope': gain(ks[14], (DEPTH, 2, C_NOPE_DIM)),
        'c_qk_norm_rope': gain(ks[15], (DEPTH, 2, C_ROPE_DIM)),
        'w_out': normal(ks[16], (DEPTH, MIX_WIDTH, D_MODEL), MIX_WIDTH ** -0.5),
        'ffn_norm': gain(ks[17], (DEPTH, D_MODEL)),
        'dense_w_gate': normal(ks[18], (N_DENSE, D_MODEL, D_FF), D_MODEL ** -0.5),
        'dense_w_up': normal(ks[19], (N_DENSE, D_MODEL, D_FF), D_MODEL ** -0.5),
        'dense_w_down': normal(ks[20], (N_DENSE, D_FF, D_MODEL), D_FF ** -0.5),
        'moe_router': normal(ks[21], (N_MOE, D_MODEL, N_EXPERTS), D_MODEL ** -0.5),
        'moe_w_gate': normal(ks[22], (N_MOE, N_EXPERTS, D_MODEL, EXPERT_FF), D_MODEL ** -0.5),
        'moe_w_up': normal(ks[23], (N_MOE, N_EXPERTS, D_MODEL, EXPERT_FF), D_MODEL ** -0.5),
        'moe_w_down': normal(ks[24], (N_MOE, N_EXPERTS, EXPERT_FF, D_MODEL), EXPERT_FF ** -0.5),
    }


def reference(x, positions, attn_norm, w_in, b_forget, a_qk_norm, a_lambda, a_head_norm, rel_bias,
              b_qk_norm, c_q_a_norm, c_kv_a_norm, c_w_uq, c_w_ukv, c_qk_norm_nope, c_qk_norm_rope,
              w_out, ffn_norm, dense_w_gate, dense_w_up, dense_w_down, moe_router, moe_w_gate,
              moe_w_up, moe_w_down):
    for layer in range(DEPTH):
        h = rms_norm(x, attn_norm[layer])
        proj = jnp.einsum('bsd,dc->bsc', h, w_in[layer])
        a_q, a_k, a_v, b_q, b_k, b_v, f_logit, c_q, c_kv, c_kpe = split_columns(proj)
        lambda_init = 0.8 - 0.6 * math.exp(-0.3 * layer)
        o_a = differential_attention(a_q, a_k, a_v, positions, a_qk_norm[layer], a_lambda[layer],
                                     a_head_norm[layer], rel_bias, lambda_init)
        o_b = forgetting_attention(b_q, b_k, b_v, f_logit, b_forget[layer], b_qk_norm[layer])
        o_c = latent_attention(c_q, c_kv, c_kpe, positions, c_q_a_norm[layer], c_kv_a_norm[layer],
                               c_w_uq[layer], c_w_ukv[layer], c_qk_norm_nope[layer], c_qk_norm_rope[layer])
        mix = jnp.concatenate([o_a, o_b, o_c], axis=-1)
        x = x + jnp.einsum('bsm,md->bsd', mix, w_out[layer])
        h = rms_norm(x, ffn_norm[layer])
        idx = layer // 2
        if layer % 2 == 0:
            x = x + swiglu(h, dense_w_gate[idx], dense_w_up[idx], dense_w_down[idx])
        else:
            x = x + moe_swiglu(h, moe_router[idx], moe_w_gate[idx], moe_w_up[idx], moe_w_down[idx])
    return x
```

```python
import functools
import math

import jax
import jax.numpy as jnp
from jax import lax
from jax.experimental import pallas as pl
from jax.experimental.pallas import tpu as pltpu

F32 = jnp.float32
BF16 = jnp.bfloat16

LANES = 128
LOG2E = 1.4426950408889634
NEG = -1e30
EPS = 1e-6
VMEM_LIMIT = 52 * 1024 * 1024

NUM_BUCKETS = 32
MAX_DISTANCE = 128
ROPE_BASE = 10000.0
TOP_K = 2

_HI = lax.Precision.HIGHEST


def _cparams(n_axes):
    return pltpu.CompilerParams(
        dimension_semantics=("arbitrary",) * n_axes, vmem_limit_bytes=VMEM_LIMIT)


def _mm_body(a_ref, w_ref, *rest, epilogue, n_aux):
    acc = jnp.dot(a_ref[...], w_ref[...], preferred_element_type=F32)
    epilogue(acc, rest[:n_aux], rest[n_aux:])


def _matmul(a, w, *, tm, tn, epilogue, out_shape, out_specs, aux=(), aux_specs=(), name):
    m, k = a.shape
    n = w.shape[1]
    assert m % tm == 0 and n % tn == 0, (m, n, tm, tn)
    return pl.pallas_call(
        functools.partial(_mm_body, epilogue=epilogue, n_aux=len(aux)),
        grid=(m // tm, n // tn),
        in_specs=[pl.BlockSpec((tm, k), lambda i, j: (i, 0)),
                  pl.BlockSpec((k, tn), lambda i, j: (0, j)), *aux_specs],
        out_specs=out_specs, out_shape=out_shape,
        compiler_params=_cparams(2), name=name)(a, w, *aux)


def _sumsq_lanes(x):
    return jnp.sum(x * x, axis=-1, keepdims=True)


def _ep_residual(acc, aux, outs):
    outs[0][...] = aux[0][...] + acc


def _ep_swiglu(acc, aux, outs):
    th = acc.shape[1] // 2
    g = acc[:, :th]
    u = acc[:, th:]
    outs[0][...] = (g * (1.0 / (1.0 + jnp.exp(-g))) * u).astype(outs[0].dtype)


def _ep_swiglu_scaled(acc, aux, outs):
    th = acc.shape[1] // 2
    g = acc[:, :th]
    u = acc[:, th:]
    y = g * (1.0 / (1.0 + jnp.exp(-g))) * u
    c = aux[0][...]
    outs[0][...] = (y * jnp.tile(c, (1, th // LANES))).astype(outs[0].dtype)


def _ep_group_norm(acc, aux, outs, *, group):
    gain = aux[0][...]
    tn = acc.shape[1]
    lane = lax.broadcasted_iota(jnp.int32, (1, LANES), 1)
    for c in range(tn // LANES):
        x = acc[:, c * LANES:(c + 1) * LANES]
        g = gain[:, c * LANES:(c + 1) * LANES]
        if group == LANES:
            r = lax.rsqrt(_sumsq_lanes(x) * (1.0 / LANES) + EPS)
        else:
            lo = lane < group
            x2 = x * x
            ss_lo = jnp.sum(jnp.where(lo, x2, 0.0), axis=-1, keepdims=True)
            ss_hi = jnp.sum(jnp.where(lo, 0.0, x2), axis=-1, keepdims=True)
            r = jnp.where(lo, lax.rsqrt(ss_lo * (1.0 / group) + EPS),
                          lax.rsqrt(ss_hi * (1.0 / group) + EPS))
        outs[0][:, c * LANES:(c + 1) * LANES] = (x * r * g).astype(outs[0].dtype)


def _ep_transpose_heads(acc, aux, outs, *, t):
    tm, tn = acc.shape
    for hh in range(tn // LANES):
        for s in range(tm // t):
            blk = acc[s * t:(s + 1) * t, hh * LANES:(hh + 1) * LANES]
            outs[0][hh, s] = blk.T.astype(outs[0].dtype)


def _ep_latent_norms(acc, aux, outs, *, n_q):
    gain = aux[0][...]
    n = acc.shape[1]
    xq = acc[:, :n_q]
    xk = acc[:, n_q:]
    rq = lax.rsqrt(_sumsq_lanes(xq) * (1.0 / n_q) + EPS)
    rk = lax.rsqrt(_sumsq_lanes(xk) * (1.0 / (n - n_q)) + EPS)
    outs[0][:, :n_q] = (xq * rq * gain[:, :n_q]).astype(outs[0].dtype)
    outs[0][:, n_q:] = (xk * rk * gain[:, n_q:]).astype(outs[0].dtype)


def _rope_half_block(y, cos, sin):
    lane = lax.broadcasted_iota(jnp.int32, (1, LANES), 1)
    ra = pltpu.roll(y, 32, axis=1)
    rb = pltpu.roll(y, 96, axis=1)
    rot = jnp.where(lane < 32, -rb, ra)
    return y * cos + rot * sin


def _ep_tail(acc, aux, outs, *, rope_dim, n_gate):
    gain = aux[0][...]
    bias = aux[1][...]
    cos = aux[2][...]
    sin = aux[3][...]
    lane = lax.broadcasted_iota(jnp.int32, (1, LANES), 1)
    is_pe = lane < rope_dim
    xpe = jnp.where(is_pe, acc, 0.0)
    r = lax.rsqrt(_sumsq_lanes(xpe) * (1.0 / rope_dim) + EPS)
    y = xpe * r * gain
    outs[0][...] = _rope_half_block(y, cos, sin).astype(outs[0].dtype)
    z = acc + bias
    logsig = -(jnp.maximum(-z, 0.0) + jnp.log(1.0 + jnp.exp(-jnp.abs(z))))
    is_gate = (lane >= rope_dim) & (lane < rope_dim + n_gate)
    outs[1][...] = jnp.where(is_gate, logsig, 0.0)


def _ep_latent_q(acc, aux, outs):
    g_nope = aux[0][...]
    g_rope = aux[1][...]
    cos = aux[2][...]
    sin = aux[3][...]
    for hh in range(acc.shape[1] // (2 * LANES)):
        xn = acc[:, hh * 256:hh * 256 + LANES]
        xr = acc[:, hh * 256 + LANES:(hh + 1) * 256]
        rn = lax.rsqrt(_sumsq_lanes(xn) * (1.0 / LANES) + EPS)
        outs[0][:, hh * 256:hh * 256 + LANES] = (xn * rn * g_nope).astype(outs[0].dtype)
        rr = lax.rsqrt(_sumsq_lanes(xr) * (1.0 / 64) + EPS)
        yr = _rope_half_block(xr * rr * g_rope, cos, sin)
        outs[0][:, hh * 256 + LANES:(hh + 1) * 256] = yr.astype(outs[0].dtype)


def _ep_latent_k(acc, aux, outs):
    g_nope = aux[0][...]
    kpe = aux[1][...]
    for hh in range(acc.shape[1] // LANES):
        xn = acc[:, hh * LANES:(hh + 1) * LANES]
        rn = lax.rsqrt(_sumsq_lanes(xn) * (1.0 / LANES) + EPS)
        outs[0][:, hh * 256:hh * 256 + LANES] = (xn * rn * g_nope).astype(outs[0].dtype)
        outs[0][:, hh * 256 + LANES:(hh + 1) * 256] = kpe


def _norm_body(x_ref, g_ref, h_ref):
    x = x_ref[...]
    d = x.shape[1]
    r = lax.rsqrt(_sumsq_lanes(x) * (1.0 / d) + EPS)
    h_ref[...] = (x * r * g_ref[...]).astype(h_ref.dtype)


def _rms_norm(x, gain, *, tm, name):
    t, d = x.shape
    return pl.pallas_call(
        _norm_body, grid=(t // tm,),
        in_specs=[pl.BlockSpec((tm, d), lambda i: (i, 0)), pl.BlockSpec((1, d), lambda i: (0, 0))],
        out_specs=pl.BlockSpec((tm, d), lambda i: (i, 0)),
        out_shape=jax.ShapeDtypeStruct((t, d), BF16),
        compiler_params=_cparams(1), name=name)(x, gain.reshape(1, d))


def _norm_router_body(x_ref, g_ref, wr_ref, h_ref, comb_ref, *, n_exp):
    x = x_ref[...]
    d = x.shape[1]
    r = lax.rsqrt(_sumsq_lanes(x) * (1.0 / d) + EPS)
    h = x * r * g_ref[...]
    h_ref[...] = h.astype(h_ref.dtype)
    logits = jnp.dot(h, wr_ref[...], preferred_element_type=F32, precision=_HI)
    lane = lax.broadcasted_iota(jnp.int32, logits.shape, 1).astype(F32)
    lg = jnp.where(lane < n_exp, logits, -jnp.inf)
    m1 = jnp.max(lg, axis=-1, keepdims=True)
    i1 = jnp.min(jnp.where(lg == m1, lane, float(LANES)), axis=-1, keepdims=True)
    sel1 = lane == i1
    lg2 = jnp.where(sel1, -jnp.inf, lg)
    m2 = jnp.max(lg2, axis=-1, keepdims=True)
    i2 = jnp.min(jnp.where(lg2 == m2, lane, float(LANES)), axis=-1, keepdims=True)
    sel2 = lane == i2
    e2 = jnp.exp(m2 - m1)
    g1 = 1.0 / (1.0 + e2)
    g2 = e2 * g1
    for e in range(n_exp):
        ce = jnp.where(i1 == float(e), g1, 0.0) + jnp.where(i2 == float(e), g2, 0.0)
        comb_ref[:, e * LANES:(e + 1) * LANES] = jnp.broadcast_to(ce, (x.shape[0], LANES))


def _rms_norm_router(x, gain, router, *, tm, name):
    t, d = x.shape
    n_exp = router.shape[1]
    wr = jnp.zeros((d, LANES), F32).at[:, :n_exp].set(router.astype(F32))
    return pl.pallas_call(
        functools.partial(_norm_router_body, n_exp=n_exp), grid=(t // tm,),
        in_specs=[pl.BlockSpec((tm, d), lambda i: (i, 0)), pl.BlockSpec((1, d), lambda i: (0, 0)),
                  pl.BlockSpec((d, LANES), lambda i: (0, 0))],
        out_specs=[pl.BlockSpec((tm, d), lambda i: (i, 0)),
                   pl.BlockSpec((tm, n_exp * LANES), lambda i: (i, 0))],
        out_shape=[jax.ShapeDtypeStruct((t, d), BF16),
                   jax.ShapeDtypeStruct((t, n_exp * LANES), F32)],
        compiler_params=_cparams(1), name=name)(x, gain.reshape(1, d), wr)


def _rope_body(pos_ref, f_ref, cos_ref, sin_ref):
    ang = pos_ref[...] * f_ref[...]
    live = f_ref[...] > 0.0
    cos_ref[...] = jnp.where(live, jnp.cos(ang), 0.0)
    sin_ref[...] = jnp.where(live, jnp.sin(ang), 0.0)


def _rope_tables(positions, *, half, tm):
    t = positions.size
    inv_freq = ROPE_BASE ** (-jnp.arange(half, dtype=F32) / half)
    f_row = jnp.concatenate([inv_freq, inv_freq, jnp.zeros((LANES - 2 * half,), F32)]).reshape(1, LANES)
    pos_rep = jnp.broadcast_to(positions.astype(F32).reshape(t, 1), (t, LANES))
    spec = pl.BlockSpec((tm, LANES), lambda i: (i, 0))
    return pl.pallas_call(
        _rope_body, grid=(t // tm,),
        in_specs=[spec, pl.BlockSpec((1, LANES), lambda i: (0, 0))],
        out_specs=[spec, spec],
        out_shape=[jax.ShapeDtypeStruct((t, LANES), F32)] * 2,
        compiler_params=_cparams(1), name="rope_tables")(pos_rep, f_row)


def _bias_body(rb_ref, o_ref, *, t):
    h = pl.program_id(0)
    d = pl.program_id(1)
    j = lax.broadcasted_iota(jnp.int32, (t, t), 0)
    i = lax.broadcasted_iota(jnp.int32, (t, t), 1)
    rel = d * t + i - j
    n = jnp.maximum(rel, 0)
    max_exact = NUM_BUCKETS // 2
    large = max_exact + (
        jnp.log(jnp.maximum(n, max_exact).astype(F32) / max_exact)
        / math.log(MAX_DISTANCE / max_exact) * (NUM_BUCKETS - max_exact)
    ).astype(jnp.int32)
    large = jnp.minimum(large, NUM_BUCKETS - 1)
    bucket = jnp.where(n < max_exact, n, large)
    far = rb_ref[NUM_BUCKETS - 1, h]
    val = jnp.zeros((t, t), F32)
    for b in range(NUM_BUCKETS - 1):
        val = jnp.where(bucket == b, rb_ref[b, h] - far, val)
    o_ref[...] = jnp.where(rel >= 0, val * LOG2E, NEG)


def _bias_tiles(rel_bias, *, t):
    assert t + 1 >= MAX_DISTANCE, "tiles beyond the first sub-diagonal must be in the last bucket"
    nh = rel_bias.shape[1]
    return pl.pallas_call(
        functools.partial(_bias_body, t=t), grid=(nh, 2),
        in_specs=[pl.BlockSpec(memory_space=pltpu.SMEM)],
        out_specs=pl.BlockSpec((None, None, t, t), lambda h, d: (h, d, 0, 0)),
        out_shape=jax.ShapeDtypeStruct((nh, 2, t, t), F32),
        compiler_params=_cparams(2), name="t5_bias_tiles")(rel_bias.astype(F32))


def _cum_body(lf_ref, o_ref, carry_ref, *, n_heads, lane0):
    @pl.when(pl.program_id(1) == 0)
    def _():
        carry_ref[...] = jnp.zeros_like(carry_ref)

    x = lf_ref[...]
    tb = x.shape[0]
    r = lax.broadcasted_iota(jnp.int32, (tb, tb), 0)
    c = lax.broadcasted_iota(jnp.int32, (tb, tb), 1)
    tri = jnp.where(r >= c, 1.0, 0.0).astype(F32)
    er = lax.broadcasted_iota(jnp.int32, (LANES, LANES), 0)
    for hh in range(n_heads):
        sel = jnp.where(er == lane0 + hh, 1.0, 0.0).astype(F32)
        xh = jnp.dot(x, sel, preferred_element_type=F32, precision=_HI)
        cum = jnp.dot(tri, xh, preferred_element_type=F32, precision=_HI) + carry_ref[hh:hh + 1, :]
        o_ref[hh] = cum * (-LOG2E)
        carry_ref[hh:hh + 1, :] = cum[tb - 1:tb, :]


def _neg_cumsum(logf, *, batch, seq, n_heads, lane0, tb):
    nb = seq // tb
    return pl.pallas_call(
        functools.partial(_cum_body, n_heads=n_heads, lane0=lane0), grid=(batch, nb),
        in_specs=[pl.BlockSpec((tb, LANES), lambda b, i: (b * nb + i, 0))],
        out_specs=pl.BlockSpec((None, n_heads, tb, LANES), lambda b, i: (b, 0, i, 0)),
        out_shape=jax.ShapeDtypeStruct((batch, n_heads, seq, LANES), F32),
        scratch_shapes=[pltpu.VMEM((n_heads, LANES), F32)],
        compiler_params=_cparams(2), name="forget_cumsum")(logf)


def _online_block(s, vb, m_ref, l_ref, acc_ref):
    m_prev = m_ref[...]
    m_new = jnp.maximum(m_prev, jnp.max(s, axis=0, keepdims=True))
    alpha = jnp.exp2(m_prev - m_new)
    p = jnp.exp2(s - m_new)
    l_ref[...] = alpha * l_ref[...] + jnp.sum(p, axis=0, keepdims=True)
    acc_ref[...] = alpha * acc_ref[...] + jnp.dot(vb, p.astype(BF16), preferred_element_type=F32)
    m_ref[...] = m_new


def _causal_mask(t):
    j = lax.broadcasted_iota(jnp.int32, (t, t), 0)
    i = lax.broadcasted_iota(jnp.int32, (t, t), 1)
    return j <= i


_NT = (((1,), (1,)), ((), ()))


def _init_state(m_ref, l_ref, acc_ref):
    m_ref[...] = jnp.full(m_ref.shape, NEG, F32)
    l_ref[...] = jnp.zeros(l_ref.shape, F32)
    acc_ref[...] = jnp.zeros(acc_ref.shape, F32)


def _attn_plain_body(q_ref, k_ref, vt_ref, *rest, t, has_key_bias):
    if has_key_bias:
        nc_ref, o_ref, m_ref, l_ref, acc_ref = rest
    else:
        o_ref, m_ref, l_ref, acc_ref = rest
    qi = pl.program_id(2)
    q = q_ref[...]
    _init_state(m_ref, l_ref, acc_ref)

    def scores(kj):
        start = pl.multiple_of(kj * t, t)
        s = lax.dot_general(k_ref[pl.ds(start, t), :], q, _NT, preferred_element_type=F32)
        if has_key_bias:
            s = s + jnp.tile(nc_ref[pl.ds(start, t), :], (1, t // LANES))
        return s

    def far(kj, carry):
        _online_block(scores(kj), vt_ref[kj], m_ref, l_ref, acc_ref)
        return carry

    lax.fori_loop(0, qi, far, 0)
    s = jnp.where(_causal_mask(t), scores(qi), NEG)
    _online_block(s, vt_ref[qi], m_ref, l_ref, acc_ref)
    o = acc_ref[...] * (1.0 / l_ref[...])
    o_ref[...] = o.T.astype(o_ref.dtype)


def _attn_diff_body(q_ref, k_ref, vt_ref, bias_ref, lam_ref, hn_ref, o_ref,
                    m0, l0, a0, m1, l1, a1, *, t, qk_dim, lambda_init):
    qi = pl.program_id(2)
    q = q_ref[...]
    lane = lax.broadcasted_iota(jnp.int32, q.shape, 1)
    q_lo = jnp.where(lane < qk_dim, q, jnp.zeros_like(q))
    q_hi = jnp.where(lane < qk_dim, jnp.zeros_like(q), q)
    _init_state(m0, l0, a0)
    _init_state(m1, l1, a1)

    def block(kj, bias):
        start = pl.multiple_of(kj * t, t)
        kb = k_ref[pl.ds(start, t), :]
        vb = vt_ref[kj]
        s0 = lax.dot_general(kb, q_lo, _NT, preferred_element_type=F32)
        s1 = lax.dot_general(kb, q_hi, _NT, preferred_element_type=F32)
        if bias is not None:
            s0 = s0 + bias
            s1 = s1 + bias
        _online_block(s0, vb, m0, l0, a0)
        _online_block(s1, vb, m1, l1, a1)

    def far(kj, carry):
        block(kj, None)
        return carry

    lax.fori_loop(0, jnp.maximum(qi - 1, 0), far, 0)

    @pl.when(qi >= 1)
    def _():
        block(qi - 1, bias_ref[1])

    block(qi, bias_ref[0])

    lp = lam_ref[...]
    lam = (jnp.exp(jnp.sum(lp[0:1] * lp[1:2], axis=-1, keepdims=True))
           - jnp.exp(jnp.sum(lp[2:3] * lp[3:4], axis=-1, keepdims=True)) + lambda_init)
    o = a0[...] * (1.0 / l0[...]) - lam * (a1[...] * (1.0 / l1[...]))
    dv = o.shape[0]
    r = lax.rsqrt(jnp.sum(o * o, axis=0, keepdims=True) * (1.0 / dv) + EPS)
    o = o * r * hn_ref[...] * (1.0 - lambda_init)
    o_ref[...] = o.T.astype(o_ref.dtype)


def _attention(mode, q_arr, k_arr, vt_arr, *, batch, seq, n_heads, t, dq, q_col0, k_col0,
               extra=(), extra_specs=(), body_kwargs=None, name):
    nq = seq // t
    dv = vt_arr.shape[2]
    tokens = batch * seq
    in_specs = [
        pl.BlockSpec((t, dq), lambda b, h, i: (b * nq + i, q_col0 + h)),
        pl.BlockSpec((seq, dq), lambda b, h, i: (b, k_col0 + h)),
        pl.BlockSpec((None, nq, dv, t), lambda b, h, i: (h, b, 0, 0)),
        *extra_specs,
    ]
    n_state = 2 if mode == "diff" else 1
    scratch = []
    for _ in range(n_state):
        scratch += [pltpu.VMEM((1, t), F32), pltpu.VMEM((1, t), F32), pltpu.VMEM((dv, t), F32)]
    if mode == "diff":
        body = functools.partial(_attn_diff_body, t=t, **body_kwargs)
    else:
        body = functools.partial(_attn_plain_body, t=t, has_key_bias=(mode == "forget"))
    return pl.pallas_call(
        body, grid=(batch, n_heads, nq), in_specs=in_specs,
        out_specs=pl.BlockSpec((t, dv), lambda b, h, i: (b * nq + i, h)),
        out_shape=jax.ShapeDtypeStruct((tokens, n_heads * dv), BF16),
        scratch_shapes=scratch, compiler_params=_cparams(3), name=name)(q_arr, k_arr, vt_arr, *extra)


def _pick(n, prefs):
    for p in prefs:
        if n % p == 0:
            return p
    raise ValueError((n, prefs))


def _expert_col(e, i, j):
    return (i, e)


def _interleave_gate_up(wg, wu, th):
    k, f = wg.shape
    return jnp.concatenate([wg.reshape(k, f // th, th), wu.reshape(k, f // th, th)], axis=2).reshape(k, 2 * f)


def kernel(x, positions, attn_norm, w_in, b_forget, a_qk_norm, a_lambda, a_head_norm, rel_bias,
           b_qk_norm, c_q_a_norm, c_kv_a_norm, c_w_uq, c_w_ukv, c_qk_norm_nope, c_qk_norm_rope,
           w_out, ffn_norm, dense_w_gate, dense_w_up, dense_w_down, moe_router, moe_w_gate,
           moe_w_up, moe_w_down):
    batch, seq, d_model = x.shape
    depth = w_in.shape[0]
    tokens = batch * seq
    n_slots = d_model // LANES
    a_heads = n_slots // 4
    b_heads = n_slots // 4
    c_heads = n_slots // 2
    a_qk = a_qk_norm.shape[-1]
    b_dim = b_qk_norm.shape[-1]
    c_q_lora = c_q_a_norm.shape[-1]
    c_kv_lora = c_kv_a_norm.shape[-1]
    c_nope = c_qk_norm_nope.shape[-1]
    c_rope = c_qk_norm_rope.shape[-1]
    c_v = c_w_ukv.shape[-1] // c_heads - c_nope
    n_exp = moe_router.shape[-1]
    assert 2 * a_qk == LANES and b_dim == LANES and c_nope == LANES and c_v == LANES and 2 * c_rope == LANES

    t = _pick(seq, (512, 256, 128))
    tm = _pick(tokens, (1024, 512, 256))
    tm_n = _pick(tokens, (256, 128))

    sizes = (a_heads * 2 * a_qk, a_heads * 2 * a_qk, a_heads * LANES, b_heads * b_dim, b_heads * b_dim,
             b_heads * b_dim, b_heads, c_q_lora, c_kv_lora, c_rope)
    offs = [0]
    for s in sizes:
        offs.append(offs[-1] + s)
    o_aq, o_ak, o_av, o_bq, o_bk, o_bv, o_f, o_cq, o_ckv, o_kpe, o_end = offs

    xf = x.reshape(tokens, d_model)
    cos_t, sin_t = _rope_tables(positions, half=c_rope // 2, tm=tm)
    bias_tiles = _bias_tiles(rel_bias, t=t)

    row_spec = lambda width: pl.BlockSpec((1, width), lambda i, j: (0, j))
    fixed_row = pl.BlockSpec((1, LANES), lambda i, j: (0, 0))
    tok_lane = pl.BlockSpec((tm, LANES), lambda i, j: (i, 0))

    def vt_call(h_bf, w, nh, name):
        tn = _pick(nh * LANES, (512, 256, 128))
        return _matmul(
            h_bf, w, tm=tm, tn=tn, epilogue=functools.partial(_ep_transpose_heads, t=t),
            out_shape=jax.ShapeDtypeStruct((nh, tokens // t, LANES, t), BF16),
            out_specs=pl.BlockSpec((tn // LANES, tm // t, LANES, t), lambda i, j: (j, i, 0, 0)),
            name=name)

    for layer in range(depth):
        lambda_init = 0.8 - 0.6 * math.exp(-0.3 * layer)
        w = w_in[layer].astype(BF16)

        h = _rms_norm(xf, attn_norm[layer], tm=tm_n, name="attn_norm")

        a_scale = a_qk ** -0.5 * LOG2E
        gain_a = jnp.concatenate([jnp.tile(a_qk_norm[layer, 0] * a_scale, 2 * a_heads),
                                  jnp.tile(a_qk_norm[layer, 1], 2 * a_heads)]).reshape(1, -1)
        tn = 512
        qk_a = _matmul(h, w[:, o_aq:o_av], tm=tm, tn=tn,
                       epilogue=functools.partial(_ep_group_norm, group=a_qk),
                       aux=(gain_a,), aux_specs=(row_spec(tn),),
                       out_shape=jax.ShapeDtypeStruct((tokens, o_av - o_aq), BF16),
                       out_specs=pl.BlockSpec((tm, tn), lambda i, j: (i, j)), name="proj_a_qk")
        vt_a = vt_call(h, w[:, o_av:o_bq], a_heads, "proj_a_v")
        hn = jnp.broadcast_to(a_head_norm[layer].reshape(LANES, 1), (LANES, t))
        o_a = _attention(
            "diff", qk_a, qk_a, vt_a, batch=batch, seq=seq, n_heads=a_heads, t=t, dq=LANES,
            q_col0=0, k_col0=a_heads,
            extra=(bias_tiles, a_lambda[layer], hn),
            extra_specs=(pl.BlockSpec((None, 2, t, t), lambda b, hh, i: (hh, 0, 0, 0)),
                         pl.BlockSpec((4, a_qk), lambda b, hh, i: (0, 0)),
                         pl.BlockSpec((LANES, t), lambda b, hh, i: (0, 0))),
            body_kwargs=dict(qk_dim=a_qk, lambda_init=lambda_init), name="attn_diff")

        b_scale = b_dim ** -0.5 * LOG2E
        gain_b = jnp.concatenate([jnp.tile(b_qk_norm[layer, 0] * b_scale, b_heads),
                                  jnp.tile(b_qk_norm[layer, 1], b_heads)]).reshape(1, -1)
        qk_b = _matmul(h, w[:, o_bq:o_bv], tm=tm, tn=tn,
                       epilogue=functools.partial(_ep_group_norm, group=b_dim),
                       aux=(gain_b,), aux_specs=(row_spec(tn),),
                       out_shape=jax.ShapeDtypeStruct((tokens, o_bv - o_bq), BF16),
                       out_specs=pl.BlockSpec((tm, tn), lambda i, j: (i, j)), name="proj_b_qk")
        vt_b = vt_call(h, w[:, o_bv:o_f], b_heads, "proj_b_v")

        w_tail = jnp.concatenate([w[:, o_kpe:o_end], w[:, o_f:o_cq],
                                  jnp.zeros((d_model, LANES - c_rope - b_heads), BF16)], axis=1)
        gain_tail = jnp.concatenate([c_qk_norm_rope[layer, 1], jnp.zeros((LANES - c_rope,), F32)]).reshape(1, LANES)
        bias_tail = jnp.zeros((LANES,), F32).at[c_rope:c_rope + b_heads].set(b_forget[layer]).reshape(1, LANES)
        kpe, logf = _matmul(
            h, w_tail, tm=tm, tn=LANES,
            epilogue=functools.partial(_ep_tail, rope_dim=c_rope, n_gate=b_heads),
            aux=(gain_tail, bias_tail, cos_t, sin_t), aux_specs=(fixed_row, fixed_row, tok_lane, tok_lane),
            out_shape=[jax.ShapeDtypeStruct((tokens, LANES), BF16), jax.ShapeDtypeStruct((tokens, LANES), F32)],
            out_specs=[tok_lane, tok_lane], name="proj_tail")
        negcum = _neg_cumsum(logf, batch=batch, seq=seq, n_heads=b_heads, lane0=c_rope, tb=_pick(seq, (256, 128)))
        o_b = _attention(
            "forget", qk_b, qk_b, vt_b, batch=batch, seq=seq, n_heads=b_heads, t=t, dq=LANES,
            q_col0=0, k_col0=b_heads,
            extra=(negcum,),
            extra_specs=(pl.BlockSpec((None, None, seq, LANES), lambda b, hh, i: (b, hh, 0, 0)),),
            name="attn_forget")

        n_lat = c_q_lora + c_kv_lora
        gain_lat = jnp.concatenate([c_q_a_norm[layer], c_kv_a_norm[layer]]).reshape(1, n_lat)
        tm_lat = _pick(tokens, (512, 256))
        lat = _matmul(h, w[:, o_cq:o_kpe], tm=tm_lat, tn=n_lat,
                      epilogue=functools.partial(_ep_latent_norms, n_q=c_q_lora),
                      aux=(gain_lat,), aux_specs=(row_spec(n_lat),),
                      out_shape=jax.ShapeDtypeStruct((tokens, n_lat), BF16),
                      out_specs=pl.BlockSpec((tm_lat, n_lat), lambda i, j: (i, j)), name="proj_c_latent")
        c_scale = (c_nope + c_rope) ** -0.5 * LOG2E
        wq = c_w_uq[layer].astype(BF16).reshape(c_q_lora, c_heads, c_nope + c_rope)
        wq = jnp.concatenate([wq, jnp.zeros((c_q_lora, c_heads, 2 * LANES - c_nope - c_rope), BF16)], axis=2)
        wq = wq.reshape(c_q_lora, c_heads * 2 * LANES)
        gq_nope = (c_qk_norm_nope[layer, 0] * c_scale).reshape(1, LANES)
        gq_rope = jnp.concatenate([c_qk_norm_rope[layer, 0] * c_scale, jnp.zeros((LANES - c_rope,), F32)]).reshape(1, LANES)
        tn_q = 512
        q_c = _matmul(lat[:, :c_q_lora], wq, tm=tm, tn=tn_q, epilogue=_ep_latent_q,
                      aux=(gq_nope, gq_rope, cos_t, sin_t), aux_specs=(fixed_row, fixed_row, tok_lane, tok_lane),
                      out_shape=jax.ShapeDtypeStruct((tokens, c_heads * 2 * LANES), BF16),
                      out_specs=pl.BlockSpec((tm, tn_q), lambda i, j: (i, j)), name="proj_c_q")
        wkv = c_w_ukv[layer].astype(BF16).reshape(c_kv_lora, c_heads, c_nope + c_v)
        wk = wkv[:, :, :c_nope].reshape(c_kv_lora, c_heads * c_nope)
        wv = wkv[:, :, c_nope:].reshape(c_kv_lora, c_heads * c_v)
        gk_nope = c_qk_norm_nope[layer, 1].reshape(1, LANES)
        tn_k = 256
        k_c = _matmul(lat[:, c_q_lora:], wk, tm=tm, tn=tn_k, epilogue=_ep_latent_k,
                      aux=(gk_nope, kpe), aux_specs=(fixed_row, tok_lane),
                      out_shape=jax.ShapeDtypeStruct((tokens, c_heads * 2 * LANES), BF16),
                      out_specs=pl.BlockSpec((tm, 2 * tn_k), lambda i, j: (i, j)), name="proj_c_k")
        vt_c = vt_call(lat[:, c_q_lora:], wv, c_heads, "proj_c_v")
        o_c = _attention(
            "latent", q_c, k_c, vt_c, batch=batch, seq=seq, n_heads=c_heads, t=t, dq=2 * LANES,
            q_col0=0, k_col0=0, name="attn_latent")

        mix = jnp.concatenate([o_a, o_b, o_c], axis=1)
        tn_o = 512
        res_spec = pl.BlockSpec((tm, tn_o), lambda i, j: (i, j))
        xf = _matmul(mix, w_out[layer].astype(BF16), tm=tm, tn=tn_o, epilogue=_ep_residual,
                     aux=(xf,), aux_specs=(res_spec,),
                     out_shape=jax.ShapeDtypeStruct((tokens, d_model), F32), out_specs=res_spec,
                     name="out_proj")

        idx = layer // 2
        th = 256
        if layer % 2 == 0:
            h2 = _rms_norm(xf, ffn_norm[layer], tm=tm_n, name="ffn_norm")
            w_gu = _interleave_gate_up(dense_w_gate[idx].astype(BF16), dense_w_up[idx].astype(BF16), th)
            f = dense_w_gate.shape[-1]
            act = _matmul(h2, w_gu, tm=tm, tn=2 * th, epilogue=_ep_swiglu,
                          out_shape=jax.ShapeDtypeStruct((tokens, f), BF16),
                          out_specs=pl.BlockSpec((tm, th), lambda i, j: (i, j)), name="ffn_up")
            tm_d = _pick(tokens, (512, 256))
            res_d = pl.BlockSpec((tm_d, tn_o), lambda i, j: (i, j))
            xf = _matmul(act, dense_w_down[idx].astype(BF16), tm=tm_d, tn=tn_o, epilogue=_ep_residual,
                         aux=(xf,), aux_specs=(res_d,),
                         out_shape=jax.ShapeDtypeStruct((tokens, d_model), F32), out_specs=res_d,
                         name="ffn_down")
        else:
            h2, comb = _rms_norm_router(xf, ffn_norm[layer], moe_router[idx], tm=tm_n, name="ffn_norm_router")
            f = moe_w_gate.shape[-1]
            for e in range(n_exp):
                w_gu = _interleave_gate_up(moe_w_gate[idx, e].astype(BF16), moe_w_up[idx, e].astype(BF16), th)
                act = _matmul(h2, w_gu, tm=tm, tn=2 * th, epilogue=_ep_swiglu_scaled,
                              aux=(comb,), aux_specs=(pl.BlockSpec((tm, LANES), functools.partial(_expert_col, e)),),
                              out_shape=jax.ShapeDtypeStruct((tokens, f), BF16),
                              out_specs=pl.BlockSpec((tm, th), lambda i, j: (i, j)), name="moe_up")
                xf = _matmul(act, moe_w_down[idx, e].astype(BF16), tm=tm, tn=tn_o, epilogue=_ep_residual,
                             aux=(xf,), aux_specs=(res_spec,),
                             out_shape=jax.ShapeDtypeStruct((tokens, d_model), F32), out_specs=res_spec,
                             name="moe_down")

    return xf.reshape(batch, seq, d_model)
```

```python
import functools
import math

import jax
import jax.numpy as jnp
from jax import lax
from jax.experimental import pallas as pl
from jax.experimental.pallas import tpu as pltpu

F32 = jnp.float32
BF16 = jnp.bfloat16

LANES = 128
ONES_ROWS = 16
LOG2E = 1.4426950408889634
NEG = -1e30
EPS = 1e-6
VMEM_LIMIT = 52 * 1024 * 1024

NUM_BUCKETS = 32
MAX_DISTANCE = 128
ROPE_BASE = 10000.0
TOP_K = 2

_HI = lax.Precision.HIGHEST


def _cparams(n_axes):
    return pltpu.CompilerParams(
        dimension_semantics=("arbitrary",) * n_axes, vmem_limit_bytes=VMEM_LIMIT)


def _mm_body(a_ref, w_ref, *rest, epilogue, n_aux):
    acc = jnp.dot(a_ref[...], w_ref[...], preferred_element_type=F32)
    epilogue(acc, rest[:n_aux], rest[n_aux:])


def _matmul(a, w, *, tm, tn, epilogue, out_shape, out_specs, aux=(), aux_specs=(), name):
    m, k = a.shape
    n = w.shape[1]
    assert m % tm == 0 and n % tn == 0, (m, n, tm, tn)
    return pl.pallas_call(
        functools.partial(_mm_body, epilogue=epilogue, n_aux=len(aux)),
        grid=(m // tm, n // tn),
        in_specs=[pl.BlockSpec((tm, k), lambda i, j: (i, 0)),
                  pl.BlockSpec((k, tn), lambda i, j: (0, j)), *aux_specs],
        out_specs=out_specs, out_shape=out_shape,
        compiler_params=_cparams(2), name=name)(a, w, *aux)


def _sumsq_lanes(x):
    return jnp.sum(x * x, axis=-1, keepdims=True)


def _ep_residual(acc, aux, outs):
    outs[0][...] = aux[0][...] + acc


def _ep_swiglu(acc, aux, outs):
    th = acc.shape[1] // 2
    g = acc[:, :th]
    u = acc[:, th:]
    outs[0][...] = (g * (1.0 / (1.0 + jnp.exp(-g))) * u).astype(outs[0].dtype)


def _ep_swiglu_scaled(acc, aux, outs):
    th = acc.shape[1] // 2
    g = acc[:, :th]
    u = acc[:, th:]
    y = g * (1.0 / (1.0 + jnp.exp(-g))) * u
    c = aux[0][...]
    outs[0][...] = (y * jnp.tile(c, (1, th // LANES))).astype(outs[0].dtype)


def _ep_group_norm(acc, aux, outs, *, group):
    gain = aux[0][...]
    tn = acc.shape[1]
    lane = lax.broadcasted_iota(jnp.int32, (1, LANES), 1)
    for c in range(tn // LANES):
        x = acc[:, c * LANES:(c + 1) * LANES]
        g = gain[:, c * LANES:(c + 1) * LANES]
        if group == LANES:
            r = lax.rsqrt(_sumsq_lanes(x) * (1.0 / LANES) + EPS)
        else:
            lo = lane < group
            x2 = x * x
            ss_lo = jnp.sum(jnp.where(lo, x2, 0.0), axis=-1, keepdims=True)
            ss_hi = jnp.sum(jnp.where(lo, 0.0, x2), axis=-1, keepdims=True)
            r = jnp.where(lo, lax.rsqrt(ss_lo * (1.0 / group) + EPS),
                          lax.rsqrt(ss_hi * (1.0 / group) + EPS))
        outs[0][:, c * LANES:(c + 1) * LANES] = (x * r * g).astype(outs[0].dtype)


def _ep_transpose_heads(acc, aux, outs, *, t):
    tm, tn = acc.shape
    ones = jnp.ones((ONES_ROWS, t), outs[0].dtype)
    for hh in range(tn // LANES):
        for s in range(tm // t):
            blk = acc[s * t:(s + 1) * t, hh * LANES:(hh + 1) * LANES]
            outs[0][hh, s, :LANES, :] = blk.T.astype(outs[0].dtype)
            outs[0][hh, s, LANES:, :] = ones


def _ep_latent_norms(acc, aux, outs, *, n_q):
    gain = aux[0][...]
    n = acc.shape[1]
    xq = acc[:, :n_q]
    xk = acc[:, n_q:]
    rq = lax.rsqrt(_sumsq_lanes(xq) * (1.0 / n_q) + EPS)
    rk = lax.rsqrt(_sumsq_lanes(xk) * (1.0 / (n - n_q)) + EPS)
    outs[0][:, :n_q] = (xq * rq * gain[:, :n_q]).astype(outs[0].dtype)
    outs[0][:, n_q:] = (xk * rk * gain[:, n_q:]).astype(outs[0].dtype)


def _rope_half_block(y, cos, sin):
    lane = lax.broadcasted_iota(jnp.int32, (1, LANES), 1)
    ra = pltpu.roll(y, 32, axis=1)
    rb = pltpu.roll(y, 96, axis=1)
    rot = jnp.where(lane < 32, -rb, ra)
    return y * cos + rot * sin


def _ep_tail(acc, aux, outs, *, rope_dim, n_gate):
    gain = aux[0][...]
    bias = aux[1][...]
    cos = aux[2][...]
    sin = aux[3][...]
    shift_col = aux[4][...]
    lane = lax.broadcasted_iota(jnp.int32, (1, LANES), 1)
    is_pe = lane < rope_dim
    xpe = jnp.where(is_pe, acc, 0.0)
    r = lax.rsqrt(_sumsq_lanes(xpe) * (1.0 / rope_dim) + EPS)
    y = xpe * r * gain
    outs[0][...] = (_rope_half_block(y, cos, sin) + shift_col).astype(outs[0].dtype)
    z = acc + bias
    logsig = -(jnp.maximum(-z, 0.0) + jnp.log(1.0 + jnp.exp(-jnp.abs(z))))
    is_gate = (lane >= rope_dim) & (lane < rope_dim + n_gate)
    outs[1][...] = jnp.where(is_gate, logsig, 0.0)


def _ep_latent_q(acc, aux, outs):
    g_nope = aux[0][...]
    g_rope = aux[1][...]
    cos = aux[2][...]
    sin = aux[3][...]
    one_col = aux[4][...]
    for hh in range(acc.shape[1] // (2 * LANES)):
        xn = acc[:, hh * 256:hh * 256 + LANES]
        xr = acc[:, hh * 256 + LANES:(hh + 1) * 256]
        rn = lax.rsqrt(_sumsq_lanes(xn) * (1.0 / LANES) + EPS)
        outs[0][:, hh * 256:hh * 256 + LANES] = (xn * rn * g_nope).astype(outs[0].dtype)
        rr = lax.rsqrt(_sumsq_lanes(xr) * (1.0 / 64) + EPS)
        yr = _rope_half_block(xr * rr * g_rope, cos, sin) + one_col
        outs[0][:, hh * 256 + LANES:(hh + 1) * 256] = yr.astype(outs[0].dtype)


def _ep_latent_k(acc, aux, outs):
    g_nope = aux[0][...]
    kpe = aux[1][...]
    for hh in range(acc.shape[1] // LANES):
        xn = acc[:, hh * LANES:(hh + 1) * LANES]
        rn = lax.rsqrt(_sumsq_lanes(xn) * (1.0 / LANES) + EPS)
        outs[0][:, hh * 256:hh * 256 + LANES] = (xn * rn * g_nope).astype(outs[0].dtype)
        outs[0][:, hh * 256 + LANES:(hh + 1) * 256] = kpe


def _norm_body(x_ref, g_ref, h_ref):
    x = x_ref[...]
    d = x.shape[1]
    r = lax.rsqrt(_sumsq_lanes(x) * (1.0 / d) + EPS)
    h_ref[...] = (x * r * g_ref[...]).astype(h_ref.dtype)


def _rms_norm(x, gain, *, tm, name):
    t, d = x.shape
    return pl.pallas_call(
        _norm_body, grid=(t // tm,),
        in_specs=[pl.BlockSpec((tm, d), lambda i: (i, 0)), pl.BlockSpec((1, d), lambda i: (0, 0))],
        out_specs=pl.BlockSpec((tm, d), lambda i: (i, 0)),
        out_shape=jax.ShapeDtypeStruct((t, d), BF16),
        compiler_params=_cparams(1), name=name)(x, gain.reshape(1, d))


def _norm_router_body(x_ref, g_ref, wr_ref, h_ref, comb_ref, *, n_exp):
    x = x_ref[...]
    d = x.shape[1]
    r = lax.rsqrt(_sumsq_lanes(x) * (1.0 / d) + EPS)
    h = x * r * g_ref[...]
    h_ref[...] = h.astype(h_ref.dtype)
    logits = jnp.dot(h, wr_ref[...], preferred_element_type=F32, precision=_HI)
    lane = lax.broadcasted_iota(jnp.int32, logits.shape, 1).astype(F32)
    lg = jnp.where(lane < n_exp, logits, -jnp.inf)
    m1 = jnp.max(lg, axis=-1, keepdims=True)
    i1 = jnp.min(jnp.where(lg == m1, lane, float(LANES)), axis=-1, keepdims=True)
    sel1 = lane == i1
    lg2 = jnp.where(sel1, -jnp.inf, lg)
    m2 = jnp.max(lg2, axis=-1, keepdims=True)
    i2 = jnp.min(jnp.where(lg2 == m2, lane, float(LANES)), axis=-1, keepdims=True)
    sel2 = lane == i2
    e2 = jnp.exp(m2 - m1)
    g1 = 1.0 / (1.0 + e2)
    g2 = e2 * g1
    for e in range(n_exp):
        ce = jnp.where(i1 == float(e), g1, 0.0) + jnp.where(i2 == float(e), g2, 0.0)
        comb_ref[:, e * LANES:(e + 1) * LANES] = jnp.broadcast_to(ce, (x.shape[0], LANES))


def _rms_norm_router(x, gain, router, *, tm, name):
    t, d = x.shape
    n_exp = router.shape[1]
    wr = jnp.zeros((d, LANES), F32).at[:, :n_exp].set(router.astype(F32))
    return pl.pallas_call(
        functools.partial(_norm_router_body, n_exp=n_exp), grid=(t // tm,),
        in_specs=[pl.BlockSpec((tm, d), lambda i: (i, 0)), pl.BlockSpec((1, d), lambda i: (0, 0)),
                  pl.BlockSpec((d, LANES), lambda i: (0, 0))],
        out_specs=[pl.BlockSpec((tm, d), lambda i: (i, 0)),
                   pl.BlockSpec((tm, n_exp * LANES), lambda i: (i, 0))],
        out_shape=[jax.ShapeDtypeStruct((t, d), BF16),
                   jax.ShapeDtypeStruct((t, n_exp * LANES), F32)],
        compiler_params=_cparams(1), name=name)(x, gain.reshape(1, d), wr)


def _rope_body(pos_ref, f_ref, cos_ref, sin_ref):
    ang = pos_ref[...] * f_ref[...]
    live = f_ref[...] > 0.0
    cos_ref[...] = jnp.where(live, jnp.cos(ang), 0.0)
    sin_ref[...] = jnp.where(live, jnp.sin(ang), 0.0)


def _rope_tables(positions, *, half, tm):
    t = positions.size
    inv_freq = ROPE_BASE ** (-jnp.arange(half, dtype=F32) / half)
    f_row = jnp.concatenate([inv_freq, inv_freq, jnp.zeros((LANES - 2 * half,), F32)]).reshape(1, LANES)
    pos_rep = jnp.broadcast_to(positions.astype(F32).reshape(t, 1), (t, LANES))
    spec = pl.BlockSpec((tm, LANES), lambda i: (i, 0))
    return pl.pallas_call(
        _rope_body, grid=(t // tm,),
        in_specs=[spec, pl.BlockSpec((1, LANES), lambda i: (0, 0))],
        out_specs=[spec, spec],
        out_shape=[jax.ShapeDtypeStruct((t, LANES), F32)] * 2,
        compiler_params=_cparams(1), name="rope_tables")(pos_rep, f_row)


def _bias_body(rb_ref, o_ref, *, t):
    h = pl.program_id(0)
    d = pl.program_id(1)
    j = lax.broadcasted_iota(jnp.int32, (t, t), 0)
    i = lax.broadcasted_iota(jnp.int32, (t, t), 1)
    rel = d * t + i - j
    n = jnp.maximum(rel, 0)
    max_exact = NUM_BUCKETS // 2
    large = max_exact + (
        jnp.log(jnp.maximum(n, max_exact).astype(F32) / max_exact)
        / math.log(MAX_DISTANCE / max_exact) * (NUM_BUCKETS - max_exact)
    ).astype(jnp.int32)
    large = jnp.minimum(large, NUM_BUCKETS - 1)
    bucket = jnp.where(n < max_exact, n, large)
    far = rb_ref[NUM_BUCKETS - 1, h]
    val = jnp.zeros((t, t), F32)
    for b in range(NUM_BUCKETS - 1):
        val = jnp.where(bucket == b, rb_ref[b, h] - far, val)
    o_ref[...] = jnp.where(rel >= 0, val * LOG2E, NEG)


def _bias_tiles(rel_bias, *, t):
    assert t + 1 >= MAX_DISTANCE, "tiles beyond the first sub-diagonal must be in the last bucket"
    nh = rel_bias.shape[1]
    return pl.pallas_call(
        functools.partial(_bias_body, t=t), grid=(nh, 2),
        in_specs=[pl.BlockSpec(memory_space=pltpu.SMEM)],
        out_specs=pl.BlockSpec((None, None, t, t), lambda h, d: (h, d, 0, 0)),
        out_shape=jax.ShapeDtypeStruct((nh, 2, t, t), F32),
        compiler_params=_cparams(2), name="t5_bias_tiles")(rel_bias.astype(F32))


N_PIECES = 3


def _split_pieces(x):
    pieces = []
    rest = x
    for _ in range(N_PIECES):
        p = rest.astype(BF16).astype(F32)
        pieces.append(p)
        rest = rest - p
    return pieces


def _cum_body(shift_ref, lf_ref, o_ref, carry_ref, *, n_heads, lane0):
    @pl.when(pl.program_id(1) == 0)
    def _():
        carry_ref[...] = jnp.zeros_like(carry_ref)

    x = lf_ref[...]
    tb = x.shape[0]
    r = lax.broadcasted_iota(jnp.int32, (tb, tb), 0)
    c = lax.broadcasted_iota(jnp.int32, (tb, tb), 1)
    tri = jnp.where(r >= c, 1.0, 0.0).astype(F32)
    er = lax.broadcasted_iota(jnp.int32, (LANES, LANES), 0)
    lane = lax.broadcasted_iota(jnp.int32, (tb, LANES), 1)
    shift = shift_ref[0]
    for hh in range(n_heads):
        sel = jnp.where(er == lane0 + hh, 1.0, 0.0).astype(F32)
        xh = jnp.dot(x, sel, preferred_element_type=F32, precision=_HI)
        cum = jnp.dot(tri, xh, preferred_element_type=F32, precision=_HI) + carry_ref[hh:hh + 1, :]
        carry_ref[hh:hh + 1, :] = cum[tb - 1:tb, :]
        cum2 = cum * LOG2E
        qcols = jnp.where(lane < N_PIECES, 1.0, 0.0)
        kcols = jnp.where((lane >= N_PIECES) & (lane < 2 * N_PIECES), 1.0, 0.0)
        for n, (pq, pk) in enumerate(zip(_split_pieces(cum2 - shift), _split_pieces(-cum2))):
            qcols = jnp.where(lane == N_PIECES + n, pq, qcols)
            kcols = jnp.where(lane == n, pk, kcols)
        o_ref[:, hh * LANES:(hh + 1) * LANES] = qcols.astype(o_ref.dtype)
        o_ref[:, (n_heads + hh) * LANES:(n_heads + hh + 1) * LANES] = kcols.astype(o_ref.dtype)


def _forget_columns(logf, shift, *, batch, seq, n_heads, lane0, tb):
    nb = seq // tb
    width = 2 * n_heads * LANES
    return pl.pallas_call(
        functools.partial(_cum_body, n_heads=n_heads, lane0=lane0), grid=(batch, nb),
        in_specs=[pl.BlockSpec(memory_space=pltpu.SMEM),
                  pl.BlockSpec((tb, LANES), lambda b, i: (b * nb + i, 0))],
        out_specs=pl.BlockSpec((tb, width), lambda b, i: (b * nb + i, 0)),
        out_shape=jax.ShapeDtypeStruct((batch * seq, width), BF16),
        scratch_shapes=[pltpu.VMEM((n_heads, LANES), F32)],
        compiler_params=_cparams(2), name="forget_cumsum")(shift.reshape(1).astype(F32), logf)


def _online_block(s, vb, m_ref, acc_ref):
    m_prev = m_ref[...]
    m_new = jnp.maximum(m_prev, jnp.max(s, axis=0, keepdims=True))
    alpha = jnp.exp2(m_prev - m_new)
    p = jnp.exp2(s - m_new).astype(BF16)
    acc_ref[...] = alpha * acc_ref[...] + jnp.dot(vb, p, preferred_element_type=F32)
    m_ref[...] = m_new


def _shifted_block(s, vb, acc_ref):
    p = jnp.exp2(s).astype(BF16)
    acc_ref[...] += jnp.dot(vb, p, preferred_element_type=F32)


def _causal_mask(t):
    j = lax.broadcasted_iota(jnp.int32, (t, t), 0)
    i = lax.broadcasted_iota(jnp.int32, (t, t), 1)
    return j <= i


_NT = (((1,), (1,)), ((), ()))
FAR_UNROLL = 2


def _init_state(m_ref, acc_ref):
    m_ref[...] = jnp.full(m_ref.shape, NEG, F32)
    acc_ref[...] = jnp.zeros(acc_ref.shape, F32)


def _attn_body(*refs, mode, t, hp, dq, dv, qk_dim=None, lambda_init=None):
    safe_ref, q_ref, k_ref, vt_ref = refs[:4]
    if mode == "latent":
        o_ref, qc_ref, m_ref, acc_ref = refs[4:]
    elif mode == "forget":
        qx_ref, kx_ref, o_ref, qc_ref, m_ref, acc_ref = refs[4:]
    else:
        qx_ref, kx_ref, bias_ref, lam_ref, hn_ref, o_ref, qc_ref, m_ref, acc_ref = refs[4:]
    n_map = 2 if mode == "diff" else 1
    qi = pl.program_id(2)
    _init_state(m_ref, acc_ref)

    for hh in range(hp):
        q = q_ref[:, hh * dq:(hh + 1) * dq]
        if mode == "latent":
            qc_ref[hh] = q
        elif mode == "forget":
            qc_ref[hh] = jnp.concatenate([q, qx_ref[:, hh * LANES:(hh + 1) * LANES]], axis=1)
        else:
            lane = lax.broadcasted_iota(jnp.int32, (t, dq), 1)
            zero = jnp.zeros_like(q)
            qc_ref[2 * hh] = jnp.concatenate([jnp.where(lane < qk_dim, q, zero), qx_ref[...]], axis=1)
            qc_ref[2 * hh + 1] = jnp.concatenate([jnp.where(lane < qk_dim, zero, q), qx_ref[...]], axis=1)

    def keys(kj, hh):
        start = pl.multiple_of(kj * t, t)
        kb = k_ref[pl.ds(start, t), hh * dq:(hh + 1) * dq]
        if mode == "forget":
            kb = jnp.concatenate([kb, kx_ref[pl.ds(start, t), hh * LANES:(hh + 1) * LANES]], axis=1)
        elif mode == "diff":
            kb = jnp.concatenate([kb, kx_ref[hh]], axis=1)
        return kb

    def run(update):
        def block(kj, kind):
            for hh in range(hp):
                kb = keys(kj, hh)
                vb = vt_ref[hh, kj]
                for mm in range(n_map):
                    c = n_map * hh + mm
                    s = lax.dot_general(kb, qc_ref[c], _NT, preferred_element_type=F32)
                    if mode == "diff":
                        if kind != "far":
                            s = s + bias_ref[hh, 0 if kind == "diag" else 1]
                    elif kind == "diag":
                        s = jnp.where(_causal_mask(t), s, NEG)
                    update(s, vb, c)

        def far_group(g, carry):
            for u in range(FAR_UNROLL):
                block(g * FAR_UNROLL + u, "far")
            return carry

        n_far = jnp.maximum(qi - 1, 0) if mode == "diff" else qi
        n_groups = n_far // FAR_UNROLL
        lax.fori_loop(0, n_groups, far_group, 0)
        for u in range(FAR_UNROLL - 1):
            @pl.when(n_groups * FAR_UNROLL + u < n_far)
            def _(u=u):
                block(n_groups * FAR_UNROLL + u, "far")

        if mode == "diff":
            @pl.when(qi >= 1)
            def _():
                block(qi - 1, "sub")
        block(qi, "diag")

    @pl.when(safe_ref[0] != 0)
    def _():
        run(lambda s, vb, c: _shifted_block(s, vb, acc_ref.at[c]))

    @pl.when(safe_ref[0] == 0)
    def _():
        run(lambda s, vb, c: _online_block(s, vb, m_ref.at[c], acc_ref.at[c]))

    if mode == "diff":
        lp = lam_ref[...]
        lam = (jnp.exp(jnp.sum(lp[0:1] * lp[1:2], axis=-1, keepdims=True))
               - jnp.exp(jnp.sum(lp[2:3] * lp[3:4], axis=-1, keepdims=True)) + lambda_init)
    for hh in range(hp):
        if mode == "diff":
            a0 = acc_ref[2 * hh]
            a1 = acc_ref[2 * hh + 1]
            o = a0[:dv] * (1.0 / a0[dv:dv + 1]) - lam * (a1[:dv] * (1.0 / a1[dv:dv + 1]))
            r = lax.rsqrt(jnp.sum(o * o, axis=0, keepdims=True) * (1.0 / dv) + EPS)
            o = o * r * hn_ref[...] * (1.0 - lambda_init)
        else:
            a = acc_ref[hh]
            o = a[:dv] * (1.0 / a[dv:dv + 1])
        o_ref[:, hh * dv:(hh + 1) * dv] = o.T.astype(o_ref.dtype)


def _attention(mode, safe, q_arr, k_arr, vt_arr, *, batch, seq, n_heads, t, hp, dq, q_col0, k_col0,
               extra=(), extra_specs=(), body_kwargs=None, name):
    nq = seq // t
    dvp = vt_arr.shape[2]
    dv = dvp - ONES_ROWS
    tokens = batch * seq
    assert n_heads % hp == 0 and q_col0 % hp == 0 and k_col0 % hp == 0
    qc, kc = q_col0 // hp, k_col0 // hp
    in_specs = [
        pl.BlockSpec(memory_space=pltpu.SMEM),
        pl.BlockSpec((t, hp * dq), lambda b, h, i: (b * nq + i, qc + h)),
        pl.BlockSpec((seq, hp * dq), lambda b, h, i: (b, kc + h)),
        pl.BlockSpec((hp, nq, dvp, t), lambda b, h, i: (h, b, 0, 0)),
        *extra_specs,
    ]
    n_chain = 2 * hp if mode == "diff" else hp
    scratch = [pltpu.VMEM((n_chain, t, 2 * LANES), BF16),
               pltpu.VMEM((n_chain, 1, t), F32), pltpu.VMEM((n_chain, dvp, t), F32)]
    body = functools.partial(_attn_body, mode=mode, t=t, hp=hp, dq=dq, dv=dv, **(body_kwargs or {}))
    return pl.pallas_call(
        body, grid=(batch, n_heads // hp, nq), in_specs=in_specs,
        out_specs=pl.BlockSpec((t, hp * dv), lambda b, h, i: (b * nq + i, h)),
        out_shape=jax.ShapeDtypeStruct((tokens, n_heads * dv), BF16),
        scratch_shapes=scratch, compiler_params=_cparams(3), name=name)(safe, q_arr, k_arr, vt_arr, *extra)


def _pick(n, prefs):
    for p in prefs:
        if n % p == 0:
            return p
    raise ValueError((n, prefs))


def _expert_col(e, i, j):
    return (i, e)


SAFE_DEPTH = 110.0


def _round_up_bf16(c):
    return (c * (1.0 + 2.0 ** -6)).astype(BF16).astype(F32)


def _safe_flag(depth):
    return (depth <= SAFE_DEPTH).astype(jnp.int32).reshape(1)


def _interleave_gate_up(wg, wu, th):
    k, f = wg.shape
    return jnp.concatenate([wg.reshape(k, f // th, th), wu.reshape(k, f // th, th)], axis=2).reshape(k, 2 * f)


def kernel(x, positions, attn_norm, w_in, b_forget, a_qk_norm, a_lambda, a_head_norm, rel_bias,
           b_qk_norm, c_q_a_norm, c_kv_a_norm, c_w_uq, c_w_ukv, c_qk_norm_nope, c_qk_norm_rope,
           w_out, ffn_norm, dense_w_gate, dense_w_up, dense_w_down, moe_router, moe_w_gate,
           moe_w_up, moe_w_down):
    batch, seq, d_model = x.shape
    depth = w_in.shape[0]
    tokens = batch * seq
    n_slots = d_model // LANES
    a_heads = n_slots // 4
    b_heads = n_slots // 4
    c_heads = n_slots // 2
    a_qk = a_qk_norm.shape[-1]
    b_dim = b_qk_norm.shape[-1]
    c_q_lora = c_q_a_norm.shape[-1]
    c_kv_lora = c_kv_a_norm.shape[-1]
    c_nope = c_qk_norm_nope.shape[-1]
    c_rope = c_qk_norm_rope.shape[-1]
    c_v = c_w_ukv.shape[-1] // c_heads - c_nope
    n_exp = moe_router.shape[-1]
    assert 2 * a_qk == LANES and b_dim == LANES and c_nope == LANES and c_v == LANES and 2 * c_rope == LANES

    t = _pick(seq, (512, 256, 128))
    tm = _pick(tokens, (1024, 512, 256))
    tm_n = _pick(tokens, (256, 128))
    hp_a, hp_b, hp_c = 1, 2, 2

    sizes = (a_heads * 2 * a_qk, a_heads * 2 * a_qk, a_heads * LANES, b_heads * b_dim, b_heads * b_dim,
             b_heads * b_dim, b_heads, c_q_lora, c_kv_lora, c_rope)
    offs = [0]
    for s in sizes:
        offs.append(offs[-1] + s)
    o_aq, o_ak, o_av, o_bq, o_bk, o_bv, o_f, o_cq, o_ckv, o_kpe, o_end = offs

    xf = x.reshape(tokens, d_model)
    cos_t, sin_t = _rope_tables(positions, half=c_rope // 2, tm=tm)
    bias_tiles = _bias_tiles(rel_bias, t=t)
    bias_log2 = (rel_bias.astype(F32) - rel_bias[-1:].astype(F32)) * LOG2E
    bias_max = jnp.max(bias_log2, axis=0)
    qx_a = jnp.zeros((t, LANES), F32).at[:, 0].set(1.0).astype(BF16)

    row_spec = lambda width: pl.BlockSpec((1, width), lambda i, j: (0, j))
    fixed_row = pl.BlockSpec((1, LANES), lambda i, j: (0, 0))
    tok_lane = pl.BlockSpec((tm, LANES), lambda i, j: (i, 0))

    def vt_call(h_bf, w, nh, name):
        tn = _pick(nh * LANES, (512, 256, 128))
        return _matmul(
            h_bf, w, tm=tm, tn=tn, epilogue=functools.partial(_ep_transpose_heads, t=t),
            out_shape=jax.ShapeDtypeStruct((nh, tokens // t, LANES + ONES_ROWS, t), BF16),
            out_specs=pl.BlockSpec((tn // LANES, tm // t, LANES + ONES_ROWS, t), lambda i, j: (j, i, 0, 0)),
            name=name)

    for layer in range(depth):
        lambda_init = 0.8 - 0.6 * math.exp(-0.3 * layer)
        w = w_in[layer].astype(BF16)

        h = _rms_norm(xf, attn_norm[layer], tm=tm_n, name="attn_norm")

        a_scale = a_qk ** -0.5 * LOG2E
        gain_a = jnp.concatenate([jnp.tile(a_qk_norm[layer, 0] * a_scale, 2 * a_heads),
                                  jnp.tile(a_qk_norm[layer, 1], 2 * a_heads)]).reshape(1, -1)
        tn = 512
        qk_a = _matmul(h, w[:, o_aq:o_av], tm=tm, tn=tn,
                       epilogue=functools.partial(_ep_group_norm, group=a_qk),
                       aux=(gain_a,), aux_specs=(row_spec(tn),),
                       out_shape=jax.ShapeDtypeStruct((tokens, o_av - o_aq), BF16),
                       out_specs=pl.BlockSpec((tm, tn), lambda i, j: (i, j)), name="proj_a_qk")
        vt_a = vt_call(h, w[:, o_av:o_bq], a_heads, "proj_a_v")
        hn = jnp.broadcast_to(a_head_norm[layer].reshape(LANES, 1), (LANES, t))
        bound_a = a_qk * jnp.max(jnp.abs(a_qk_norm[layer, 0] * a_scale)) * jnp.max(jnp.abs(a_qk_norm[layer, 1]))
        shift_a = _round_up_bf16(bound_a + bias_max)
        safe_a = _safe_flag(2.0 * bound_a + jnp.max(bias_max - bias_log2[0]))
        kx_a = jnp.zeros((a_heads, t, LANES), F32).at[:, :, 0].set(-shift_a[:, None]).astype(BF16)
        o_a = _attention(
            "diff", safe_a, qk_a, qk_a, vt_a, batch=batch, seq=seq, n_heads=a_heads, t=t, hp=hp_a, dq=LANES,
            q_col0=0, k_col0=a_heads,
            extra=(qx_a, kx_a, bias_tiles, a_lambda[layer], hn),
            extra_specs=(pl.BlockSpec((t, LANES), lambda b, hh, i: (0, 0)),
                         pl.BlockSpec((hp_a, t, LANES), lambda b, hh, i: (hh, 0, 0)),
                         pl.BlockSpec((hp_a, 2, t, t), lambda b, hh, i: (hh, 0, 0, 0)),
                         pl.BlockSpec((4, a_qk), lambda b, hh, i: (0, 0)),
                         pl.BlockSpec((LANES, t), lambda b, hh, i: (0, 0))),
            body_kwargs=dict(qk_dim=a_qk, lambda_init=lambda_init), name="attn_diff")

        b_scale = b_dim ** -0.5 * LOG2E
        gain_b = jnp.concatenate([jnp.tile(b_qk_norm[layer, 0] * b_scale, b_heads),
                                  jnp.tile(b_qk_norm[layer, 1], b_heads)]).reshape(1, -1)
        qk_b = _matmul(h, w[:, o_bq:o_bv], tm=tm, tn=tn,
                       epilogue=functools.partial(_ep_group_norm, group=b_dim),
                       aux=(gain_b,), aux_specs=(row_spec(tn),),
                       out_shape=jax.ShapeDtypeStruct((tokens, o_bv - o_bq), BF16),
                       out_specs=pl.BlockSpec((tm, tn), lambda i, j: (i, j)), name="proj_b_qk")
        vt_b = vt_call(h, w[:, o_bv:o_f], b_heads, "proj_b_v")

        w_tail = jnp.concatenate([w[:, o_kpe:o_end], w[:, o_f:o_cq],
                                  jnp.zeros((d_model, LANES - c_rope - b_heads), BF16)], axis=1)
        gain_tail = jnp.concatenate([c_qk_norm_rope[layer, 1], jnp.zeros((LANES - c_rope,), F32)]).reshape(1, LANES)
        bias_tail = jnp.zeros((LANES,), F32).at[c_rope:c_rope + b_heads].set(b_forget[layer]).reshape(1, LANES)
        c_scale = (c_nope + c_rope) ** -0.5 * LOG2E
        gq_n = c_qk_norm_nope[layer, 0] * c_scale
        gq_r = c_qk_norm_rope[layer, 0] * c_scale
        gk_n = c_qk_norm_nope[layer, 1]
        gk_r = c_qk_norm_rope[layer, 1]
        bound_c = (jnp.sqrt(c_nope * jnp.max(gq_n * gq_n) + c_rope * jnp.max(gq_r * gq_r))
                   * jnp.sqrt(c_nope * jnp.max(gk_n * gk_n) + c_rope * jnp.max(gk_r * gk_r)))
        shift_c = _round_up_bf16(bound_c)
        safe_c = _safe_flag(2.0 * bound_c)
        shift_col = jnp.zeros((1, LANES), F32).at[0, c_rope].set(-shift_c)
        one_col = jnp.zeros((1, LANES), F32).at[0, c_rope].set(1.0)
        kpe, logf = _matmul(
            h, w_tail, tm=tm, tn=LANES,
            epilogue=functools.partial(_ep_tail, rope_dim=c_rope, n_gate=b_heads),
            aux=(gain_tail, bias_tail, cos_t, sin_t, shift_col),
            aux_specs=(fixed_row, fixed_row, tok_lane, tok_lane, fixed_row),
            out_shape=[jax.ShapeDtypeStruct((tokens, LANES), BF16), jax.ShapeDtypeStruct((tokens, LANES), F32)],
            out_specs=[tok_lane, tok_lane], name="proj_tail")
        bound_b = b_dim * jnp.max(jnp.abs(b_qk_norm[layer, 0] * b_scale)) * jnp.max(jnp.abs(b_qk_norm[layer, 1]))
        safe_b = _safe_flag(2.0 * bound_b)
        x_b = _forget_columns(logf, bound_b, batch=batch, seq=seq, n_heads=b_heads, lane0=c_rope,
                              tb=_pick(seq, (256, 128)))
        nbh = b_heads // hp_b
        o_b = _attention(
            "forget", safe_b, qk_b, qk_b, vt_b, batch=batch, seq=seq, n_heads=b_heads, t=t, hp=hp_b, dq=LANES,
            q_col0=0, k_col0=b_heads,
            extra=(x_b, x_b),
            extra_specs=(pl.BlockSpec((t, hp_b * LANES), lambda b, hh, i: (b * (seq // t) + i, hh)),
                         pl.BlockSpec((seq, hp_b * LANES), lambda b, hh, i: (b, nbh + hh))),
            name="attn_forget")

        n_lat = c_q_lora + c_kv_lora
        gain_lat = jnp.concatenate([c_q_a_norm[layer], c_kv_a_norm[layer]]).reshape(1, n_lat)
        tm_lat = _pick(tokens, (512, 256))
        lat = _matmul(h, w[:, o_cq:o_kpe], tm=tm_lat, tn=n_lat,
                      epilogue=functools.partial(_ep_latent_norms, n_q=c_q_lora),
                      aux=(gain_lat,), aux_specs=(row_spec(n_lat),),
                      out_shape=jax.ShapeDtypeStruct((tokens, n_lat), BF16),
                      out_specs=pl.BlockSpec((tm_lat, n_lat), lambda i, j: (i, j)), name="proj_c_latent")
        wq = c_w_uq[layer].astype(BF16).reshape(c_q_lora, c_heads, c_nope + c_rope)
        wq = jnp.concatenate([wq, jnp.zeros((c_q_lora, c_heads, 2 * LANES - c_nope - c_rope), BF16)], axis=2)
        wq = wq.reshape(c_q_lora, c_heads * 2 * LANES)
        gq_nope = gq_n.reshape(1, LANES)
        gq_rope = jnp.concatenate([gq_r, jnp.zeros((LANES - c_rope,), F32)]).reshape(1, LANES)
        tn_q = 512
        q_c = _matmul(lat[:, :c_q_lora], wq, tm=tm, tn=tn_q, epilogue=_ep_latent_q,
                      aux=(gq_nope, gq_rope, cos_t, sin_t, one_col),
                      aux_specs=(fixed_row, fixed_row, tok_lane, tok_lane, fixed_row),
                      out_shape=jax.ShapeDtypeStruct((tokens, c_heads * 2 * LANES), BF16),
                      out_specs=pl.BlockSpec((tm, tn_q), lambda i, j: (i, j)), name="proj_c_q")
        wkv = c_w_ukv[layer].astype(BF16).reshape(c_kv_lora, c_heads, c_nope + c_v)
        wk = wkv[:, :, :c_nope].reshape(c_kv_lora, c_heads * c_nope)
        wv = wkv[:, :, c_nope:].reshape(c_kv_lora, c_heads * c_v)
        gk_nope = c_qk_norm_nope[layer, 1].reshape(1, LANES)
        tn_k = 256
        k_c = _matmul(lat[:, c_q_lora:], wk, tm=tm, tn=tn_k, epilogue=_ep_latent_k,
                      aux=(gk_nope, kpe), aux_specs=(fixed_row, tok_lane),
                      out_shape=jax.ShapeDtypeStruct((tokens, c_heads * 2 * LANES), BF16),
                      out_specs=pl.BlockSpec((tm, 2 * tn_k), lambda i, j: (i, j)), name="proj_c_k")
        vt_c = vt_call(lat[:, c_q_lora:], wv, c_heads, "proj_c_v")
        o_c = _attention(
            "latent", safe_c, q_c, k_c, vt_c, batch=batch, seq=seq, n_heads=c_heads, t=t, hp=hp_c, dq=2 * LANES,
            q_col0=0, k_col0=0, name="attn_latent")

        mix = jnp.concatenate([o_a, o_b, o_c], axis=1)
        tn_o = 512
        res_spec = pl.BlockSpec((tm, tn_o), lambda i, j: (i, j))
        xf = _matmul(mix, w_out[layer].astype(BF16), tm=tm, tn=tn_o, epilogue=_ep_residual,
                     aux=(xf,), aux_specs=(res_spec,),
                     out_shape=jax.ShapeDtypeStruct((tokens, d_model), F32), out_specs=res_spec,
                     name="out_proj")

        idx = layer // 2
        th = 256
        if layer % 2 == 0:
            h2 = _rms_norm(xf, ffn_norm[layer], tm=tm_n, name="ffn_norm")
            w_gu = _interleave_gate_up(dense_w_gate[idx].astype(BF16), dense_w_up[idx].astype(BF16), th)
            f = dense_w_gate.shape[-1]
            act = _matmul(h2, w_gu, tm=tm, tn=2 * th, epilogue=_ep_swiglu,
                          out_shape=jax.ShapeDtypeStruct((tokens, f), BF16),
                          out_specs=pl.BlockSpec((tm, th), lambda i, j: (i, j)), name="ffn_up")
            tm_d = _pick(tokens, (512, 256))
            res_d = pl.BlockSpec((tm_d, tn_o), lambda i, j: (i, j))
            xf = _matmul(act, dense_w_down[idx].astype(BF16), tm=tm_d, tn=tn_o, epilogue=_ep_residual,
                         aux=(xf,), aux_specs=(res_d,),
                         out_shape=jax.ShapeDtypeStruct((tokens, d_model), F32), out_specs=res_d,
                         name="ffn_down")
        else:
            h2, comb = _rms_norm_router(xf, ffn_norm[layer], moe_router[idx], tm=tm_n, name="ffn_norm_router")
            f = moe_w_gate.shape[-1]
            for e in range(n_exp):
                w_gu = _interleave_gate_up(moe_w_gate[idx, e].astype(BF16), moe_w_up[idx, e].astype(BF16), th)
                act = _matmul(h2, w_gu, tm=tm, tn=2 * th, epilogue=_ep_swiglu_scaled,
                              aux=(comb,), aux_specs=(pl.BlockSpec((tm, LANES), functools.partial(_expert_col, e)),),
                              out_shape=jax.ShapeDtypeStruct((tokens, f), BF16),
                              out_specs=pl.BlockSpec((tm, th), lambda i, j: (i, j)), name="moe_up")
                xf = _matmul(act, moe_w_down[idx, e].astype(BF16), tm=tm, tn=tn_o, epilogue=_ep_residual,
                             aux=(xf,), aux_specs=(res_spec,),
                             out_shape=jax.ShapeDtypeStruct((tokens, d_model), F32), out_specs=res_spec,
                             name="moe_down")

    return xf.reshape(batch, seq, d_model)
```

```python
import functools
import math

import jax
import jax.numpy as jnp
from jax import lax
from jax.experimental import pallas as pl
from jax.experimental.pallas import tpu as pltpu

F32 = jnp.float32
BF16 = jnp.bfloat16

LANES = 128
ONES_ROWS = 16
LOG2E = 1.4426950408889634
NEG = -1e30
EPS = 1e-6
VMEM_LIMIT = 52 * 1024 * 1024

NUM_BUCKETS = 32
MAX_DISTANCE = 128
ROPE_BASE = 10000.0
TOP_K = 2

_HI = lax.Precision.HIGHEST


def _cparams(n_axes):
    return pltpu.CompilerParams(
        dimension_semantics=("arbitrary",) * n_axes, vmem_limit_bytes=VMEM_LIMIT)


def _mm_body(a_ref, w_ref, *rest, epilogue, n_aux):
    acc = jnp.dot(a_ref[...], w_ref[...], preferred_element_type=F32)
    epilogue(acc, rest[:n_aux], rest[n_aux:])


def _matmul(a, w, *, tm, tn, epilogue, out_shape, out_specs, aux=(), aux_specs=(), name,
            a_cols=None, w_cols=None):
    m = a.shape[0]
    a0, k = a_cols if a_cols is not None else (0, a.shape[1])
    w0, n = w_cols if w_cols is not None else (0, w.shape[1])
    if a0 % k:
        a, a0 = a[:, a0:a0 + k], 0
    if w0 % tn:
        w, w0 = w[:, w0:w0 + n], 0
    assert k == w.shape[0]
    assert m % tm == 0 and n % tn == 0, (m, n, tm, tn)
    ab, wb = a0 // k, w0 // tn
    return pl.pallas_call(
        functools.partial(_mm_body, epilogue=epilogue, n_aux=len(aux)),
        grid=(m // tm, n // tn),
        in_specs=[pl.BlockSpec((tm, k), lambda i, j: (i, ab)),
                  pl.BlockSpec((k, tn), lambda i, j: (0, wb + j)), *aux_specs],
        out_specs=out_specs, out_shape=out_shape,
        compiler_params=_cparams(2), name=name)(a, w, *aux)


def _sumsq_lanes(x):
    return jnp.sum(x * x, axis=-1, keepdims=True)


def _ep_residual(acc, aux, outs):
    outs[0][...] = aux[0][...] + acc


def _out_proj_body(*refs, widths):
    part_refs = refs[:len(widths)]
    w_ref, x_ref, o_ref = refs[len(widths):]
    acc = x_ref[...]
    row = 0
    for p_ref, width in zip(part_refs, widths):
        acc = acc + jnp.dot(p_ref[...], w_ref[row:row + width, :], preferred_element_type=F32)
        row += width
    o_ref[...] = acc


def _out_proj(parts, w, x, *, tm, tn):
    m, n = x.shape
    widths = tuple(p.shape[1] for p in parts)
    res = pl.BlockSpec((tm, tn), lambda i, j: (i, j))
    return pl.pallas_call(
        functools.partial(_out_proj_body, widths=widths), grid=(m // tm, n // tn),
        in_specs=[pl.BlockSpec((tm, wd), lambda i, j: (i, 0)) for wd in widths]
        + [pl.BlockSpec((w.shape[0], tn), lambda i, j: (0, j)), res],
        out_specs=res, out_shape=jax.ShapeDtypeStruct((m, n), F32),
        compiler_params=_cparams(2), name="out_proj")(*parts, w, x)


def _swiglu_up_body(a_ref, wg_ref, wu_ref, o_ref):
    a = a_ref[...]
    g = jnp.dot(a, wg_ref[...], preferred_element_type=F32)
    u = jnp.dot(a, wu_ref[...], preferred_element_type=F32)
    o_ref[...] = (g * (1.0 / (1.0 + jnp.exp(-g))) * u).astype(o_ref.dtype)


def _swiglu_up(a, wg, wu, *, tm, th, name):
    m, k = a.shape
    f = wg.shape[1]
    w_spec = pl.BlockSpec((k, th), lambda i, j: (0, j))
    return pl.pallas_call(
        _swiglu_up_body, grid=(m // tm, f // th),
        in_specs=[pl.BlockSpec((tm, k), lambda i, j: (i, 0)), w_spec, w_spec],
        out_specs=pl.BlockSpec((tm, th), lambda i, j: (i, j)),
        out_shape=jax.ShapeDtypeStruct((m, f), BF16),
        compiler_params=_cparams(2), name=name)(a, wg, wu)


def _ep_group_norm(acc, aux, outs, *, group):
    gain = aux[0][...]
    tn = acc.shape[1]
    lane = lax.broadcasted_iota(jnp.int32, (1, LANES), 1)
    for c in range(tn // LANES):
        x = acc[:, c * LANES:(c + 1) * LANES]
        g = gain[:, c * LANES:(c + 1) * LANES]
        if group == LANES:
            r = lax.rsqrt(_sumsq_lanes(x) * (1.0 / LANES) + EPS)
        else:
            lo = lane < group
            x2 = x * x
            ss_lo = jnp.sum(jnp.where(lo, x2, 0.0), axis=-1, keepdims=True)
            ss_hi = jnp.sum(jnp.where(lo, 0.0, x2), axis=-1, keepdims=True)
            r = jnp.where(lo, lax.rsqrt(ss_lo * (1.0 / group) + EPS),
                          lax.rsqrt(ss_hi * (1.0 / group) + EPS))
        outs[0][:, c * LANES:(c + 1) * LANES] = (x * r * g).astype(outs[0].dtype)


def _ep_transpose_heads(acc, aux, outs, *, t):
    tm, tn = acc.shape
    ones = jnp.ones((ONES_ROWS, t), outs[0].dtype)
    for hh in range(tn // LANES):
        for s in range(tm // t):
            blk = acc[s * t:(s + 1) * t, hh * LANES:(hh + 1) * LANES]
            outs[0][hh, s, :LANES, :] = blk.T.astype(outs[0].dtype)
            outs[0][hh, s, LANES:, :] = ones


def _ep_latent_norms(acc, aux, outs, *, n_q):
    gain = aux[0][...]
    n = acc.shape[1]
    xq = acc[:, :n_q]
    xk = acc[:, n_q:]
    rq = lax.rsqrt(_sumsq_lanes(xq) * (1.0 / n_q) + EPS)
    rk = lax.rsqrt(_sumsq_lanes(xk) * (1.0 / (n - n_q)) + EPS)
    outs[0][:, :n_q] = (xq * rq * gain[:, :n_q]).astype(outs[0].dtype)
    outs[0][:, n_q:] = (xk * rk * gain[:, n_q:]).astype(outs[0].dtype)


def _rope_half_block(y, cos, sin):
    lane = lax.broadcasted_iota(jnp.int32, (1, LANES), 1)
    ra = pltpu.roll(y, 32, axis=1)
    rb = pltpu.roll(y, 96, axis=1)
    rot = jnp.where(lane < 32, -rb, ra)
    return y * cos + rot * sin


def _ep_tail(acc, aux, outs, *, rope_dim, n_gate):
    gain = aux[0][...]
    bias = aux[1][...]
    cos = aux[2][...]
    sin = aux[3][...]
    shift_col = aux[4][...]
    lane = lax.broadcasted_iota(jnp.int32, (1, LANES), 1)
    is_pe = lane < rope_dim
    xpe = jnp.where(is_pe, acc, 0.0)
    r = lax.rsqrt(_sumsq_lanes(xpe) * (1.0 / rope_dim) + EPS)
    y = xpe * r * gain
    outs[0][...] = (_rope_half_block(y, cos, sin) + shift_col).astype(outs[0].dtype)
    z = acc + bias
    logsig = -(jnp.maximum(-z, 0.0) + jnp.log(1.0 + jnp.exp(-jnp.abs(z))))
    is_gate = (lane >= rope_dim) & (lane < rope_dim + n_gate)
    outs[1][...] = jnp.where(is_gate, logsig, 0.0)


def _ep_latent_q(acc, aux, outs):
    g_nope = aux[0][...]
    g_rope = aux[1][...]
    cos = aux[2][...]
    sin = aux[3][...]
    one_col = aux[4][...]
    for hh in range(acc.shape[1] // (2 * LANES)):
        xn = acc[:, hh * 256:hh * 256 + LANES]
        xr = acc[:, hh * 256 + LANES:(hh + 1) * 256]
        rn = lax.rsqrt(_sumsq_lanes(xn) * (1.0 / LANES) + EPS)
        outs[0][:, hh * 256:hh * 256 + LANES] = (xn * rn * g_nope).astype(outs[0].dtype)
        rr = lax.rsqrt(_sumsq_lanes(xr) * (1.0 / 64) + EPS)
        yr = _rope_half_block(xr * rr * g_rope, cos, sin) + one_col
        outs[0][:, hh * 256 + LANES:(hh + 1) * 256] = yr.astype(outs[0].dtype)


def _ep_latent_k(acc, aux, outs):
    g_nope = aux[0][...]
    kpe = aux[1][...]
    for hh in range(acc.shape[1] // LANES):
        xn = acc[:, hh * LANES:(hh + 1) * LANES]
        rn = lax.rsqrt(_sumsq_lanes(xn) * (1.0 / LANES) + EPS)
        outs[0][:, hh * 256:hh * 256 + LANES] = (xn * rn * g_nope).astype(outs[0].dtype)
        outs[0][:, hh * 256 + LANES:(hh + 1) * 256] = kpe


def _norm_body(x_ref, g_ref, h_ref):
    x = x_ref[...]
    d = x.shape[1]
    r = lax.rsqrt(_sumsq_lanes(x) * (1.0 / d) + EPS)
    h_ref[...] = (x * r * g_ref[...]).astype(h_ref.dtype)


def _rms_norm(x, gain, *, tm, name):
    t, d = x.shape
    return pl.pallas_call(
        _norm_body, grid=(t // tm,),
        in_specs=[pl.BlockSpec((tm, d), lambda i: (i, 0)), pl.BlockSpec((1, d), lambda i: (0, 0))],
        out_specs=pl.BlockSpec((tm, d), lambda i: (i, 0)),
        out_shape=jax.ShapeDtypeStruct((t, d), BF16),
        compiler_params=_cparams(1), name=name)(x, gain.reshape(1, d))


ROUTE_I1, ROUTE_I2, ROUTE_G1, ROUTE_G2, ROUTE_R1, ROUTE_R2 = range(6)


def _norm_router_body(x_ref, g_ref, wr_ref, h_ref, route_ref, count_ref, carry_ref, *, n_exp):
    @pl.when(pl.program_id(0) == 0)
    def _():
        carry_ref[...] = jnp.zeros_like(carry_ref)

    x = x_ref[...]
    tm, d = x.shape
    r = lax.rsqrt(_sumsq_lanes(x) * (1.0 / d) + EPS)
    h = x * r * g_ref[...]
    h_ref[...] = h.astype(h_ref.dtype)
    logits = jnp.dot(h, wr_ref[...], preferred_element_type=F32, precision=_HI)
    lane = lax.broadcasted_iota(jnp.int32, logits.shape, 1).astype(F32)
    lg = jnp.where(lane < n_exp, logits, -jnp.inf)
    m1 = jnp.max(lg, axis=-1, keepdims=True)
    i1 = jnp.min(jnp.where(lg == m1, lane, float(LANES)), axis=-1, keepdims=True)
    lg2 = jnp.where(lane == i1, -jnp.inf, lg)
    m2 = jnp.max(lg2, axis=-1, keepdims=True)
    i2 = jnp.min(jnp.where(lg2 == m2, lane, float(LANES)), axis=-1, keepdims=True)
    e2 = jnp.exp(m2 - m1)
    g1 = 1.0 / (1.0 + e2)
    g2 = e2 * g1

    chosen = jnp.where((lane == i1) | (lane == i2), 1.0, 0.0)
    rr = lax.broadcasted_iota(jnp.int32, (tm, tm), 0)
    cc = lax.broadcasted_iota(jnp.int32, (tm, tm), 1)
    strict = jnp.where(rr > cc, 1.0, 0.0).astype(BF16)
    before = jnp.dot(strict, chosen.astype(BF16), preferred_element_type=F32) + carry_ref[...]
    rank1 = jnp.sum(jnp.where(lane == i1, before, 0.0), axis=-1, keepdims=True)
    rank2 = jnp.sum(jnp.where(lane == i2, before, 0.0), axis=-1, keepdims=True)
    carry_ref[...] += jnp.sum(chosen, axis=0, keepdims=True)
    count_ref[...] = carry_ref[...]

    rec = jnp.zeros((tm, LANES), F32)
    for k, v in ((ROUTE_I1, i1), (ROUTE_I2, i2), (ROUTE_G1, g1), (ROUTE_G2, g2), (ROUTE_R1, rank1), (ROUTE_R2, rank2)):
        rec = jnp.where(lane == float(k), v, rec)
    route_ref[...] = rec


def _rms_norm_router(x, gain, router, *, tm, name):
    t, d = x.shape
    n_exp = router.shape[1]
    wr = jnp.zeros((d, LANES), F32).at[:, :n_exp].set(router.astype(F32))
    return pl.pallas_call(
        functools.partial(_norm_router_body, n_exp=n_exp), grid=(t // tm,),
        in_specs=[pl.BlockSpec((tm, d), lambda i: (i, 0)), pl.BlockSpec((1, d), lambda i: (0, 0)),
                  pl.BlockSpec((d, LANES), lambda i: (0, 0))],
        out_specs=[pl.BlockSpec((tm, d), lambda i: (i, 0)),
                   pl.BlockSpec((tm, LANES), lambda i: (i, 0)),
                   pl.BlockSpec((1, LANES), lambda i: (0, 0))],
        out_shape=[jax.ShapeDtypeStruct((t, d), BF16),
                   jax.ShapeDtypeStruct((t, LANES), F32),
                   jax.ShapeDtypeStruct((1, LANES), F32)],
        scratch_shapes=[pltpu.VMEM((1, LANES), F32)],
        compiler_params=_cparams(1), name=name)(x, gain.reshape(1, d), wr)


def _row_view(a):
    rows, d = a.shape
    return a.reshape(rows, d // LANES, LANES)


def _dispatch_body(pos_ref, h_ref, init_ref, xs_ref, sem, *, tm):
    del init_ref
    base = pl.program_id(0) * tm

    def issue(r, carry):
        pltpu.make_async_copy(h_ref.at[base + r], xs_ref.at[pos_ref[0, 0, r]], sem).start()
        pltpu.make_async_copy(h_ref.at[base + r], xs_ref.at[pos_ref[0, 0, tm + r]], sem).start()
        return carry

    def drain(r, carry):
        pltpu.make_async_copy(h_ref.at[0], xs_ref.at[0], sem).wait()
        pltpu.make_async_copy(h_ref.at[0], xs_ref.at[0], sem).wait()
        return carry

    lax.fori_loop(0, tm, issue, 0)
    lax.fori_loop(0, tm, drain, 0)


def _dispatch(h_rows, pos, n_rows, *, tm):
    n_tiles = pos.shape[0]
    init = jnp.zeros((n_rows,) + h_rows.shape[1:], h_rows.dtype)
    return pl.pallas_call(
        functools.partial(_dispatch_body, tm=tm), grid=(n_tiles,),
        in_specs=[pl.BlockSpec((1, 1, 2 * tm), lambda i: (i, 0, 0), memory_space=pltpu.SMEM),
                  pl.BlockSpec(memory_space=pl.ANY), pl.BlockSpec(memory_space=pl.ANY)],
        out_specs=pl.BlockSpec(memory_space=pl.ANY),
        out_shape=jax.ShapeDtypeStruct(init.shape, init.dtype),
        scratch_shapes=[pltpu.SemaphoreType.DMA(())],
        input_output_aliases={2: 0},
        compiler_params=_cparams(1), name="moe_dispatch")(pos, h_rows, init)


def _combine_body(pos_ref, gate_ref, ys_ref, x_ref, o_ref, buf, sem, *, tm):
    def issue(r, carry):
        pltpu.make_async_copy(ys_ref.at[pos_ref[0, 0, r]], buf.at[0, r], sem).start()
        pltpu.make_async_copy(ys_ref.at[pos_ref[0, 0, tm + r]], buf.at[1, r], sem).start()
        return carry

    def drain(r, carry):
        pltpu.make_async_copy(ys_ref.at[0], buf.at[0, 0], sem).wait()
        pltpu.make_async_copy(ys_ref.at[0], buf.at[1, 0], sem).wait()
        return carry

    def mix(r, carry):
        o_ref[r] = (x_ref[r] + gate_ref[0, 0, r] * buf[0, r].astype(F32)
                    + gate_ref[0, 0, tm + r] * buf[1, r].astype(F32))
        return carry

    lax.fori_loop(0, tm, issue, 0)
    lax.fori_loop(0, tm, drain, 0)
    lax.fori_loop(0, tm, mix, 0)


def _combine(ys_rows, pos, gates, x_rows, *, tm):
    n_tiles = pos.shape[0]
    chunks = x_rows.shape[1]
    tile = pl.BlockSpec((tm, chunks, LANES), lambda i: (i, 0, 0))
    smem_tile = pl.BlockSpec((1, 1, 2 * tm), lambda i: (i, 0, 0), memory_space=pltpu.SMEM)
    return pl.pallas_call(
        functools.partial(_combine_body, tm=tm), grid=(n_tiles,),
        in_specs=[smem_tile, smem_tile, pl.BlockSpec(memory_space=pl.ANY), tile],
        out_specs=tile,
        out_shape=jax.ShapeDtypeStruct(x_rows.shape, F32),
        scratch_shapes=[pltpu.VMEM((2, tm, chunks, LANES), ys_rows.dtype), pltpu.SemaphoreType.DMA(())],
        compiler_params=_cparams(1), name="moe_combine")(pos, gates, ys_rows, x_rows)


def _grouped_up_body(te_ref, nv_ref, a_ref, wg_ref, wu_ref, o_ref):
    @pl.when(pl.program_id(1) < nv_ref[0])
    def _():
        a = a_ref[...]
        g = jnp.dot(a, wg_ref[...], preferred_element_type=F32)
        u = jnp.dot(a, wu_ref[...], preferred_element_type=F32)
        o_ref[...] = (g * (1.0 / (1.0 + jnp.exp(-g))) * u).astype(o_ref.dtype)

    @pl.when(pl.program_id(1) >= nv_ref[0])
    def _():
        o_ref[...] = jnp.zeros_like(o_ref)


def _grouped_down_body(te_ref, nv_ref, a_ref, w_ref, o_ref):
    @pl.when(pl.program_id(1) < nv_ref[0])
    def _():
        o_ref[...] = jnp.dot(a_ref[...], w_ref[...], preferred_element_type=F32).astype(o_ref.dtype)

    @pl.when(pl.program_id(1) >= nv_ref[0])
    def _():
        o_ref[...] = jnp.zeros_like(o_ref)


def _grouped_matmul(body, a, weights, tile_expert, n_valid, *, tmo, tn, name):
    r, k = a.shape
    n = weights[0].shape[2]

    def row(j, i, te, nv):
        return jnp.minimum(i, nv[0] - 1)

    grid_spec = pltpu.PrefetchScalarGridSpec(
        num_scalar_prefetch=2, grid=(n // tn, r // tmo),
        in_specs=[pl.BlockSpec((tmo, k), lambda j, i, te, nv: (row(j, i, te, nv), 0))]
        + [pl.BlockSpec((None, k, tn), lambda j, i, te, nv: (te[row(j, i, te, nv)], 0, j)) for _ in weights],
        out_specs=pl.BlockSpec((tmo, tn), lambda j, i, te, nv: (i, j)))
    return pl.pallas_call(
        body, grid_spec=grid_spec, out_shape=jax.ShapeDtypeStruct((r, n), BF16),
        compiler_params=_cparams(2), name=name)(tile_expert, n_valid, a, *weights)


def _rope_body(pos_ref, f_ref, cos_ref, sin_ref):
    ang = pos_ref[...] * f_ref[...]
    live = f_ref[...] > 0.0
    cos_ref[...] = jnp.where(live, jnp.cos(ang), 0.0)
    sin_ref[...] = jnp.where(live, jnp.sin(ang), 0.0)


def _rope_tables(positions, *, half, tm):
    t = positions.size
    inv_freq = ROPE_BASE ** (-jnp.arange(half, dtype=F32) / half)
    f_row = jnp.concatenate([inv_freq, inv_freq, jnp.zeros((LANES - 2 * half,), F32)]).reshape(1, LANES)
    pos_rep = jnp.broadcast_to(positions.astype(F32).reshape(t, 1), (t, LANES))
    spec = pl.BlockSpec((tm, LANES), lambda i: (i, 0))
    return pl.pallas_call(
        _rope_body, grid=(t // tm,),
        in_specs=[spec, pl.BlockSpec((1, LANES), lambda i: (0, 0))],
        out_specs=[spec, spec],
        out_shape=[jax.ShapeDtypeStruct((t, LANES), F32)] * 2,
        compiler_params=_cparams(1), name="rope_tables")(pos_rep, f_row)


def _bias_body(rb_ref, o_ref, *, t):
    h = pl.program_id(0)
    d = pl.program_id(1)
    j = lax.broadcasted_iota(jnp.int32, (t, t), 0)
    i = lax.broadcasted_iota(jnp.int32, (t, t), 1)
    rel = d * t + i - j
    n = jnp.maximum(rel, 0)
    max_exact = NUM_BUCKETS // 2
    large = max_exact + (
        jnp.log(jnp.maximum(n, max_exact).astype(F32) / max_exact)
        / math.log(MAX_DISTANCE / max_exact) * (NUM_BUCKETS - max_exact)
    ).astype(jnp.int32)
    large = jnp.minimum(large, NUM_BUCKETS - 1)
    bucket = jnp.where(n < max_exact, n, large)
    far = rb_ref[NUM_BUCKETS - 1, h]
    val = jnp.zeros((t, t), F32)
    for b in range(NUM_BUCKETS - 1):
        val = jnp.where(bucket == b, rb_ref[b, h] - far, val)
    o_ref[...] = jnp.where(rel >= 0, val * LOG2E, NEG)


def _bias_tiles(rel_bias, *, t):
    assert t + 1 >= MAX_DISTANCE, "tiles beyond the first sub-diagonal must be in the last bucket"
    nh = rel_bias.shape[1]
    return pl.pallas_call(
        functools.partial(_bias_body, t=t), grid=(nh, 2),
        in_specs=[pl.BlockSpec(memory_space=pltpu.SMEM)],
        out_specs=pl.BlockSpec((None, None, t, t), lambda h, d: (h, d, 0, 0)),
        out_shape=jax.ShapeDtypeStruct((nh, 2, t, t), F32),
        compiler_params=_cparams(2), name="t5_bias_tiles")(rel_bias.astype(F32))


N_PIECES = 3


def _split_pieces(x):
    pieces = []
    rest = x
    for _ in range(N_PIECES):
        p = rest.astype(BF16).astype(F32)
        pieces.append(p)
        rest = rest - p
    return pieces


def _cum_body(shift_ref, lf_ref, o_ref, carry_ref, *, n_heads, lane0):
    @pl.when(pl.program_id(1) == 0)
    def _():
        carry_ref[...] = jnp.zeros_like(carry_ref)

    x = lf_ref[...]
    tb = x.shape[0]
    r = lax.broadcasted_iota(jnp.int32, (tb, tb), 0)
    c = lax.broadcasted_iota(jnp.int32, (tb, tb), 1)
    tri = jnp.where(r >= c, 1.0, 0.0).astype(F32)
    er = lax.broadcasted_iota(jnp.int32, (LANES, LANES), 0)
    lane = lax.broadcasted_iota(jnp.int32, (tb, LANES), 1)
    shift = shift_ref[0]
    for hh in range(n_heads):
        sel = jnp.where(er == lane0 + hh, 1.0, 0.0).astype(F32)
        xh = jnp.dot(x, sel, preferred_element_type=F32, precision=_HI)
        cum = jnp.dot(tri, xh, preferred_element_type=F32, precision=_HI) + carry_ref[hh:hh + 1, :]
        carry_ref[hh:hh + 1, :] = cum[tb - 1:tb, :]
        cum2 = cum * LOG2E
        qcols = jnp.where(lane < N_PIECES, 1.0, 0.0)
        kcols = jnp.where((lane >= N_PIECES) & (lane < 2 * N_PIECES), 1.0, 0.0)
        for n, (pq, pk) in enumerate(zip(_split_pieces(cum2 - shift), _split_pieces(-cum2))):
            qcols = jnp.where(lane == N_PIECES + n, pq, qcols)
            kcols = jnp.where(lane == n, pk, kcols)
        o_ref[:, hh * LANES:(hh + 1) * LANES] = qcols.astype(o_ref.dtype)
        o_ref[:, (n_heads + hh) * LANES:(n_heads + hh + 1) * LANES] = kcols.astype(o_ref.dtype)


def _forget_columns(logf, shift, *, batch, seq, n_heads, lane0, tb):
    nb = seq // tb
    width = 2 * n_heads * LANES
    return pl.pallas_call(
        functools.partial(_cum_body, n_heads=n_heads, lane0=lane0), grid=(batch, nb),
        in_specs=[pl.BlockSpec(memory_space=pltpu.SMEM),
                  pl.BlockSpec((tb, LANES), lambda b, i: (b * nb + i, 0))],
        out_specs=pl.BlockSpec((tb, width), lambda b, i: (b * nb + i, 0)),
        out_shape=jax.ShapeDtypeStruct((batch * seq, width), BF16),
        scratch_shapes=[pltpu.VMEM((n_heads, LANES), F32)],
        compiler_params=_cparams(2), name="forget_cumsum")(shift.reshape(1).astype(F32), logf)


def _online_block(s, vb, m_ref, acc_ref):
    m_prev = m_ref[...]
    m_new = jnp.maximum(m_prev, jnp.max(s, axis=0, keepdims=True))
    alpha = jnp.exp2(m_prev - m_new)
    p = jnp.exp2(s - m_new).astype(BF16)
    acc_ref[...] = alpha * acc_ref[...] + jnp.dot(vb, p, preferred_element_type=F32)
    m_ref[...] = m_new


def _shifted_block(s, vb, acc_ref):
    p = jnp.exp2(s).astype(BF16)
    acc_ref[...] += jnp.dot(vb, p, preferred_element_type=F32)


def _causal_mask(t):
    j = lax.broadcasted_iota(jnp.int32, (t, t), 0)
    i = lax.broadcasted_iota(jnp.int32, (t, t), 1)
    return j <= i


_NT = (((1,), (1,)), ((), ()))
FAR_UNROLL = 2


def _init_state(m_ref, acc_ref):
    m_ref[...] = jnp.full(m_ref.shape, NEG, F32)
    acc_ref[...] = jnp.zeros(acc_ref.shape, F32)


def _attn_body(*refs, mode, t, hp, dq, dv, qk_dim=None, lambda_init=None):
    safe_ref, q_ref, k_ref, vt_ref = refs[:4]
    if mode == "latent":
        o_ref, qc_ref, m_ref, acc_ref = refs[4:]
    elif mode == "forget":
        qx_ref, kx_ref, o_ref, qc_ref, m_ref, acc_ref = refs[4:]
    else:
        qx_ref, kx_ref, bias_ref, lam_ref, hn_ref, o_ref, qc_ref, m_ref, acc_ref = refs[4:]
    n_map = 2 if mode == "diff" else 1
    qi = pl.program_id(2)
    _init_state(m_ref, acc_ref)

    for hh in range(hp):
        q = q_ref[:, hh * dq:(hh + 1) * dq]
        if mode == "latent":
            qc_ref[hh] = q
        elif mode == "forget":
            qc_ref[hh] = jnp.concatenate([q, qx_ref[:, hh * LANES:(hh + 1) * LANES]], axis=1)
        else:
            lane = lax.broadcasted_iota(jnp.int32, (t, dq), 1)
            zero = jnp.zeros_like(q)
            qc_ref[2 * hh] = jnp.concatenate([jnp.where(lane < qk_dim, q, zero), qx_ref[...]], axis=1)
            qc_ref[2 * hh + 1] = jnp.concatenate([jnp.where(lane < qk_dim, zero, q), qx_ref[...]], axis=1)

    def keys(kj, hh):
        start = pl.multiple_of(kj * t, t)
        kb = k_ref[pl.ds(start, t), hh * dq:(hh + 1) * dq]
        if mode == "forget":
            kb = jnp.concatenate([kb, kx_ref[pl.ds(start, t), hh * LANES:(hh + 1) * LANES]], axis=1)
        elif mode == "diff":
            kb = jnp.concatenate([kb, kx_ref[hh]], axis=1)
        return kb

    def run(update):
        def block(kj, kind):
            for hh in range(hp):
                kb = keys(kj, hh)
                vb = vt_ref[hh, kj]
                for mm in range(n_map):
                    c = n_map * hh + mm
                    s = lax.dot_general(kb, qc_ref[c], _NT, preferred_element_type=F32)
                    if mode == "diff":
                        if kind != "far":
                            s = s + bias_ref[hh, 0 if kind == "diag" else 1]
                    elif kind == "diag":
                        s = jnp.where(_causal_mask(t), s, NEG)
                    update(s, vb, c)

        def far_group(g, carry):
            for u in range(FAR_UNROLL):
                block(g * FAR_UNROLL + u, "far")
            return carry

        n_far = jnp.maximum(qi - 1, 0) if mode == "diff" else qi
        n_groups = n_far // FAR_UNROLL
        lax.fori_loop(0, n_groups, far_group, 0)
        for u in range(FAR_UNROLL - 1):
            @pl.when(n_groups * FAR_UNROLL + u < n_far)
            def _(u=u):
                block(n_groups * FAR_UNROLL + u, "far")

        if mode == "diff":
            @pl.when(qi >= 1)
            def _():
                block(qi - 1, "sub")
        block(qi, "diag")

    @pl.when(safe_ref[0] != 0)
    def _():
        run(lambda s, vb, c: _shifted_block(s, vb, acc_ref.at[c]))

    @pl.when(safe_ref[0] == 0)
    def _():
        run(lambda s, vb, c: _online_block(s, vb, m_ref.at[c], acc_ref.at[c]))

    if mode == "diff":
        lp = lam_ref[...]
        lam = (jnp.exp(jnp.sum(lp[0:1] * lp[1:2], axis=-1, keepdims=True))
               - jnp.exp(jnp.sum(lp[2:3] * lp[3:4], axis=-1, keepdims=True)) + lambda_init)
    for hh in range(hp):
        if mode == "diff":
            a0 = acc_ref[2 * hh]
            a1 = acc_ref[2 * hh + 1]
            o = a0[:dv] * (1.0 / a0[dv:dv + 1]) - lam * (a1[:dv] * (1.0 / a1[dv:dv + 1]))
            r = lax.rsqrt(jnp.sum(o * o, axis=0, keepdims=True) * (1.0 / dv) + EPS)
            o = o * r * hn_ref[...] * (1.0 - lambda_init)
        else:
            a = acc_ref[hh]
            o = a[:dv] * (1.0 / a[dv:dv + 1])
        o_ref[:, hh * dv:(hh + 1) * dv] = o.T.astype(o_ref.dtype)


def _attention(mode, safe, q_arr, k_arr, vt_arr, *, batch, seq, n_heads, t, hp, dq, q_col0, k_col0,
               extra=(), extra_specs=(), body_kwargs=None, name):
    nq = seq // t
    dvp = vt_arr.shape[2]
    dv = dvp - ONES_ROWS
    tokens = batch * seq
    assert n_heads % hp == 0 and q_col0 % hp == 0 and k_col0 % hp == 0
    qc, kc = q_col0 // hp, k_col0 // hp
    in_specs = [
        pl.BlockSpec(memory_space=pltpu.SMEM),
        pl.BlockSpec((t, hp * dq), lambda b, h, i: (b * nq + i, qc + h)),
        pl.BlockSpec((seq, hp * dq), lambda b, h, i: (b, kc + h)),
        pl.BlockSpec((hp, nq, dvp, t), lambda b, h, i: (h, b, 0, 0)),
        *extra_specs,
    ]
    n_chain = 2 * hp if mode == "diff" else hp
    scratch = [pltpu.VMEM((n_chain, t, 2 * LANES), BF16),
               pltpu.VMEM((n_chain, 1, t), F32), pltpu.VMEM((n_chain, dvp, t), F32)]
    body = functools.partial(_attn_body, mode=mode, t=t, hp=hp, dq=dq, dv=dv, **(body_kwargs or {}))
    return pl.pallas_call(
        body, grid=(batch, n_heads // hp, nq), in_specs=in_specs,
        out_specs=pl.BlockSpec((t, hp * dv), lambda b, h, i: (b * nq + i, h)),
        out_shape=jax.ShapeDtypeStruct((tokens, n_heads * dv), BF16),
        scratch_shapes=scratch, compiler_params=_cparams(3), name=name)(safe, q_arr, k_arr, vt_arr, *extra)


def _pick(n, prefs):
    for p in prefs:
        if n % p == 0:
            return p
    raise ValueError((n, prefs))


SAFE_DEPTH = 110.0


def _round_up_bf16(c):
    return (c * (1.0 + 2.0 ** -6)).astype(BF16).astype(F32)


def _safe_flag(depth):
    return (depth <= SAFE_DEPTH).astype(jnp.int32).reshape(1)


def kernel(x, positions, attn_norm, w_in, b_forget, a_qk_norm, a_lambda, a_head_norm, rel_bias,
           b_qk_norm, c_q_a_norm, c_kv_a_norm, c_w_uq, c_w_ukv, c_qk_norm_nope, c_qk_norm_rope,
           w_out, ffn_norm, dense_w_gate, dense_w_up, dense_w_down, moe_router, moe_w_gate,
           moe_w_up, moe_w_down):
    batch, seq, d_model = x.shape
    depth = w_in.shape[0]
    tokens = batch * seq
    n_slots = d_model // LANES
    a_heads = n_slots // 4
    b_heads = n_slots // 4
    c_heads = n_slots // 2
    a_qk = a_qk_norm.shape[-1]
    b_dim = b_qk_norm.shape[-1]
    c_q_lora = c_q_a_norm.shape[-1]
    c_kv_lora = c_kv_a_norm.shape[-1]
    c_nope = c_qk_norm_nope.shape[-1]
    c_rope = c_qk_norm_rope.shape[-1]
    c_v = c_w_ukv.shape[-1] // c_heads - c_nope
    n_exp = moe_router.shape[-1]
    assert 2 * a_qk == LANES and b_dim == LANES and c_nope == LANES and c_v == LANES and 2 * c_rope == LANES

    t = _pick(seq, (512, 256, 128))
    tm = _pick(tokens, (1024, 512, 256))
    tm_n = _pick(tokens, (256, 128))
    hp_a, hp_b, hp_c = 1, 2, 2

    sizes = (a_heads * 2 * a_qk, a_heads * 2 * a_qk, a_heads * LANES, b_heads * b_dim, b_heads * b_dim,
             b_heads * b_dim, b_heads, c_q_lora, c_kv_lora, c_rope)
    offs = [0]
    for s in sizes:
        offs.append(offs[-1] + s)
    o_aq, o_ak, o_av, o_bq, o_bk, o_bv, o_f, o_cq, o_ckv, o_kpe, o_end = offs

    xf = x.reshape(tokens, d_model)
    cos_t, sin_t = _rope_tables(positions, half=c_rope // 2, tm=tm)
    bias_tiles = _bias_tiles(rel_bias, t=t)
    bias_log2 = (rel_bias.astype(F32) - rel_bias[-1:].astype(F32)) * LOG2E
    bias_max = jnp.max(bias_log2, axis=0)
    qx_a = jnp.zeros((t, LANES), F32).at[:, 0].set(1.0).astype(BF16)

    row_spec = lambda width: pl.BlockSpec((1, width), lambda i, j: (0, j))
    fixed_row = pl.BlockSpec((1, LANES), lambda i, j: (0, 0))
    tok_lane = pl.BlockSpec((tm, LANES), lambda i, j: (i, 0))

    def vt_call(h_bf, w, nh, name, a_cols=None, w_cols=None):
        tn = _pick(nh * LANES, (512, 256, 128))
        return _matmul(
            h_bf, w, a_cols=a_cols, w_cols=w_cols, tm=tm, tn=tn,
            epilogue=functools.partial(_ep_transpose_heads, t=t),
            out_shape=jax.ShapeDtypeStruct((nh, tokens // t, LANES + ONES_ROWS, t), BF16),
            out_specs=pl.BlockSpec((tn // LANES, tm // t, LANES + ONES_ROWS, t), lambda i, j: (j, i, 0, 0)),
            name=name)

    for layer in range(depth):
        lambda_init = 0.8 - 0.6 * math.exp(-0.3 * layer)
        w = w_in[layer].astype(BF16)

        h = _rms_norm(xf, attn_norm[layer], tm=tm_n, name="attn_norm")

        a_scale = a_qk ** -0.5 * LOG2E
        gain_a = jnp.concatenate([jnp.tile(a_qk_norm[layer, 0] * a_scale, 2 * a_heads),
                                  jnp.tile(a_qk_norm[layer, 1], 2 * a_heads)]).reshape(1, -1)
        tn = 512
        qk_a = _matmul(h, w, w_cols=(o_aq, o_av - o_aq), tm=tm, tn=tn,
                       epilogue=functools.partial(_ep_group_norm, group=a_qk),
                       aux=(gain_a,), aux_specs=(row_spec(tn),),
                       out_shape=jax.ShapeDtypeStruct((tokens, o_av - o_aq), BF16),
                       out_specs=pl.BlockSpec((tm, tn), lambda i, j: (i, j)), name="proj_a_qk")
        vt_a = vt_call(h, w, a_heads, "proj_a_v", w_cols=(o_av, o_bq - o_av))
        hn = jnp.broadcast_to(a_head_norm[layer].reshape(LANES, 1), (LANES, t))
        bound_a = a_qk * jnp.max(jnp.abs(a_qk_norm[layer, 0] * a_scale)) * jnp.max(jnp.abs(a_qk_norm[layer, 1]))
        shift_a = _round_up_bf16(bound_a + bias_max)
        safe_a = _safe_flag(2.0 * bound_a + jnp.max(bias_max - bias_log2[0]))
        kx_a = jnp.zeros((a_heads, t, LANES), F32).at[:, :, 0].set(-shift_a[:, None]).astype(BF16)
        o_a = _attention(
            "diff", safe_a, qk_a, qk_a, vt_a, batch=batch, seq=seq, n_heads=a_heads, t=t, hp=hp_a, dq=LANES,
            q_col0=0, k_col0=a_heads,
            extra=(qx_a, kx_a, bias_tiles, a_lambda[layer], hn),
            extra_specs=(pl.BlockSpec((t, LANES), lambda b, hh, i: (0, 0)),
                         pl.BlockSpec((hp_a, t, LANES), lambda b, hh, i: (hh, 0, 0)),
                         pl.BlockSpec((hp_a, 2, t, t), lambda b, hh, i: (hh, 0, 0, 0)),
                         pl.BlockSpec((4, a_qk), lambda b, hh, i: (0, 0)),
                         pl.BlockSpec((LANES, t), lambda b, hh, i: (0, 0))),
            body_kwargs=dict(qk_dim=a_qk, lambda_init=lambda_init), name="attn_diff")

        b_scale = b_dim ** -0.5 * LOG2E
        gain_b = jnp.concatenate([jnp.tile(b_qk_norm[layer, 0] * b_scale, b_heads),
                                  jnp.tile(b_qk_norm[layer, 1], b_heads)]).reshape(1, -1)
        qk_b = _matmul(h, w, w_cols=(o_bq, o_bv - o_bq), tm=tm, tn=tn,
                       epilogue=functools.partial(_ep_group_norm, group=b_dim),
                       aux=(gain_b,), aux_specs=(row_spec(tn),),
                       out_shape=jax.ShapeDtypeStruct((tokens, o_bv - o_bq), BF16),
                       out_specs=pl.BlockSpec((tm, tn), lambda i, j: (i, j)), name="proj_b_qk")
        vt_b = vt_call(h, w, b_heads, "proj_b_v", w_cols=(o_bv, o_f - o_bv))

        w_tail = jnp.concatenate([w[:, o_kpe:o_end], w[:, o_f:o_cq],
                                  jnp.zeros((d_model, LANES - c_rope - b_heads), BF16)], axis=1)
        gain_tail = jnp.concatenate([c_qk_norm_rope[layer, 1], jnp.zeros((LANES - c_rope,), F32)]).reshape(1, LANES)
        bias_tail = jnp.zeros((LANES,), F32).at[c_rope:c_rope + b_heads].set(b_forget[layer]).reshape(1, LANES)
        c_scale = (c_nope + c_rope) ** -0.5 * LOG2E
        gq_n = c_qk_norm_nope[layer, 0] * c_scale
        gq_r = c_qk_norm_rope[layer, 0] * c_scale
        gk_n = c_qk_norm_nope[layer, 1]
        gk_r = c_qk_norm_rope[layer, 1]
        bound_c = (jnp.sqrt(c_nope * jnp.max(gq_n * gq_n) + c_rope * jnp.max(gq_r * gq_r))
                   * jnp.sqrt(c_nope * jnp.max(gk_n * gk_n) + c_rope * jnp.max(gk_r * gk_r)))
        shift_c = _round_up_bf16(bound_c)
        safe_c = _safe_flag(2.0 * bound_c)
        shift_col = jnp.zeros((1, LANES), F32).at[0, c_rope].set(-shift_c)
        one_col = jnp.zeros((1, LANES), F32).at[0, c_rope].set(1.0)
        kpe, logf = _matmul(
            h, w_tail, tm=tm, tn=LANES,
            epilogue=functools.partial(_ep_tail, rope_dim=c_rope, n_gate=b_heads),
            aux=(gain_tail, bias_tail, cos_t, sin_t, shift_col),
            aux_specs=(fixed_row, fixed_row, tok_lane, tok_lane, fixed_row),
            out_shape=[jax.ShapeDtypeStruct((tokens, LANES), BF16), jax.ShapeDtypeStruct((tokens, LANES), F32)],
            out_specs=[tok_lane, tok_lane], name="proj_tail")
        bound_b = b_dim * jnp.max(jnp.abs(b_qk_norm[layer, 0] * b_scale)) * jnp.max(jnp.abs(b_qk_norm[layer, 1]))
        safe_b = _safe_flag(2.0 * bound_b)
        x_b = _forget_columns(logf, bound_b, batch=batch, seq=seq, n_heads=b_heads, lane0=c_rope,
                              tb=_pick(seq, (256, 128)))
        nbh = b_heads // hp_b
        o_b = _attention(
            "forget", safe_b, qk_b, qk_b, vt_b, batch=batch, seq=seq, n_heads=b_heads, t=t, hp=hp_b, dq=LANES,
            q_col0=0, k_col0=b_heads,
            extra=(x_b, x_b),
            extra_specs=(pl.BlockSpec((t, hp_b * LANES), lambda b, hh, i: (b * (seq // t) + i, hh)),
                         pl.BlockSpec((seq, hp_b * LANES), lambda b, hh, i: (b, nbh + hh))),
            name="attn_forget")

        n_lat = c_q_lora + c_kv_lora
        gain_lat = jnp.concatenate([c_q_a_norm[layer], c_kv_a_norm[layer]]).reshape(1, n_lat)
        tm_lat = _pick(tokens, (512, 256))
        lat = _matmul(h, w[:, o_cq:o_kpe], tm=tm_lat, tn=n_lat,
                      epilogue=functools.partial(_ep_latent_norms, n_q=c_q_lora),
                      aux=(gain_lat,), aux_specs=(row_spec(n_lat),),
                      out_shape=jax.ShapeDtypeStruct((tokens, n_lat), BF16),
                      out_specs=pl.BlockSpec((tm_lat, n_lat), lambda i, j: (i, j)), name="proj_c_latent")
        wq = c_w_uq[layer].astype(BF16).reshape(c_q_lora, c_heads, c_nope + c_rope)
        wq = jnp.concatenate([wq, jnp.zeros((c_q_lora, c_heads, 2 * LANES - c_nope - c_rope), BF16)], axis=2)
        wq = wq.reshape(c_q_lora, c_heads * 2 * LANES)
        gq_nope = gq_n.reshape(1, LANES)
        gq_rope = jnp.concatenate([gq_r, jnp.zeros((LANES - c_rope,), F32)]).reshape(1, LANES)
        tn_q = 512
        lat_q_cols, lat_kv_cols = (0, c_q_lora), (c_q_lora, c_kv_lora)
        q_c = _matmul(lat, wq, a_cols=lat_q_cols, tm=tm, tn=tn_q, epilogue=_ep_latent_q,
                      aux=(gq_nope, gq_rope, cos_t, sin_t, one_col),
                      aux_specs=(fixed_row, fixed_row, tok_lane, tok_lane, fixed_row),
                      out_shape=jax.ShapeDtypeStruct((tokens, c_heads * 2 * LANES), BF16),
                      out_specs=pl.BlockSpec((tm, tn_q), lambda i, j: (i, j)), name="proj_c_q")
        wkv = c_w_ukv[layer].astype(BF16).reshape(c_kv_lora, c_heads, c_nope + c_v)
        wk = wkv[:, :, :c_nope].reshape(c_kv_lora, c_heads * c_nope)
        wv = wkv[:, :, c_nope:].reshape(c_kv_lora, c_heads * c_v)
        gk_nope = c_qk_norm_nope[layer, 1].reshape(1, LANES)
        tn_k = 256
        k_c = _matmul(lat, wk, a_cols=lat_kv_cols, tm=tm, tn=tn_k, epilogue=_ep_latent_k,
                      aux=(gk_nope, kpe), aux_specs=(fixed_row, tok_lane),
                      out_shape=jax.ShapeDtypeStruct((tokens, c_heads * 2 * LANES), BF16),
                      out_specs=pl.BlockSpec((tm, 2 * tn_k), lambda i, j: (i, j)), name="proj_c_k")
        vt_c = vt_call(lat, wv, c_heads, "proj_c_v", a_cols=lat_kv_cols)
        o_c = _attention(
            "latent", safe_c, q_c, k_c, vt_c, batch=batch, seq=seq, n_heads=c_heads, t=t, hp=hp_c, dq=2 * LANES,
            q_col0=0, k_col0=0, name="attn_latent")

        tn_o = 512
        xf = _out_proj((o_a, o_b, o_c), w_out[layer].astype(BF16), xf, tm=tm, tn=tn_o)

        idx = layer // 2
        th = 256
        if layer % 2 == 0:
            h2 = _rms_norm(xf, ffn_norm[layer], tm=tm_n, name="ffn_norm")
            act = _swiglu_up(h2, dense_w_gate[idx].astype(BF16), dense_w_up[idx].astype(BF16),
                             tm=tm, th=th, name="ffn_up")
            tm_d = _pick(tokens, (512, 256))
            res_d = pl.BlockSpec((tm_d, tn_o), lambda i, j: (i, j))
            xf = _matmul(act, dense_w_down[idx].astype(BF16), tm=tm_d, tn=tn_o, epilogue=_ep_residual,
                         aux=(xf,), aux_specs=(res_d,),
                         out_shape=jax.ShapeDtypeStruct((tokens, d_model), F32), out_specs=res_d,
                         name="ffn_down")
        else:
            h2, route, counts = _rms_norm_router(xf, ffn_norm[layer], moe_router[idx], tm=tm_n,
                                                 name="ffn_norm_router")
            tmo = _pick(tokens, (512, 256, 128))
            n_rows = TOP_K * tokens + n_exp * tmo
            cnt = counts[0, :n_exp].astype(jnp.int32)
            padded = (cnt + tmo - 1) // tmo * tmo
            ends = jnp.cumsum(padded)
            starts = ends - padded
            n_valid = (ends[-1] // tmo).reshape(1).astype(jnp.int32)
            tile_expert = jnp.minimum(
                jnp.searchsorted(ends, jnp.arange(n_rows // tmo, dtype=jnp.int32) * tmo, side="right"),
                n_exp - 1).astype(jnp.int32)
            e1 = route[:, ROUTE_I1].astype(jnp.int32)
            e2 = route[:, ROUTE_I2].astype(jnp.int32)
            pos1 = starts[e1] + route[:, ROUTE_R1].astype(jnp.int32)
            pos2 = starts[e2] + route[:, ROUTE_R2].astype(jnp.int32)
            tm_r = _pick(tokens, (256, 128))
            per_tile = lambda a, b: jnp.concatenate(
                [a.reshape(tokens // tm_r, 1, tm_r), b.reshape(tokens // tm_r, 1, tm_r)], axis=2)
            pos = per_tile(pos1, pos2)
            gates = per_tile(route[:, ROUTE_G1], route[:, ROUTE_G2])

            xs = _dispatch(_row_view(h2), pos, n_rows, tm=tm_r).reshape(n_rows, d_model)
            act = _grouped_matmul(_grouped_up_body, xs,
                                  (moe_w_gate[idx].astype(BF16), moe_w_up[idx].astype(BF16)),
                                  tile_expert, n_valid, tmo=tmo, tn=512, name="moe_up")
            ys = _grouped_matmul(_grouped_down_body, act, (moe_w_down[idx].astype(BF16),),
                                 tile_expert, n_valid, tmo=tmo, tn=1024, name="moe_down")
            xf = _combine(_row_view(ys), pos, gates, _row_view(xf), tm=tm_r).reshape(tokens, d_model)

    return xf.reshape(batch, seq, d_model)
```

```python
import functools
import math

import jax
import jax.numpy as jnp
from jax import lax
from jax.experimental import pallas as pl
from jax.experimental.pallas import tpu as pltpu

F32 = jnp.float32
BF16 = jnp.bfloat16

LANES = 128
ONES_ROWS = 16
LOG2E = 1.4426950408889634
NEG = -1e30
EPS = 1e-6
VMEM_LIMIT = 52 * 1024 * 1024

NUM_BUCKETS = 32
MAX_DISTANCE = 128
ROPE_BASE = 10000.0
TOP_K = 2

_HI = lax.Precision.HIGHEST


def _cparams(n_axes):
    return pltpu.CompilerParams(
        dimension_semantics=("arbitrary",) * n_axes, vmem_limit_bytes=VMEM_LIMIT)


def _mm_body(a_ref, w_ref, *rest, epilogue, n_aux):
    acc = jnp.dot(a_ref[...], w_ref[...], preferred_element_type=F32)
    epilogue(acc, rest[:n_aux], rest[n_aux:])


def _matmul(a, w, *, tm, tn, epilogue, out_shape, out_specs, aux=(), aux_specs=(), name,
            a_cols=None, w_cols=None):
    m = a.shape[0]
    a0, k = a_cols if a_cols is not None else (0, a.shape[1])
    w0, n = w_cols if w_cols is not None else (0, w.shape[1])
    if a0 % k:
        a, a0 = a[:, a0:a0 + k], 0
    if w0 % tn:
        w, w0 = w[:, w0:w0 + n], 0
    assert k == w.shape[0]
    assert m % tm == 0 and n % tn == 0, (m, n, tm, tn)
    ab, wb = a0 // k, w0 // tn
    return pl.pallas_call(
        functools.partial(_mm_body, epilogue=epilogue, n_aux=len(aux)),
        grid=(m // tm, n // tn),
        in_specs=[pl.BlockSpec((tm, k), lambda i, j: (i, ab)),
                  pl.BlockSpec((k, tn), lambda i, j: (0, wb + j)), *aux_specs],
        out_specs=out_specs, out_shape=out_shape,
        compiler_params=_cparams(2), name=name)(a, w, *aux)


def _sumsq_lanes(x):
    return jnp.sum(x * x, axis=-1, keepdims=True)


def _ep_residual(acc, aux, outs):
    outs[0][...] = aux[0][...] + acc


def _out_proj_body(*refs, widths):
    part_refs = refs[:len(widths)]
    w_ref, x_ref, o_ref = refs[len(widths):]
    acc = x_ref[...]
    row = 0
    for p_ref, width in zip(part_refs, widths):
        acc = acc + jnp.dot(p_ref[...], w_ref[row:row + width, :], preferred_element_type=F32)
        row += width
    o_ref[...] = acc


def _out_proj(parts, w, x, *, tm, tn):
    m, n = x.shape
    widths = tuple(p.shape[1] for p in parts)
    res = pl.BlockSpec((tm, tn), lambda i, j: (i, j))
    return pl.pallas_call(
        functools.partial(_out_proj_body, widths=widths), grid=(m // tm, n // tn),
        in_specs=[pl.BlockSpec((tm, wd), lambda i, j: (i, 0)) for wd in widths]
        + [pl.BlockSpec((w.shape[0], tn), lambda i, j: (0, j)), res],
        out_specs=res, out_shape=jax.ShapeDtypeStruct((m, n), F32),
        compiler_params=_cparams(2), name="out_proj")(*parts, w, x)


def _swiglu_up_body(a_ref, wg_ref, wu_ref, o_ref):
    a = a_ref[...]
    g = jnp.dot(a, wg_ref[...], preferred_element_type=F32)
    u = jnp.dot(a, wu_ref[...], preferred_element_type=F32)
    o_ref[...] = (g * (1.0 / (1.0 + jnp.exp(-g))) * u).astype(o_ref.dtype)


def _swiglu_up(a, wg, wu, *, tm, th, name):
    m, k = a.shape
    f = wg.shape[1]
    w_spec = pl.BlockSpec((k, th), lambda i, j: (0, j))
    return pl.pallas_call(
        _swiglu_up_body, grid=(m // tm, f // th),
        in_specs=[pl.BlockSpec((tm, k), lambda i, j: (i, 0)), w_spec, w_spec],
        out_specs=pl.BlockSpec((tm, th), lambda i, j: (i, j)),
        out_shape=jax.ShapeDtypeStruct((m, f), BF16),
        compiler_params=_cparams(2), name=name)(a, wg, wu)


def _ep_group_norm(acc, aux, outs, *, group):
    gain = aux[0][...]
    tn = acc.shape[1]
    lane = lax.broadcasted_iota(jnp.int32, (1, LANES), 1)
    for c in range(tn // LANES):
        x = acc[:, c * LANES:(c + 1) * LANES]
        g = gain[:, c * LANES:(c + 1) * LANES]
        if group == LANES:
            r = lax.rsqrt(_sumsq_lanes(x) * (1.0 / LANES) + EPS)
        else:
            lo = lane < group
            x2 = x * x
            ss_lo = jnp.sum(jnp.where(lo, x2, 0.0), axis=-1, keepdims=True)
            ss_hi = jnp.sum(jnp.where(lo, 0.0, x2), axis=-1, keepdims=True)
            r = jnp.where(lo, lax.rsqrt(ss_lo * (1.0 / group) + EPS),
                          lax.rsqrt(ss_hi * (1.0 / group) + EPS))
        outs[0][:, c * LANES:(c + 1) * LANES] = (x * r * g).astype(outs[0].dtype)


def _ep_transpose_heads(acc, aux, outs, *, t):
    tm, tn = acc.shape
    ones = jnp.ones((ONES_ROWS, t), outs[0].dtype)
    for hh in range(tn // LANES):
        for s in range(tm // t):
            blk = acc[s * t:(s + 1) * t, hh * LANES:(hh + 1) * LANES]
            outs[0][hh, s, :LANES, :] = blk.T.astype(outs[0].dtype)
            outs[0][hh, s, LANES:, :] = ones


def _ep_latent_norms(acc, aux, outs, *, n_q):
    gain = aux[0][...]
    n = acc.shape[1]
    xq = acc[:, :n_q]
    xk = acc[:, n_q:]
    rq = lax.rsqrt(_sumsq_lanes(xq) * (1.0 / n_q) + EPS)
    rk = lax.rsqrt(_sumsq_lanes(xk) * (1.0 / (n - n_q)) + EPS)
    outs[0][:, :n_q] = (xq * rq * gain[:, :n_q]).astype(outs[0].dtype)
    outs[0][:, n_q:] = (xk * rk * gain[:, n_q:]).astype(outs[0].dtype)


def _rope_half_block(y, cos, sin):
    lane = lax.broadcasted_iota(jnp.int32, (1, LANES), 1)
    ra = pltpu.roll(y, 32, axis=1)
    rb = pltpu.roll(y, 96, axis=1)
    rot = jnp.where(lane < 32, -rb, ra)
    return y * cos + rot * sin


def _ep_tail(acc, aux, outs, *, rope_dim, n_gate):
    gain = aux[0][...]
    bias = aux[1][...]
    cos = aux[2][...]
    sin = aux[3][...]
    shift_col = aux[4][...]
    lane = lax.broadcasted_iota(jnp.int32, (1, LANES), 1)
    is_pe = lane < rope_dim
    xpe = jnp.where(is_pe, acc, 0.0)
    r = lax.rsqrt(_sumsq_lanes(xpe) * (1.0 / rope_dim) + EPS)
    y = xpe * r * gain
    outs[0][...] = (_rope_half_block(y, cos, sin) + shift_col).astype(outs[0].dtype)
    z = acc + bias
    logsig = -(jnp.maximum(-z, 0.0) + jnp.log(1.0 + jnp.exp(-jnp.abs(z))))
    is_gate = (lane >= rope_dim) & (lane < rope_dim + n_gate)
    outs[1][...] = jnp.where(is_gate, logsig, 0.0)


def _ep_latent_q(acc, aux, outs):
    g_nope = aux[0][...]
    g_rope = aux[1][...]
    cos = aux[2][...]
    sin = aux[3][...]
    one_col = aux[4][...]
    for hh in range(acc.shape[1] // (2 * LANES)):
        xn = acc[:, hh * 256:hh * 256 + LANES]
        xr = acc[:, hh * 256 + LANES:(hh + 1) * 256]
        rn = lax.rsqrt(_sumsq_lanes(xn) * (1.0 / LANES) + EPS)
        outs[0][:, hh * 256:hh * 256 + LANES] = (xn * rn * g_nope).astype(outs[0].dtype)
        rr = lax.rsqrt(_sumsq_lanes(xr) * (1.0 / 64) + EPS)
        yr = _rope_half_block(xr * rr * g_rope, cos, sin) + one_col
        outs[0][:, hh * 256 + LANES:(hh + 1) * 256] = yr.astype(outs[0].dtype)


def _ep_latent_k(acc, aux, outs):
    g_nope = aux[0][...]
    kpe = aux[1][...]
    for hh in range(acc.shape[1] // LANES):
        xn = acc[:, hh * LANES:(hh + 1) * LANES]
        rn = lax.rsqrt(_sumsq_lanes(xn) * (1.0 / LANES) + EPS)
        outs[0][:, hh * 256:hh * 256 + LANES] = (xn * rn * g_nope).astype(outs[0].dtype)
        outs[0][:, hh * 256 + LANES:(hh + 1) * 256] = kpe


def _norm_body(x_ref, g_ref, h_ref):
    x = x_ref[...]
    d = x.shape[1]
    r = lax.rsqrt(_sumsq_lanes(x) * (1.0 / d) + EPS)
    h_ref[...] = (x * r * g_ref[...]).astype(h_ref.dtype)


def _rms_norm(x, gain, *, tm, name):
    t, d = x.shape
    return pl.pallas_call(
        _norm_body, grid=(t // tm,),
        in_specs=[pl.BlockSpec((tm, d), lambda i: (i, 0)), pl.BlockSpec((1, d), lambda i: (0, 0))],
        out_specs=pl.BlockSpec((tm, d), lambda i: (i, 0)),
        out_shape=jax.ShapeDtypeStruct((t, d), BF16),
        compiler_params=_cparams(1), name=name)(x, gain.reshape(1, d))


ROUTE_I1, ROUTE_I2, ROUTE_G1, ROUTE_G2, ROUTE_R1, ROUTE_R2 = range(6)


def _norm_router_body(x_ref, g_ref, wr_ref, h_ref, route_ref, count_ref, carry_ref, *, n_exp):
    @pl.when(pl.program_id(0) == 0)
    def _():
        carry_ref[...] = jnp.zeros_like(carry_ref)

    x = x_ref[...]
    tm, d = x.shape
    r = lax.rsqrt(_sumsq_lanes(x) * (1.0 / d) + EPS)
    h = x * r * g_ref[...]
    h_ref[...] = h.astype(h_ref.dtype)
    logits = jnp.dot(h, wr_ref[...], preferred_element_type=F32, precision=_HI)
    lane = lax.broadcasted_iota(jnp.int32, logits.shape, 1).astype(F32)
    lg = jnp.where(lane < n_exp, logits, -jnp.inf)
    m1 = jnp.max(lg, axis=-1, keepdims=True)
    i1 = jnp.min(jnp.where(lg == m1, lane, float(LANES)), axis=-1, keepdims=True)
    lg2 = jnp.where(lane == i1, -jnp.inf, lg)
    m2 = jnp.max(lg2, axis=-1, keepdims=True)
    i2 = jnp.min(jnp.where(lg2 == m2, lane, float(LANES)), axis=-1, keepdims=True)
    e2 = jnp.exp(m2 - m1)
    g1 = 1.0 / (1.0 + e2)
    g2 = e2 * g1

    chosen = jnp.where((lane == i1) | (lane == i2), 1.0, 0.0)
    rr = lax.broadcasted_iota(jnp.int32, (tm, tm), 0)
    cc = lax.broadcasted_iota(jnp.int32, (tm, tm), 1)
    strict = jnp.where(rr > cc, 1.0, 0.0).astype(BF16)
    before = jnp.dot(strict, chosen.astype(BF16), preferred_element_type=F32) + carry_ref[...]
    rank1 = jnp.sum(jnp.where(lane == i1, before, 0.0), axis=-1, keepdims=True)
    rank2 = jnp.sum(jnp.where(lane == i2, before, 0.0), axis=-1, keepdims=True)
    carry_ref[...] += jnp.sum(chosen, axis=0, keepdims=True)
    count_ref[...] = carry_ref[...]

    rec = jnp.zeros((tm, LANES), F32)
    for k, v in ((ROUTE_I1, i1), (ROUTE_I2, i2), (ROUTE_G1, g1), (ROUTE_G2, g2), (ROUTE_R1, rank1), (ROUTE_R2, rank2)):
        rec = jnp.where(lane == float(k), v, rec)
    route_ref[...] = rec


def _rms_norm_router(x, gain, router, *, tm, name):
    t, d = x.shape
    n_exp = router.shape[1]
    wr = jnp.zeros((d, LANES), F32).at[:, :n_exp].set(router.astype(F32))
    return pl.pallas_call(
        functools.partial(_norm_router_body, n_exp=n_exp), grid=(t // tm,),
        in_specs=[pl.BlockSpec((tm, d), lambda i: (i, 0)), pl.BlockSpec((1, d), lambda i: (0, 0)),
                  pl.BlockSpec((d, LANES), lambda i: (0, 0))],
        out_specs=[pl.BlockSpec((tm, d), lambda i: (i, 0)),
                   pl.BlockSpec((tm, LANES), lambda i: (i, 0)),
                   pl.BlockSpec((1, LANES), lambda i: (0, 0))],
        out_shape=[jax.ShapeDtypeStruct((t, d), BF16),
                   jax.ShapeDtypeStruct((t, LANES), F32),
                   jax.ShapeDtypeStruct((1, LANES), F32)],
        scratch_shapes=[pltpu.VMEM((1, LANES), F32)],
        compiler_params=_cparams(1), name=name)(x, gain.reshape(1, d), wr)


def _row_view(a):
    rows, d = a.shape
    return a.reshape(rows, d // LANES, LANES)


def _dispatch_body(src_ref, h_ref, o_ref, sem, *, tm):
    def issue(r, carry):
        pltpu.make_async_copy(h_ref.at[src_ref[0, 0, r]], o_ref.at[r], sem).start()
        return carry

    def drain(r, carry):
        pltpu.make_async_copy(h_ref.at[0], o_ref.at[0], sem).wait()
        return carry

    lax.fori_loop(0, tm, issue, 0)
    lax.fori_loop(0, tm, drain, 0)


def _dispatch(h_rows, src, *, tm):
    n_tiles = src.shape[0]
    chunks = h_rows.shape[1]
    return pl.pallas_call(
        functools.partial(_dispatch_body, tm=tm), grid=(n_tiles,),
        in_specs=[pl.BlockSpec((1, 1, tm), lambda i: (i, 0, 0), memory_space=pltpu.SMEM),
                  pl.BlockSpec(memory_space=pl.ANY)],
        out_specs=pl.BlockSpec((tm, chunks, LANES), lambda i: (i, 0, 0)),
        out_shape=jax.ShapeDtypeStruct((n_tiles * tm, chunks, LANES), h_rows.dtype),
        scratch_shapes=[pltpu.SemaphoreType.DMA(())],
        compiler_params=_cparams(1), name="moe_dispatch")(src, h_rows)


def _combine_body(pos_ref, gate_ref, ys_ref, x_ref, o_ref, buf, sem, *, tm):
    def issue(r, carry):
        pltpu.make_async_copy(ys_ref.at[pos_ref[0, 0, r]], buf.at[0, r], sem).start()
        pltpu.make_async_copy(ys_ref.at[pos_ref[0, 0, tm + r]], buf.at[1, r], sem).start()
        return carry

    def drain(r, carry):
        pltpu.make_async_copy(ys_ref.at[0], buf.at[0, 0], sem).wait()
        pltpu.make_async_copy(ys_ref.at[0], buf.at[1, 0], sem).wait()
        return carry

    def mix(r, carry):
        o_ref[r] = (x_ref[r] + gate_ref[0, 0, r] * buf[0, r].astype(F32)
                    + gate_ref[0, 0, tm + r] * buf[1, r].astype(F32))
        return carry

    lax.fori_loop(0, tm, issue, 0)
    lax.fori_loop(0, tm, drain, 0)
    lax.fori_loop(0, tm, mix, 0)


def _combine(ys_rows, pos, gates, x_rows, *, tm):
    n_tiles = pos.shape[0]
    chunks = x_rows.shape[1]
    tile = pl.BlockSpec((tm, chunks, LANES), lambda i: (i, 0, 0))
    smem_tile = pl.BlockSpec((1, 1, 2 * tm), lambda i: (i, 0, 0), memory_space=pltpu.SMEM)
    return pl.pallas_call(
        functools.partial(_combine_body, tm=tm), grid=(n_tiles,),
        in_specs=[smem_tile, smem_tile, pl.BlockSpec(memory_space=pl.ANY), tile],
        out_specs=tile,
        out_shape=jax.ShapeDtypeStruct(x_rows.shape, F32),
        scratch_shapes=[pltpu.VMEM((2, tm, chunks, LANES), ys_rows.dtype), pltpu.SemaphoreType.DMA(())],
        compiler_params=_cparams(1), name="moe_combine")(pos, gates, ys_rows, x_rows)


def _grouped_up_body(te_ref, nv_ref, a_ref, wg_ref, wu_ref, o_ref):
    @pl.when(pl.program_id(1) < nv_ref[0])
    def _():
        a = a_ref[...]
        g = jnp.dot(a, wg_ref[...], preferred_element_type=F32)
        u = jnp.dot(a, wu_ref[...], preferred_element_type=F32)
        o_ref[...] = (g * (1.0 / (1.0 + jnp.exp(-g))) * u).astype(o_ref.dtype)

    @pl.when(pl.program_id(1) >= nv_ref[0])
    def _():
        o_ref[...] = jnp.zeros_like(o_ref)


def _grouped_down_body(te_ref, nv_ref, a_ref, w_ref, o_ref):
    @pl.when(pl.program_id(1) < nv_ref[0])
    def _():
        o_ref[...] = jnp.dot(a_ref[...], w_ref[...], preferred_element_type=F32).astype(o_ref.dtype)

    @pl.when(pl.program_id(1) >= nv_ref[0])
    def _():
        o_ref[...] = jnp.zeros_like(o_ref)


def _grouped_matmul(body, a, weights, tile_expert, n_valid, *, tmo, tn, name):
    r, k = a.shape
    n = weights[0].shape[2]

    def row(j, i, te, nv):
        return jnp.minimum(i, nv[0] - 1)

    grid_spec = pltpu.PrefetchScalarGridSpec(
        num_scalar_prefetch=2, grid=(n // tn, r // tmo),
        in_specs=[pl.BlockSpec((tmo, k), lambda j, i, te, nv: (row(j, i, te, nv), 0))]
        + [pl.BlockSpec((None, k, tn), lambda j, i, te, nv: (te[row(j, i, te, nv)], 0, j)) for _ in weights],
        out_specs=pl.BlockSpec((tmo, tn), lambda j, i, te, nv: (i, j)))
    return pl.pallas_call(
        body, grid_spec=grid_spec, out_shape=jax.ShapeDtypeStruct((r, n), BF16),
        compiler_params=_cparams(2), name=name)(tile_expert, n_valid, a, *weights)


def _rope_body(pos_ref, f_ref, cos_ref, sin_ref):
    ang = pos_ref[...] * f_ref[...]
    live = f_ref[...] > 0.0
    cos_ref[...] = jnp.where(live, jnp.cos(ang), 0.0)
    sin_ref[...] = jnp.where(live, jnp.sin(ang), 0.0)


def _rope_tables(positions, *, half, tm):
    t = positions.size
    inv_freq = ROPE_BASE ** (-jnp.arange(half, dtype=F32) / half)
    f_row = jnp.concatenate([inv_freq, inv_freq, jnp.zeros((LANES - 2 * half,), F32)]).reshape(1, LANES)
    pos_rep = jnp.broadcast_to(positions.astype(F32).reshape(t, 1), (t, LANES))
    spec = pl.BlockSpec((tm, LANES), lambda i: (i, 0))
    return pl.pallas_call(
        _rope_body, grid=(t // tm,),
        in_specs=[spec, pl.BlockSpec((1, LANES), lambda i: (0, 0))],
        out_specs=[spec, spec],
        out_shape=[jax.ShapeDtypeStruct((t, LANES), F32)] * 2,
        compiler_params=_cparams(1), name="rope_tables")(pos_rep, f_row)


def _bias_body(rb_ref, o_ref, *, t):
    h = pl.program_id(0)
    d = pl.program_id(1)
    j = lax.broadcasted_iota(jnp.int32, (t, t), 0)
    i = lax.broadcasted_iota(jnp.int32, (t, t), 1)
    rel = d * t + i - j
    n = jnp.maximum(rel, 0)
    max_exact = NUM_BUCKETS // 2
    large = max_exact + (
        jnp.log(jnp.maximum(n, max_exact).astype(F32) / max_exact)
        / math.log(MAX_DISTANCE / max_exact) * (NUM_BUCKETS - max_exact)
    ).astype(jnp.int32)
    large = jnp.minimum(large, NUM_BUCKETS - 1)
    bucket = jnp.where(n < max_exact, n, large)
    far = rb_ref[NUM_BUCKETS - 1, h]
    val = jnp.zeros((t, t), F32)
    for b in range(NUM_BUCKETS - 1):
        val = jnp.where(bucket == b, rb_ref[b, h] - far, val)
    o_ref[...] = jnp.where(rel >= 0, val * LOG2E, NEG)


def _bias_tiles(rel_bias, *, t):
    assert t + 1 >= MAX_DISTANCE, "tiles beyond the first sub-diagonal must be in the last bucket"
    nh = rel_bias.shape[1]
    return pl.pallas_call(
        functools.partial(_bias_body, t=t), grid=(nh, 2),
        in_specs=[pl.BlockSpec(memory_space=pltpu.SMEM)],
        out_specs=pl.BlockSpec((None, None, t, t), lambda h, d: (h, d, 0, 0)),
        out_shape=jax.ShapeDtypeStruct((nh, 2, t, t), F32),
        compiler_params=_cparams(2), name="t5_bias_tiles")(rel_bias.astype(F32))


N_PIECES = 3


def _split_pieces(x):
    pieces = []
    rest = x
    for _ in range(N_PIECES):
        p = rest.astype(BF16).astype(F32)
        pieces.append(p)
        rest = rest - p
    return pieces


def _cum_body(shift_ref, lf_ref, o_ref, carry_ref, *, n_heads, lane0):
    @pl.when(pl.program_id(1) == 0)
    def _():
        carry_ref[...] = jnp.zeros_like(carry_ref)

    x = lf_ref[...]
    tb = x.shape[0]
    r = lax.broadcasted_iota(jnp.int32, (tb, tb), 0)
    c = lax.broadcasted_iota(jnp.int32, (tb, tb), 1)
    tri = jnp.where(r >= c, 1.0, 0.0).astype(F32)
    er = lax.broadcasted_iota(jnp.int32, (LANES, LANES), 0)
    lane = lax.broadcasted_iota(jnp.int32, (tb, LANES), 1)
    shift = shift_ref[0]
    for hh in range(n_heads):
        sel = jnp.where(er == lane0 + hh, 1.0, 0.0).astype(F32)
        xh = jnp.dot(x, sel, preferred_element_type=F32, precision=_HI)
        cum = jnp.dot(tri, xh, preferred_element_type=F32, precision=_HI) + carry_ref[hh:hh + 1, :]
        carry_ref[hh:hh + 1, :] = cum[tb - 1:tb, :]
        cum2 = cum * LOG2E
        qcols = jnp.where(lane < N_PIECES, 1.0, 0.0)
        kcols = jnp.where((lane >= N_PIECES) & (lane < 2 * N_PIECES), 1.0, 0.0)
        for n, (pq, pk) in enumerate(zip(_split_pieces(cum2 - shift), _split_pieces(-cum2))):
            qcols = jnp.where(lane == N_PIECES + n, pq, qcols)
            kcols = jnp.where(lane == n, pk, kcols)
        o_ref[:, hh * LANES:(hh + 1) * LANES] = qcols.astype(o_ref.dtype)
        o_ref[:, (n_heads + hh) * LANES:(n_heads + hh + 1) * LANES] = kcols.astype(o_ref.dtype)


def _forget_columns(logf, shift, *, batch, seq, n_heads, lane0, tb):
    nb = seq // tb
    width = 2 * n_heads * LANES
    return pl.pallas_call(
        functools.partial(_cum_body, n_heads=n_heads, lane0=lane0), grid=(batch, nb),
        in_specs=[pl.BlockSpec(memory_space=pltpu.SMEM),
                  pl.BlockSpec((tb, LANES), lambda b, i: (b * nb + i, 0))],
        out_specs=pl.BlockSpec((tb, width), lambda b, i: (b * nb + i, 0)),
        out_shape=jax.ShapeDtypeStruct((batch * seq, width), BF16),
        scratch_shapes=[pltpu.VMEM((n_heads, LANES), F32)],
        compiler_params=_cparams(2), name="forget_cumsum")(shift.reshape(1).astype(F32), logf)


def _online_block(s, vb, m_ref, acc_ref):
    m_prev = m_ref[...]
    m_new = jnp.maximum(m_prev, jnp.max(s, axis=0, keepdims=True))
    alpha = jnp.exp2(m_prev - m_new)
    p = jnp.exp2(s - m_new).astype(BF16)
    acc_ref[...] = alpha * acc_ref[...] + jnp.dot(vb, p, preferred_element_type=F32)
    m_ref[...] = m_new


def _shifted_block(s, vb, acc_ref):
    p = jnp.exp2(s).astype(BF16)
    acc_ref[...] += jnp.dot(vb, p, preferred_element_type=F32)


def _causal_mask(t):
    j = lax.broadcasted_iota(jnp.int32, (t, t), 0)
    i = lax.broadcasted_iota(jnp.int32, (t, t), 1)
    return j <= i


_NT = (((1,), (1,)), ((), ()))
FAR_UNROLL = 4


def _init_state(m_ref, acc_ref):
    m_ref[...] = jnp.full(m_ref.shape, NEG, F32)
    acc_ref[...] = jnp.zeros(acc_ref.shape, F32)


def _attn_body(*refs, mode, t, hp, dq, dv, qk_dim=None, lambda_init=None):
    safe_ref, q_ref, k_ref, vt_ref = refs[:4]
    if mode == "latent":
        o_ref, qc_ref, m_ref, acc_ref = refs[4:]
    elif mode == "forget":
        qx_ref, kx_ref, o_ref, qc_ref, m_ref, acc_ref = refs[4:]
    else:
        qx_ref, kx_ref, bias_ref, lam_ref, hn_ref, o_ref, qc_ref, m_ref, acc_ref = refs[4:]
    n_map = 2 if mode == "diff" else 1
    qi = pl.program_id(2)
    _init_state(m_ref, acc_ref)

    for hh in range(hp):
        q = q_ref[:, hh * dq:(hh + 1) * dq]
        if mode == "latent":
            qc_ref[hh] = q
        elif mode == "forget":
            qc_ref[hh] = jnp.concatenate([q, qx_ref[:, hh * LANES:(hh + 1) * LANES]], axis=1)
        else:
            lane = lax.broadcasted_iota(jnp.int32, (t, dq), 1)
            zero = jnp.zeros_like(q)
            qc_ref[2 * hh] = jnp.concatenate([jnp.where(lane < qk_dim, q, zero), qx_ref[...]], axis=1)
            qc_ref[2 * hh + 1] = jnp.concatenate([jnp.where(lane < qk_dim, zero, q), qx_ref[...]], axis=1)

    def keys(kj, hh):
        start = pl.multiple_of(kj * t, t)
        kb = k_ref[pl.ds(start, t), hh * dq:(hh + 1) * dq]
        if mode == "forget":
            kb = jnp.concatenate([kb, kx_ref[pl.ds(start, t), hh * LANES:(hh + 1) * LANES]], axis=1)
        elif mode == "diff":
            kb = jnp.concatenate([kb, kx_ref[hh]], axis=1)
        return kb

    def run(update):
        def scores(kj, kind, hh, mm):
            s = lax.dot_general(keys(kj, hh), qc_ref[n_map * hh + mm], _NT, preferred_element_type=F32)
            if mode == "diff":
                if kind != "far":
                    s = s + bias_ref[hh, 0 if kind == "diag" else 1]
            elif kind == "diag":
                s = jnp.where(_causal_mask(t), s, NEG)
            return s

        def blocks(items):
            work = [(kj, kind, hh, mm) for kj, kind in items for hh in range(hp) for mm in range(n_map)]
            s_next = scores(*work[0])
            for n, (kj, kind, hh, mm) in enumerate(work):
                s_cur = s_next
                if n + 1 < len(work):
                    s_next = scores(*work[n + 1])
                update(s_cur, vt_ref[hh, kj], n_map * hh + mm)

        def far_group(g, carry):
            blocks([(g * FAR_UNROLL + u, "far") for u in range(FAR_UNROLL)])
            return carry

        n_far = jnp.maximum(qi - 1, 0) if mode == "diff" else qi
        n_groups = n_far // FAR_UNROLL
        lax.fori_loop(0, n_groups, far_group, 0)

        def far_single(kj, carry):
            blocks([(kj, "far")])
            return carry

        lax.fori_loop(n_groups * FAR_UNROLL, n_far, far_single, 0)

        if mode == "diff":
            @pl.when(qi >= 1)
            def _():
                blocks([(qi - 1, "sub")])
        blocks([(qi, "diag")])

    @pl.when(safe_ref[0] != 0)
    def _():
        run(lambda s, vb, c: _shifted_block(s, vb, acc_ref.at[c]))

    @pl.when(safe_ref[0] == 0)
    def _():
        run(lambda s, vb, c: _online_block(s, vb, m_ref.at[c], acc_ref.at[c]))

    if mode == "diff":
        lp = lam_ref[...]
        lam = (jnp.exp(jnp.sum(lp[0:1] * lp[1:2], axis=-1, keepdims=True))
               - jnp.exp(jnp.sum(lp[2:3] * lp[3:4], axis=-1, keepdims=True)) + lambda_init)
    for hh in range(hp):
        if mode == "diff":
            a0 = acc_ref[2 * hh]
            a1 = acc_ref[2 * hh + 1]
            o = a0[:dv] * (1.0 / a0[dv:dv + 1]) - lam * (a1[:dv] * (1.0 / a1[dv:dv + 1]))
            r = lax.rsqrt(jnp.sum(o * o, axis=0, keepdims=True) * (1.0 / dv) + EPS)
            o = o * r * hn_ref[...] * (1.0 - lambda_init)
        else:
            a = acc_ref[hh]
            o = a[:dv] * (1.0 / a[dv:dv + 1])
        o_ref[:, hh * dv:(hh + 1) * dv] = o.T.astype(o_ref.dtype)


def _attention(mode, safe, q_arr, k_arr, vt_arr, *, batch, seq, n_heads, t, hp, dq, q_col0, k_col0,
               extra=(), extra_specs=(), body_kwargs=None, name):
    nq = seq // t
    dvp = vt_arr.shape[2]
    dv = dvp - ONES_ROWS
    tokens = batch * seq
    assert n_heads % hp == 0 and q_col0 % hp == 0 and k_col0 % hp == 0
    qc, kc = q_col0 // hp, k_col0 // hp
    in_specs = [
        pl.BlockSpec(memory_space=pltpu.SMEM),
        pl.BlockSpec((t, hp * dq), lambda b, h, i: (b * nq + i, qc + h)),
        pl.BlockSpec((seq, hp * dq), lambda b, h, i: (b, kc + h)),
        pl.BlockSpec((hp, nq, dvp, t), lambda b, h, i: (h, b, 0, 0)),
        *extra_specs,
    ]
    n_chain = 2 * hp if mode == "diff" else hp
    scratch = [pltpu.VMEM((n_chain, t, 2 * LANES), BF16),
               pltpu.VMEM((n_chain, 1, t), F32), pltpu.VMEM((n_chain, dvp, t), F32)]
    body = functools.partial(_attn_body, mode=mode, t=t, hp=hp, dq=dq, dv=dv, **(body_kwargs or {}))
    return pl.pallas_call(
        body, grid=(batch, n_heads // hp, nq), in_specs=in_specs,
        out_specs=pl.BlockSpec((t, hp * dv), lambda b, h, i: (b * nq + i, h)),
        out_shape=jax.ShapeDtypeStruct((tokens, n_heads * dv), BF16),
        scratch_shapes=scratch, compiler_params=_cparams(3), name=name)(safe, q_arr, k_arr, vt_arr, *extra)


def _pick(n, prefs):
    for p in prefs:
        if n % p == 0:
            return p
    raise ValueError((n, prefs))


SAFE_DEPTH = 110.0


def _round_up_bf16(c):
    return (c * (1.0 + 2.0 ** -6)).astype(BF16).astype(F32)


def _safe_flag(depth):
    return (depth <= SAFE_DEPTH).astype(jnp.int32).reshape(1)


def kernel(x, positions, attn_norm, w_in, b_forget, a_qk_norm, a_lambda, a_head_norm, rel_bias,
           b_qk_norm, c_q_a_norm, c_kv_a_norm, c_w_uq, c_w_ukv, c_qk_norm_nope, c_qk_norm_rope,
           w_out, ffn_norm, dense_w_gate, dense_w_up, dense_w_down, moe_router, moe_w_gate,
           moe_w_up, moe_w_down):
    batch, seq, d_model = x.shape
    depth = w_in.shape[0]
    tokens = batch * seq
    n_slots = d_model // LANES
    a_heads = n_slots // 4
    b_heads = n_slots // 4
    c_heads = n_slots // 2
    a_qk = a_qk_norm.shape[-1]
    b_dim = b_qk_norm.shape[-1]
    c_q_lora = c_q_a_norm.shape[-1]
    c_kv_lora = c_kv_a_norm.shape[-1]
    c_nope = c_qk_norm_nope.shape[-1]
    c_rope = c_qk_norm_rope.shape[-1]
    c_v = c_w_ukv.shape[-1] // c_heads - c_nope
    n_exp = moe_router.shape[-1]
    assert 2 * a_qk == LANES and b_dim == LANES and c_nope == LANES and c_v == LANES and 2 * c_rope == LANES

    t = _pick(seq, (512, 256, 128))
    tm = _pick(tokens, (1024, 512, 256))
    tm_n = _pick(tokens, (256, 128))
    hp_a, hp_b, hp_c = 1, 2, 2

    sizes = (a_heads * 2 * a_qk, a_heads * 2 * a_qk, a_heads * LANES, b_heads * b_dim, b_heads * b_dim,
             b_heads * b_dim, b_heads, c_q_lora, c_kv_lora, c_rope)
    offs = [0]
    for s in sizes:
        offs.append(offs[-1] + s)
    o_aq, o_ak, o_av, o_bq, o_bk, o_bv, o_f, o_cq, o_ckv, o_kpe, o_end = offs

    xf = x.reshape(tokens, d_model)
    cos_t, sin_t = _rope_tables(positions, half=c_rope // 2, tm=tm)
    bias_tiles = _bias_tiles(rel_bias, t=t)
    bias_log2 = (rel_bias.astype(F32) - rel_bias[-1:].astype(F32)) * LOG2E
    bias_max = jnp.max(bias_log2, axis=0)
    qx_a = jnp.zeros((t, LANES), F32).at[:, 0].set(1.0).astype(BF16)

    row_spec = lambda width: pl.BlockSpec((1, width), lambda i, j: (0, j))
    fixed_row = pl.BlockSpec((1, LANES), lambda i, j: (0, 0))
    tok_lane = pl.BlockSpec((tm, LANES), lambda i, j: (i, 0))

    def vt_call(h_bf, w, nh, name, a_cols=None, w_cols=None):
        tn = _pick(nh * LANES, (512, 256, 128))
        return _matmul(
            h_bf, w, a_cols=a_cols, w_cols=w_cols, tm=tm, tn=tn,
            epilogue=functools.partial(_ep_transpose_heads, t=t),
            out_shape=jax.ShapeDtypeStruct((nh, tokens // t, LANES + ONES_ROWS, t), BF16),
            out_specs=pl.BlockSpec((tn // LANES, tm // t, LANES + ONES_ROWS, t), lambda i, j: (j, i, 0, 0)),
            name=name)

    for layer in range(depth):
        lambda_init = 0.8 - 0.6 * math.exp(-0.3 * layer)
        w = w_in[layer].astype(BF16)

        h = _rms_norm(xf, attn_norm[layer], tm=tm_n, name="attn_norm")

        a_scale = a_qk ** -0.5 * LOG2E
        gain_a = jnp.concatenate([jnp.tile(a_qk_norm[layer, 0] * a_scale, 2 * a_heads),
                                  jnp.tile(a_qk_norm[layer, 1], 2 * a_heads)]).reshape(1, -1)
        tn = 512
        qk_a = _matmul(h, w, w_cols=(o_aq, o_av - o_aq), tm=tm, tn=tn,
                       epilogue=functools.partial(_ep_group_norm, group=a_qk),
                       aux=(gain_a,), aux_specs=(row_spec(tn),),
                       out_shape=jax.ShapeDtypeStruct((tokens, o_av - o_aq), BF16),
                       out_specs=pl.BlockSpec((tm, tn), lambda i, j: (i, j)), name="proj_a_qk")
        vt_a = vt_call(h, w, a_heads, "proj_a_v", w_cols=(o_av, o_bq - o_av))
        hn = jnp.broadcast_to(a_head_norm[layer].reshape(LANES, 1), (LANES, t))
        bound_a = a_qk * jnp.max(jnp.abs(a_qk_norm[layer, 0] * a_scale)) * jnp.max(jnp.abs(a_qk_norm[layer, 1]))
        shift_a = _round_up_bf16(bound_a + bias_max)
        safe_a = _safe_flag(2.0 * bound_a + jnp.max(bias_max - bias_log2[0]))
        kx_a = jnp.zeros((a_heads, t, LANES), F32).at[:, :, 0].set(-shift_a[:, None]).astype(BF16)
        o_a = _attention(
            "diff", safe_a, qk_a, qk_a, vt_a, batch=batch, seq=seq, n_heads=a_heads, t=t, hp=hp_a, dq=LANES,
            q_col0=0, k_col0=a_heads,
            extra=(qx_a, kx_a, bias_tiles, a_lambda[layer], hn),
            extra_specs=(pl.BlockSpec((t, LANES), lambda b, hh, i: (0, 0)),
                         pl.BlockSpec((hp_a, t, LANES), lambda b, hh, i: (hh, 0, 0)),
                         pl.BlockSpec((hp_a, 2, t, t), lambda b, hh, i: (hh, 0, 0, 0)),
                         pl.BlockSpec((4, a_qk), lambda b, hh, i: (0, 0)),
                         pl.BlockSpec((LANES, t), lambda b, hh, i: (0, 0))),
            body_kwargs=dict(qk_dim=a_qk, lambda_init=lambda_init), name="attn_diff")

        b_scale = b_dim ** -0.5 * LOG2E
        gain_b = jnp.concatenate([jnp.tile(b_qk_norm[layer, 0] * b_scale, b_heads),
                                  jnp.tile(b_qk_norm[layer, 1], b_heads)]).reshape(1, -1)
        qk_b = _matmul(h, w, w_cols=(o_bq, o_bv - o_bq), tm=tm, tn=tn,
                       epilogue=functools.partial(_ep_group_norm, group=b_dim),
                       aux=(gain_b,), aux_specs=(row_spec(tn),),
                       out_shape=jax.ShapeDtypeStruct((tokens, o_bv - o_bq), BF16),
                       out_specs=pl.BlockSpec((tm, tn), lambda i, j: (i, j)), name="proj_b_qk")
        vt_b = vt_call(h, w, b_heads, "proj_b_v", w_cols=(o_bv, o_f - o_bv))

        w_tail = jnp.concatenate([w[:, o_kpe:o_end], w[:, o_f:o_cq],
                                  jnp.zeros((d_model, LANES - c_rope - b_heads), BF16)], axis=1)
        gain_tail = jnp.concatenate([c_qk_norm_rope[layer, 1], jnp.zeros((LANES - c_rope,), F32)]).reshape(1, LANES)
        bias_tail = jnp.zeros((LANES,), F32).at[c_rope:c_rope + b_heads].set(b_forget[layer]).reshape(1, LANES)
        c_scale = (c_nope + c_rope) ** -0.5 * LOG2E
        gq_n = c_qk_norm_nope[layer, 0] * c_scale
        gq_r = c_qk_norm_rope[layer, 0] * c_scale
        gk_n = c_qk_norm_nope[layer, 1]
        gk_r = c_qk_norm_rope[layer, 1]
        bound_c = (jnp.sqrt(c_nope * jnp.max(gq_n * gq_n) + c_rope * jnp.max(gq_r * gq_r))
                   * jnp.sqrt(c_nope * jnp.max(gk_n * gk_n) + c_rope * jnp.max(gk_r * gk_r)))
        shift_c = _round_up_bf16(bound_c)
        safe_c = _safe_flag(2.0 * bound_c)
        shift_col = jnp.zeros((1, LANES), F32).at[0, c_rope].set(-shift_c)
        one_col = jnp.zeros((1, LANES), F32).at[0, c_rope].set(1.0)
        kpe, logf = _matmul(
            h, w_tail, tm=tm, tn=LANES,
            epilogue=functools.partial(_ep_tail, rope_dim=c_rope, n_gate=b_heads),
            aux=(gain_tail, bias_tail, cos_t, sin_t, shift_col),
            aux_specs=(fixed_row, fixed_row, tok_lane, tok_lane, fixed_row),
            out_shape=[jax.ShapeDtypeStruct((tokens, LANES), BF16), jax.ShapeDtypeStruct((tokens, LANES), F32)],
            out_specs=[tok_lane, tok_lane], name="proj_tail")
        bound_b = b_dim * jnp.max(jnp.abs(b_qk_norm[layer, 0] * b_scale)) * jnp.max(jnp.abs(b_qk_norm[layer, 1]))
        safe_b = _safe_flag(2.0 * bound_b)
        x_b = _forget_columns(logf, bound_b, batch=batch, seq=seq, n_heads=b_heads, lane0=c_rope,
                              tb=_pick(seq, (256, 128)))
        nbh = b_heads // hp_b
        o_b = _attention(
            "forget", safe_b, qk_b, qk_b, vt_b, batch=batch, seq=seq, n_heads=b_heads, t=t, hp=hp_b, dq=LANES,
            q_col0=0, k_col0=b_heads,
            extra=(x_b, x_b),
            extra_specs=(pl.BlockSpec((t, hp_b * LANES), lambda b, hh, i: (b * (seq // t) + i, hh)),
                         pl.BlockSpec((seq, hp_b * LANES), lambda b, hh, i: (b, nbh + hh))),
            name="attn_forget")

        n_lat = c_q_lora + c_kv_lora
        gain_lat = jnp.concatenate([c_q_a_norm[layer], c_kv_a_norm[layer]]).reshape(1, n_lat)
        tm_lat = _pick(tokens, (512, 256))
        lat = _matmul(h, w[:, o_cq:o_kpe], tm=tm_lat, tn=n_lat,
                      epilogue=functools.partial(_ep_latent_norms, n_q=c_q_lora),
                      aux=(gain_lat,), aux_specs=(row_spec(n_lat),),
                      out_shape=jax.ShapeDtypeStruct((tokens, n_lat), BF16),
                      out_specs=pl.BlockSpec((tm_lat, n_lat), lambda i, j: (i, j)), name="proj_c_latent")
        wq = c_w_uq[layer].astype(BF16).reshape(c_q_lora, c_heads, c_nope + c_rope)
        wq = jnp.concatenate([wq, jnp.zeros((c_q_lora, c_heads, 2 * LANES - c_nope - c_rope), BF16)], axis=2)
        wq = wq.reshape(c_q_lora, c_heads * 2 * LANES)
        gq_nope = gq_n.reshape(1, LANES)
        gq_rope = jnp.concatenate([gq_r, jnp.zeros((LANES - c_rope,), F32)]).reshape(1, LANES)
        tn_q = 512
        lat_q_cols, lat_kv_cols = (0, c_q_lora), (c_q_lora, c_kv_lora)
        q_c = _matmul(lat, wq, a_cols=lat_q_cols, tm=tm, tn=tn_q, epilogue=_ep_latent_q,
                      aux=(gq_nope, gq_rope, cos_t, sin_t, one_col),
                      aux_specs=(fixed_row, fixed_row, tok_lane, tok_lane, fixed_row),
                      out_shape=jax.ShapeDtypeStruct((tokens, c_heads * 2 * LANES), BF16),
                      out_specs=pl.BlockSpec((tm, tn_q), lambda i, j: (i, j)), name="proj_c_q")
        wkv = c_w_ukv[layer].astype(BF16).reshape(c_kv_lora, c_heads, c_nope + c_v)
        wk = wkv[:, :, :c_nope].reshape(c_kv_lora, c_heads * c_nope)
        wv = wkv[:, :, c_nope:].reshape(c_kv_lora, c_heads * c_v)
        gk_nope = c_qk_norm_nope[layer, 1].reshape(1, LANES)
        tn_k = 256
        k_c = _matmul(lat, wk, a_cols=lat_kv_cols, tm=tm, tn=tn_k, epilogue=_ep_latent_k,
                      aux=(gk_nope, kpe), aux_specs=(fixed_row, tok_lane),
                      out_shape=jax.ShapeDtypeStruct((tokens, c_heads * 2 * LANES), BF16),
                      out_specs=pl.BlockSpec((tm, 2 * tn_k), lambda i, j: (i, j)), name="proj_c_k")
        vt_c = vt_call(lat, wv, c_heads, "proj_c_v", a_cols=lat_kv_cols)
        o_c = _attention(
            "latent", safe_c, q_c, k_c, vt_c, batch=batch, seq=seq, n_heads=c_heads, t=t, hp=hp_c, dq=2 * LANES,
            q_col0=0, k_col0=0, name="attn_latent")

        tn_o = 512
        xf = _out_proj((o_a, o_b, o_c), w_out[layer].astype(BF16), xf, tm=tm, tn=tn_o)

        idx = layer // 2
        th = 256
        if layer % 2 == 0:
            h2 = _rms_norm(xf, ffn_norm[layer], tm=tm_n, name="ffn_norm")
            act = _swiglu_up(h2, dense_w_gate[idx].astype(BF16), dense_w_up[idx].astype(BF16),
                             tm=tm, th=th, name="ffn_up")
            tm_d = _pick(tokens, (512, 256))
            res_d = pl.BlockSpec((tm_d, tn_o), lambda i, j: (i, j))
            xf = _matmul(act, dense_w_down[idx].astype(BF16), tm=tm_d, tn=tn_o, epilogue=_ep_residual,
                         aux=(xf,), aux_specs=(res_d,),
                         out_shape=jax.ShapeDtypeStruct((tokens, d_model), F32), out_specs=res_d,
                         name="ffn_down")
        else:
            h2, route, counts = _rms_norm_router(xf, ffn_norm[layer], moe_router[idx], tm=tm_n,
                                                 name="ffn_norm_router")
            tmo = _pick(tokens, (512, 256, 128))
            n_rows = TOP_K * tokens + n_exp * tmo
            cnt = counts[0, :n_exp].astype(jnp.int32)
            padded = (cnt + tmo - 1) // tmo * tmo
            ends = jnp.cumsum(padded)
            starts = ends - padded
            n_valid = (ends[-1] // tmo).reshape(1).astype(jnp.int32)
            tile_expert = jnp.minimum(
                jnp.searchsorted(ends, jnp.arange(n_rows // tmo, dtype=jnp.int32) * tmo, side="right"),
                n_exp - 1).astype(jnp.int32)
            e1 = route[:, ROUTE_I1].astype(jnp.int32)
            e2 = route[:, ROUTE_I2].astype(jnp.int32)
            pos1 = starts[e1] + route[:, ROUTE_R1].astype(jnp.int32)
            pos2 = starts[e2] + route[:, ROUTE_R2].astype(jnp.int32)
            tm_r = _pick(tokens, (256, 128))
            per_tile = lambda a, b: jnp.concatenate(
                [a.reshape(tokens // tm_r, 1, tm_r), b.reshape(tokens // tm_r, 1, tm_r)], axis=2)
            pos = per_tile(pos1, pos2)
            gates = per_tile(route[:, ROUTE_G1], route[:, ROUTE_G2])

            tok = jnp.arange(tokens, dtype=jnp.int32)
            src = jnp.zeros((n_rows,), jnp.int32).at[jnp.concatenate([pos1, pos2])].set(
                jnp.concatenate([tok, tok]), unique_indices=True)
            xs = _dispatch(_row_view(h2), src.reshape(n_rows // tm_r, 1, tm_r), tm=tm_r).reshape(n_rows, d_model)
            act = _grouped_matmul(_grouped_up_body, xs,
                                  (moe_w_gate[idx].astype(BF16), moe_w_up[idx].astype(BF16)),
                                  tile_expert, n_valid, tmo=tmo, tn=512, name="moe_up")
            ys = _grouped_matmul(_grouped_down_body, act, (moe_w_down[idx].astype(BF16),),
                                 tile_expert, n_valid, tmo=tmo, tn=1024, name="moe_down")
            xf = _combine(_row_view(ys), pos, gates, _row_view(xf), tm=tm_r).reshape(tokens, d_model)

    return xf.reshape(batch, seq, d_model)
```

```python
import functools
import math

import jax
import jax.numpy as jnp
from jax import lax
from jax.experimental import pallas as pl
from jax.experimental.pallas import tpu as pltpu

F32 = jnp.float32
BF16 = jnp.bfloat16

LANES = 128
ONES_ROWS = 16
LOG2E = 1.4426950408889634
NEG = -1e30
EPS = 1e-6
VMEM_LIMIT = 52 * 1024 * 1024

NUM_BUCKETS = 32
MAX_DISTANCE = 128
ROPE_BASE = 10000.0
TOP_K = 2

_HI = lax.Precision.HIGHEST


def _cparams(n_axes):
    return pltpu.CompilerParams(
        dimension_semantics=("arbitrary",) * n_axes, vmem_limit_bytes=VMEM_LIMIT)


def _mm_body(a_ref, w_ref, *rest, epilogue, n_aux):
    acc = jnp.dot(a_ref[...], w_ref[...], preferred_element_type=F32)
    epilogue(acc, rest[:n_aux], rest[n_aux:])


def _matmul(a, w, *, tm, tn, epilogue, out_shape, out_specs, aux=(), aux_specs=(), name,
            a_cols=None, w_cols=None):
    m = a.shape[0]
    a0, k = a_cols if a_cols is not None else (0, a.shape[1])
    w0, n = w_cols if w_cols is not None else (0, w.shape[1])
    if a0 % k:
        a, a0 = a[:, a0:a0 + k], 0
    if w0 % tn:
        w, w0 = w[:, w0:w0 + n], 0
    assert k == w.shape[0]
    assert m % tm == 0 and n % tn == 0, (m, n, tm, tn)
    ab, wb = a0 // k, w0 // tn
    return pl.pallas_call(
        functools.partial(_mm_body, epilogue=epilogue, n_aux=len(aux)),
        grid=(m // tm, n // tn),
        in_specs=[pl.BlockSpec((tm, k), lambda i, j: (i, ab)),
                  pl.BlockSpec((k, tn), lambda i, j: (0, wb + j)), *aux_specs],
        out_specs=out_specs, out_shape=out_shape,
        compiler_params=_cparams(2), name=name)(a, w, *aux)


def _sumsq_lanes(x):
    return jnp.sum(x * x, axis=-1, keepdims=True)


def _ep_residual(acc, aux, outs):
    outs[0][...] = aux[0][...] + acc


def _out_proj_body(*refs, widths):
    part_refs = refs[:len(widths)]
    w_ref, x_ref, o_ref = refs[len(widths):]
    acc = x_ref[...]
    row = 0
    for p_ref, width in zip(part_refs, widths):
        acc = acc + jnp.dot(p_ref[...], w_ref[row:row + width, :], preferred_element_type=F32)
        row += width
    o_ref[...] = acc


def _out_proj(parts, w, x, *, tm, tn):
    m, n = x.shape
    widths = tuple(p.shape[1] for p in parts)
    res = pl.BlockSpec((tm, tn), lambda i, j: (i, j))
    return pl.pallas_call(
        functools.partial(_out_proj_body, widths=widths), grid=(m // tm, n // tn),
        in_specs=[pl.BlockSpec((tm, wd), lambda i, j: (i, 0)) for wd in widths]
        + [pl.BlockSpec((w.shape[0], tn), lambda i, j: (0, j)), res],
        out_specs=res, out_shape=jax.ShapeDtypeStruct((m, n), F32),
        compiler_params=_cparams(2), name="out_proj")(*parts, w, x)


def _swiglu_up_body(a_ref, wg_ref, wu_ref, o_ref):
    a = a_ref[...]
    g = jnp.dot(a, wg_ref[...], preferred_element_type=F32)
    u = jnp.dot(a, wu_ref[...], preferred_element_type=F32)
    o_ref[...] = (g * (1.0 / (1.0 + jnp.exp(-g))) * u).astype(o_ref.dtype)


def _swiglu_up(a, wg, wu, *, tm, th, name):
    m, k = a.shape
    f = wg.shape[1]
    w_spec = pl.BlockSpec((k, th), lambda i, j: (0, j))
    return pl.pallas_call(
        _swiglu_up_body, grid=(m // tm, f // th),
        in_specs=[pl.BlockSpec((tm, k), lambda i, j: (i, 0)), w_spec, w_spec],
        out_specs=pl.BlockSpec((tm, th), lambda i, j: (i, j)),
        out_shape=jax.ShapeDtypeStruct((m, f), BF16),
        compiler_params=_cparams(2), name=name)(a, wg, wu)


def _ep_group_norm(acc, aux, outs, *, group):
    gain = aux[0][...]
    tn = acc.shape[1]
    lane = lax.broadcasted_iota(jnp.int32, (1, LANES), 1)
    for c in range(tn // LANES):
        x = acc[:, c * LANES:(c + 1) * LANES]
        g = gain[:, c * LANES:(c + 1) * LANES]
        if group == LANES:
            r = lax.rsqrt(_sumsq_lanes(x) * (1.0 / LANES) + EPS)
        else:
            lo = lane < group
            x2 = x * x
            ss_lo = jnp.sum(jnp.where(lo, x2, 0.0), axis=-1, keepdims=True)
            ss_hi = jnp.sum(jnp.where(lo, 0.0, x2), axis=-1, keepdims=True)
            r = jnp.where(lo, lax.rsqrt(ss_lo * (1.0 / group) + EPS),
                          lax.rsqrt(ss_hi * (1.0 / group) + EPS))
        outs[0][:, c * LANES:(c + 1) * LANES] = (x * r * g).astype(outs[0].dtype)


def _ep_transpose_heads(acc, aux, outs, *, t):
    tm, tn = acc.shape
    ones = jnp.ones((ONES_ROWS, t), outs[0].dtype)
    for hh in range(tn // LANES):
        for s in range(tm // t):
            blk = acc[s * t:(s + 1) * t, hh * LANES:(hh + 1) * LANES]
            outs[0][hh, s, :LANES, :] = blk.T.astype(outs[0].dtype)
            outs[0][hh, s, LANES:, :] = ones


def _ep_latent_norms(acc, aux, outs, *, n_q):
    gain = aux[0][...]
    n = acc.shape[1]
    xq = acc[:, :n_q]
    xk = acc[:, n_q:]
    rq = lax.rsqrt(_sumsq_lanes(xq) * (1.0 / n_q) + EPS)
    rk = lax.rsqrt(_sumsq_lanes(xk) * (1.0 / (n - n_q)) + EPS)
    outs[0][:, :n_q] = (xq * rq * gain[:, :n_q]).astype(outs[0].dtype)
    outs[0][:, n_q:] = (xk * rk * gain[:, n_q:]).astype(outs[0].dtype)


def _rope_half_block(y, cos, sin):
    lane = lax.broadcasted_iota(jnp.int32, (1, LANES), 1)
    ra = pltpu.roll(y, 32, axis=1)
    rb = pltpu.roll(y, 96, axis=1)
    rot = jnp.where(lane < 32, -rb, ra)
    return y * cos + rot * sin


def _ep_tail(acc, aux, outs, *, rope_dim, n_gate):
    gain = aux[0][...]
    bias = aux[1][...]
    cos = aux[2][...]
    sin = aux[3][...]
    shift_col = aux[4][...]
    lane = lax.broadcasted_iota(jnp.int32, (1, LANES), 1)
    is_pe = lane < rope_dim
    xpe = jnp.where(is_pe, acc, 0.0)
    r = lax.rsqrt(_sumsq_lanes(xpe) * (1.0 / rope_dim) + EPS)
    y = xpe * r * gain
    outs[0][...] = (_rope_half_block(y, cos, sin) + shift_col).astype(outs[0].dtype)
    z = acc + bias
    logsig = -(jnp.maximum(-z, 0.0) + jnp.log(1.0 + jnp.exp(-jnp.abs(z))))
    is_gate = (lane >= rope_dim) & (lane < rope_dim + n_gate)
    outs[1][...] = jnp.where(is_gate, logsig, 0.0)


def _ep_latent_q(acc, aux, outs):
    g_nope = aux[0][...]
    g_rope = aux[1][...]
    cos = aux[2][...]
    sin = aux[3][...]
    one_col = aux[4][...]
    for hh in range(acc.shape[1] // (2 * LANES)):
        xn = acc[:, hh * 256:hh * 256 + LANES]
        xr = acc[:, hh * 256 + LANES:(hh + 1) * 256]
        rn = lax.rsqrt(_sumsq_lanes(xn) * (1.0 / LANES) + EPS)
        outs[0][:, hh * 256:hh * 256 + LANES] = (xn * rn * g_nope).astype(outs[0].dtype)
        rr = lax.rsqrt(_sumsq_lanes(xr) * (1.0 / 64) + EPS)
        yr = _rope_half_block(xr * rr * g_rope, cos, sin) + one_col
        outs[0][:, hh * 256 + LANES:(hh + 1) * 256] = yr.astype(outs[0].dtype)


def _ep_latent_k(acc, aux, outs):
    g_nope = aux[0][...]
    kpe = aux[1][...]
    for hh in range(acc.shape[1] // LANES):
        xn = acc[:, hh * LANES:(hh + 1) * LANES]
        rn = lax.rsqrt(_sumsq_lanes(xn) * (1.0 / LANES) + EPS)
        outs[0][:, hh * 256:hh * 256 + LANES] = (xn * rn * g_nope).astype(outs[0].dtype)
        outs[0][:, hh * 256 + LANES:(hh + 1) * 256] = kpe


def _norm_body(x_ref, g_ref, h_ref):
    x = x_ref[...]
    d = x.shape[1]
    r = lax.rsqrt(_sumsq_lanes(x) * (1.0 / d) + EPS)
    h_ref[...] = (x * r * g_ref[...]).astype(h_ref.dtype)


def _rms_norm(x, gain, *, tm, name):
    t, d = x.shape
    return pl.pallas_call(
        _norm_body, grid=(t // tm,),
        in_specs=[pl.BlockSpec((tm, d), lambda i: (i, 0)), pl.BlockSpec((1, d), lambda i: (0, 0))],
        out_specs=pl.BlockSpec((tm, d), lambda i: (i, 0)),
        out_shape=jax.ShapeDtypeStruct((t, d), BF16),
        compiler_params=_cparams(1), name=name)(x, gain.reshape(1, d))


ROUTE_I1, ROUTE_I2, ROUTE_G1, ROUTE_G2, ROUTE_R1, ROUTE_R2 = range(6)


def _norm_router_body(x_ref, g_ref, wr_ref, h_ref, route_ref, count_ref, carry_ref, *, n_exp):
    @pl.when(pl.program_id(0) == 0)
    def _():
        carry_ref[...] = jnp.zeros_like(carry_ref)

    x = x_ref[...]
    tm, d = x.shape
    r = lax.rsqrt(_sumsq_lanes(x) * (1.0 / d) + EPS)
    h = x * r * g_ref[...]
    h_ref[...] = h.astype(h_ref.dtype)
    logits = jnp.dot(h, wr_ref[...], preferred_element_type=F32, precision=_HI)
    lane = lax.broadcasted_iota(jnp.int32, logits.shape, 1).astype(F32)
    lg = jnp.where(lane < n_exp, logits, -jnp.inf)
    m1 = jnp.max(lg, axis=-1, keepdims=True)
    i1 = jnp.min(jnp.where(lg == m1, lane, float(LANES)), axis=-1, keepdims=True)
    lg2 = jnp.where(lane == i1, -jnp.inf, lg)
    m2 = jnp.max(lg2, axis=-1, keepdims=True)
    i2 = jnp.min(jnp.where(lg2 == m2, lane, float(LANES)), axis=-1, keepdims=True)
    e2 = jnp.exp(m2 - m1)
    g1 = 1.0 / (1.0 + e2)
    g2 = e2 * g1

    chosen = jnp.where((lane == i1) | (lane == i2), 1.0, 0.0)
    rr = lax.broadcasted_iota(jnp.int32, (tm, tm), 0)
    cc = lax.broadcasted_iota(jnp.int32, (tm, tm), 1)
    strict = jnp.where(rr > cc, 1.0, 0.0).astype(BF16)
    before = jnp.dot(strict, chosen.astype(BF16), preferred_element_type=F32) + carry_ref[...]
    rank1 = jnp.sum(jnp.where(lane == i1, before, 0.0), axis=-1, keepdims=True)
    rank2 = jnp.sum(jnp.where(lane == i2, before, 0.0), axis=-1, keepdims=True)
    carry_ref[...] += jnp.sum(chosen, axis=0, keepdims=True)
    count_ref[...] = carry_ref[...]

    rec = jnp.zeros((tm, LANES), F32)
    for k, v in ((ROUTE_I1, i1), (ROUTE_I2, i2), (ROUTE_G1, g1), (ROUTE_G2, g2), (ROUTE_R1, rank1), (ROUTE_R2, rank2)):
        rec = jnp.where(lane == float(k), v, rec)
    route_ref[...] = rec


def _rms_norm_router(x, gain, router, *, tm, name):
    t, d = x.shape
    n_exp = router.shape[1]
    wr = jnp.zeros((d, LANES), F32).at[:, :n_exp].set(router.astype(F32))
    return pl.pallas_call(
        functools.partial(_norm_router_body, n_exp=n_exp), grid=(t // tm,),
        in_specs=[pl.BlockSpec((tm, d), lambda i: (i, 0)), pl.BlockSpec((1, d), lambda i: (0, 0)),
                  pl.BlockSpec((d, LANES), lambda i: (0, 0))],
        out_specs=[pl.BlockSpec((tm, d), lambda i: (i, 0)),
                   pl.BlockSpec((tm, LANES), lambda i: (i, 0)),
                   pl.BlockSpec((1, LANES), lambda i: (0, 0))],
        out_shape=[jax.ShapeDtypeStruct((t, d), BF16),
                   jax.ShapeDtypeStruct((t, LANES), F32),
                   jax.ShapeDtypeStruct((1, LANES), F32)],
        scratch_shapes=[pltpu.VMEM((1, LANES), F32)],
        compiler_params=_cparams(1), name=name)(x, gain.reshape(1, d), wr)


def _row_view(a):
    rows, d = a.shape
    return a.reshape(rows, d // LANES, LANES)


def _dispatch_body(src_ref, h_ref, o_ref, sem, *, tm):
    def issue(pair, carry):
        for prio in range(2):
            r = 2 * pair + prio
            pltpu.make_async_copy(h_ref.at[src_ref[0, 0, r]], o_ref.at[r], sem).start(priority=prio)
        return carry

    def drain(r, carry):
        pltpu.make_async_copy(h_ref.at[0], o_ref.at[0], sem).wait()
        return carry

    lax.fori_loop(0, tm // 2, issue, 0)
    lax.fori_loop(0, tm, drain, 0)


def _dispatch(h_rows, src, *, tm):
    n_tiles = src.shape[0]
    chunks = h_rows.shape[1]
    return pl.pallas_call(
        functools.partial(_dispatch_body, tm=tm), grid=(n_tiles,),
        in_specs=[pl.BlockSpec((1, 1, tm), lambda i: (i, 0, 0), memory_space=pltpu.SMEM),
                  pl.BlockSpec(memory_space=pl.ANY)],
        out_specs=pl.BlockSpec((tm, chunks, LANES), lambda i: (i, 0, 0)),
        out_shape=jax.ShapeDtypeStruct((n_tiles * tm, chunks, LANES), h_rows.dtype),
        scratch_shapes=[pltpu.SemaphoreType.DMA(())],
        compiler_params=_cparams(1), name="moe_dispatch")(src, h_rows)


def _combine_body(pos_ref, gate_ref, ys_ref, x_ref, o_ref, buf, sem, *, tm):
    def issue(r, carry):
        pltpu.make_async_copy(ys_ref.at[pos_ref[0, 0, r]], buf.at[0, r], sem).start(priority=0)
        pltpu.make_async_copy(ys_ref.at[pos_ref[0, 0, tm + r]], buf.at[1, r], sem).start(priority=1)
        return carry

    def drain(r, carry):
        pltpu.make_async_copy(ys_ref.at[0], buf.at[0, 0], sem).wait()
        pltpu.make_async_copy(ys_ref.at[0], buf.at[1, 0], sem).wait()
        return carry

    def mix(r, carry):
        o_ref[r] = (x_ref[r] + gate_ref[0, 0, r] * buf[0, r].astype(F32)
                    + gate_ref[0, 0, tm + r] * buf[1, r].astype(F32))
        return carry

    lax.fori_loop(0, tm, issue, 0)
    lax.fori_loop(0, tm, drain, 0)
    lax.fori_loop(0, tm, mix, 0)


def _combine(ys_rows, pos, gates, x_rows, *, tm):
    n_tiles = pos.shape[0]
    chunks = x_rows.shape[1]
    tile = pl.BlockSpec((tm, chunks, LANES), lambda i: (i, 0, 0))
    smem_tile = pl.BlockSpec((1, 1, 2 * tm), lambda i: (i, 0, 0), memory_space=pltpu.SMEM)
    return pl.pallas_call(
        functools.partial(_combine_body, tm=tm), grid=(n_tiles,),
        in_specs=[smem_tile, smem_tile, pl.BlockSpec(memory_space=pl.ANY), tile],
        out_specs=tile,
        out_shape=jax.ShapeDtypeStruct(x_rows.shape, F32),
        scratch_shapes=[pltpu.VMEM((2, tm, chunks, LANES), ys_rows.dtype), pltpu.SemaphoreType.DMA(())],
        compiler_params=_cparams(1), name="moe_combine")(pos, gates, ys_rows, x_rows)


def _grouped_up_body(te_ref, nv_ref, a_ref, wg_ref, wu_ref, o_ref):
    @pl.when(pl.program_id(1) < nv_ref[0])
    def _():
        a = a_ref[...]
        g = jnp.dot(a, wg_ref[...], preferred_element_type=F32)
        u = jnp.dot(a, wu_ref[...], preferred_element_type=F32)
        o_ref[...] = (g * (1.0 / (1.0 + jnp.exp(-g))) * u).astype(o_ref.dtype)

    @pl.when(pl.program_id(1) >= nv_ref[0])
    def _():
        o_ref[...] = jnp.zeros_like(o_ref)


def _grouped_down_body(te_ref, nv_ref, a_ref, w_ref, o_ref):
    @pl.when(pl.program_id(1) < nv_ref[0])
    def _():
        o_ref[...] = jnp.dot(a_ref[...], w_ref[...], preferred_element_type=F32).astype(o_ref.dtype)

    @pl.when(pl.program_id(1) >= nv_ref[0])
    def _():
        o_ref[...] = jnp.zeros_like(o_ref)


def _grouped_matmul(body, a, weights, tile_expert, n_valid, *, tmo, tn, name):
    r, k = a.shape
    n = weights[0].shape[2]

    def row(j, i, te, nv):
        return jnp.minimum(i, nv[0] - 1)

    grid_spec = pltpu.PrefetchScalarGridSpec(
        num_scalar_prefetch=2, grid=(n // tn, r // tmo),
        in_specs=[pl.BlockSpec((tmo, k), lambda j, i, te, nv: (row(j, i, te, nv), 0))]
        + [pl.BlockSpec((None, k, tn), lambda j, i, te, nv: (te[row(j, i, te, nv)], 0, j)) for _ in weights],
        out_specs=pl.BlockSpec((tmo, tn), lambda j, i, te, nv: (i, j)))
    return pl.pallas_call(
        body, grid_spec=grid_spec, out_shape=jax.ShapeDtypeStruct((r, n), BF16),
        compiler_params=_cparams(2), name=name)(tile_expert, n_valid, a, *weights)


def _rope_body(pos_ref, f_ref, cos_ref, sin_ref):
    ang = pos_ref[...] * f_ref[...]
    live = f_ref[...] > 0.0
    cos_ref[...] = jnp.where(live, jnp.cos(ang), 0.0)
    sin_ref[...] = jnp.where(live, jnp.sin(ang), 0.0)


def _rope_tables(positions, *, half, tm):
    t = positions.size
    inv_freq = ROPE_BASE ** (-jnp.arange(half, dtype=F32) / half)
    f_row = jnp.concatenate([inv_freq, inv_freq, jnp.zeros((LANES - 2 * half,), F32)]).reshape(1, LANES)
    pos_rep = jnp.broadcast_to(positions.astype(F32).reshape(t, 1), (t, LANES))
    spec = pl.BlockSpec((tm, LANES), lambda i: (i, 0))
    return pl.pallas_call(
        _rope_body, grid=(t // tm,),
        in_specs=[spec, pl.BlockSpec((1, LANES), lambda i: (0, 0))],
        out_specs=[spec, spec],
        out_shape=[jax.ShapeDtypeStruct((t, LANES), F32)] * 2,
        compiler_params=_cparams(1), name="rope_tables")(pos_rep, f_row)


def _bias_body(rb_ref, o_ref, *, t):
    h = pl.program_id(0)
    d = pl.program_id(1)
    j = lax.broadcasted_iota(jnp.int32, (t, t), 0)
    i = lax.broadcasted_iota(jnp.int32, (t, t), 1)
    rel = d * t + i - j
    n = jnp.maximum(rel, 0)
    max_exact = NUM_BUCKETS // 2
    large = max_exact + (
        jnp.log(jnp.maximum(n, max_exact).astype(F32) / max_exact)
        / math.log(MAX_DISTANCE / max_exact) * (NUM_BUCKETS - max_exact)
    ).astype(jnp.int32)
    large = jnp.minimum(large, NUM_BUCKETS - 1)
    bucket = jnp.where(n < max_exact, n, large)
    far = rb_ref[NUM_BUCKETS - 1, h]
    val = jnp.zeros((t, t), F32)
    for b in range(NUM_BUCKETS - 1):
        val = jnp.where(bucket == b, rb_ref[b, h] - far, val)
    o_ref[...] = jnp.where(rel >= 0, val * LOG2E, NEG)


def _bias_tiles(rel_bias, *, t):
    assert t + 1 >= MAX_DISTANCE, "tiles beyond the first sub-diagonal must be in the last bucket"
    nh = rel_bias.shape[1]
    return pl.pallas_call(
        functools.partial(_bias_body, t=t), grid=(nh, 2),
        in_specs=[pl.BlockSpec(memory_space=pltpu.SMEM)],
        out_specs=pl.BlockSpec((None, None, t, t), lambda h, d: (h, d, 0, 0)),
        out_shape=jax.ShapeDtypeStruct((nh, 2, t, t), F32),
        compiler_params=_cparams(2), name="t5_bias_tiles")(rel_bias.astype(F32))


N_PIECES = 3


def _split_pieces(x):
    pieces = []
    rest = x
    for _ in range(N_PIECES):
        p = rest.astype(BF16).astype(F32)
        pieces.append(p)
        rest = rest - p
    return pieces


def _cum_body(shift_ref, lf_ref, o_ref, end_ref, carry_ref, *, n_heads, lane0):
    @pl.when(pl.program_id(1) == 0)
    def _():
        carry_ref[...] = jnp.zeros_like(carry_ref)

    x = lf_ref[...]
    tb = x.shape[0]
    r = lax.broadcasted_iota(jnp.int32, (tb, tb), 0)
    c = lax.broadcasted_iota(jnp.int32, (tb, tb), 1)
    tri = jnp.where(r >= c, 1.0, 0.0).astype(F32)
    cum = jnp.dot(tri, x, preferred_element_type=F32, precision=_HI) + carry_ref[...]
    carry_ref[...] = cum[tb - 1:tb, :]
    cum2 = cum * LOG2E
    end_ref[0] = cum2[tb - 1:tb, :]
    lane = lax.broadcasted_iota(jnp.int32, (tb, LANES), 1)
    shift = shift_ref[0]
    for hh in range(n_heads):
        col = cum2[:, lane0 + hh:lane0 + hh + 1]
        qcols = jnp.where(lane < N_PIECES, 1.0, 0.0)
        kcols = jnp.where((lane >= N_PIECES) & (lane < 2 * N_PIECES), 1.0, 0.0)
        for n, (pq, pk) in enumerate(zip(_split_pieces(col - shift), _split_pieces(-col))):
            qcols = jnp.where(lane == N_PIECES + n, pq, qcols)
            kcols = jnp.where(lane == n, pk, kcols)
        o_ref[:, hh * LANES:(hh + 1) * LANES] = qcols.astype(o_ref.dtype)
        o_ref[:, (n_heads + hh) * LANES:(n_heads + hh + 1) * LANES] = kcols.astype(o_ref.dtype)


def _forget_columns(logf, shift, *, batch, seq, n_heads, lane0, tb):
    nb = seq // tb
    width = 2 * n_heads * LANES
    return pl.pallas_call(
        functools.partial(_cum_body, n_heads=n_heads, lane0=lane0), grid=(batch, nb),
        in_specs=[pl.BlockSpec(memory_space=pltpu.SMEM),
                  pl.BlockSpec((tb, LANES), lambda b, i: (b * nb + i, 0))],
        out_specs=[pl.BlockSpec((tb, width), lambda b, i: (b * nb + i, 0)),
                   pl.BlockSpec((1, 1, LANES), lambda b, i: (b * nb + i, 0, 0))],
        out_shape=[jax.ShapeDtypeStruct((batch * seq, width), BF16),
                   jax.ShapeDtypeStruct((batch * nb, 1, LANES), F32)],
        scratch_shapes=[pltpu.VMEM((1, LANES), F32)],
        compiler_params=_cparams(2), name="forget_cumsum")(shift.reshape(1).astype(F32), logf)


def _online_block(s, vb, m_ref, acc_ref):
    m_prev = m_ref[...]
    m_new = jnp.maximum(m_prev, jnp.max(s, axis=0, keepdims=True))
    alpha = jnp.exp2(m_prev - m_new)
    p = jnp.exp2(s - m_new).astype(BF16)
    acc_ref[...] = alpha * acc_ref[...] + jnp.dot(vb, p, preferred_element_type=F32)
    m_ref[...] = m_new


def _shifted_block(s, vb, acc_ref):
    p = jnp.exp2(s).astype(BF16)
    acc_ref[...] += jnp.dot(vb, p, preferred_element_type=F32)


def _causal_mask(t):
    j = lax.broadcasted_iota(jnp.int32, (t, t), 0)
    i = lax.broadcasted_iota(jnp.int32, (t, t), 1)
    return j <= i


_NT = (((1,), (1,)), ((), ()))
FAR_UNROLL = 4


def _init_state(m_ref, acc_ref):
    m_ref[...] = jnp.full(m_ref.shape, NEG, F32)
    acc_ref[...] = jnp.zeros(acc_ref.shape, F32)


def _attn_body(*refs, mode, t, hp, dq, dv, qk_dim=None, lambda_init=None):
    safe_ref, start_ref, q_ref, k_ref, vt_ref = refs[:5]
    if mode == "latent":
        o_ref, qc_ref, m_ref, acc_ref = refs[5:]
    elif mode == "forget":
        qx_ref, kx_ref, o_ref, qc_ref, m_ref, acc_ref = refs[5:]
    else:
        qx_ref, kx_ref, bias_ref, lam_ref, hn_ref, o_ref, qc_ref, m_ref, acc_ref = refs[5:]
    n_map = 2 if mode == "diff" else 1
    qi = pl.program_id(2)
    step = (pl.program_id(0) * pl.num_programs(1) + pl.program_id(1)) * pl.num_programs(2) + qi
    _init_state(m_ref, acc_ref)

    for hh in range(hp):
        q = q_ref[:, hh * dq:(hh + 1) * dq]
        if mode == "latent":
            qc_ref[hh] = q
        elif mode == "forget":
            qc_ref[hh] = jnp.concatenate([q, qx_ref[:, hh * LANES:(hh + 1) * LANES]], axis=1)
        else:
            lane = lax.broadcasted_iota(jnp.int32, (t, dq), 1)
            zero = jnp.zeros_like(q)
            qc_ref[2 * hh] = jnp.concatenate([jnp.where(lane < qk_dim, q, zero), qx_ref[...]], axis=1)
            qc_ref[2 * hh + 1] = jnp.concatenate([jnp.where(lane < qk_dim, zero, q), qx_ref[...]], axis=1)

    def keys(kj, hh):
        start = pl.multiple_of(kj * t, t)
        kb = k_ref[pl.ds(start, t), hh * dq:(hh + 1) * dq]
        if mode == "forget":
            kb = jnp.concatenate([kb, kx_ref[pl.ds(start, t), hh * LANES:(hh + 1) * LANES]], axis=1)
        elif mode == "diff":
            kb = jnp.concatenate([kb, kx_ref[hh]], axis=1)
        return kb

    def run(update, skip_dead):
        def scores(kj, kind, hh, mm):
            s = lax.dot_general(keys(kj, hh), qc_ref[n_map * hh + mm], _NT, preferred_element_type=F32)
            if mode == "diff":
                if kind != "far":
                    s = s + bias_ref[hh, 0 if kind == "diag" else 1]
            elif kind == "diag":
                s = jnp.where(_causal_mask(t), s, NEG)
            return s

        def blocks(items):
            work = [(kj, kind, hh, mm) for kj, kind in items for hh in range(hp) for mm in range(n_map)]
            s_next = scores(*work[0])
            for n, (kj, kind, hh, mm) in enumerate(work):
                s_cur = s_next
                if n + 1 < len(work):
                    s_next = scores(*work[n + 1])
                update(s_cur, vt_ref[hh, kj], n_map * hh + mm)

        n_far = jnp.maximum(qi - 1, 0) if mode == "diff" else qi
        k0 = jnp.minimum(start_ref[step], n_far) if skip_dead else 0
        n_groups = (n_far - k0) // FAR_UNROLL

        def far_group(g, carry):
            blocks([(k0 + g * FAR_UNROLL + u, "far") for u in range(FAR_UNROLL)])
            return carry

        def far_single(kj, carry):
            blocks([(kj, "far")])
            return carry

        lax.fori_loop(0, n_groups, far_group, 0)
        lax.fori_loop(k0 + n_groups * FAR_UNROLL, n_far, far_single, 0)

        if mode == "diff":
            @pl.when(qi >= 1)
            def _():
                blocks([(qi - 1, "sub")])
        blocks([(qi, "diag")])

    @pl.when(safe_ref[0] != 0)
    def _():
        run(lambda s, vb, c: _shifted_block(s, vb, acc_ref.at[c]), True)

    @pl.when(safe_ref[0] == 0)
    def _():
        run(lambda s, vb, c: _online_block(s, vb, m_ref.at[c], acc_ref.at[c]), False)

    if mode == "diff":
        lp = lam_ref[...]
        lam = (jnp.exp(jnp.sum(lp[0:1] * lp[1:2], axis=-1, keepdims=True))
               - jnp.exp(jnp.sum(lp[2:3] * lp[3:4], axis=-1, keepdims=True)) + lambda_init)
    for hh in range(hp):
        if mode == "diff":
            a0 = acc_ref[2 * hh]
            a1 = acc_ref[2 * hh + 1]
            o = a0[:dv] * (1.0 / a0[dv:dv + 1]) - lam * (a1[:dv] * (1.0 / a1[dv:dv + 1]))
            r = lax.rsqrt(jnp.sum(o * o, axis=0, keepdims=True) * (1.0 / dv) + EPS)
            o = o * r * hn_ref[...] * (1.0 - lambda_init)
        else:
            a = acc_ref[hh]
            o = a[:dv] * (1.0 / a[dv:dv + 1])
        o_ref[:, hh * dv:(hh + 1) * dv] = o.T.astype(o_ref.dtype)


def _attention(mode, safe, q_arr, k_arr, vt_arr, *, batch, seq, n_heads, t, hp, dq, q_col0, k_col0,
               extra=(), extra_specs=(), body_kwargs=None, first_block=None, name):
    nq = seq // t
    dvp = vt_arr.shape[2]
    dv = dvp - ONES_ROWS
    tokens = batch * seq
    assert n_heads % hp == 0 and q_col0 % hp == 0 and k_col0 % hp == 0
    qc, kc = q_col0 // hp, k_col0 // hp
    if first_block is None:
        first_block = jnp.zeros((batch * (n_heads // hp) * nq,), jnp.int32)
    in_specs = [
        pl.BlockSpec(memory_space=pltpu.SMEM),
        pl.BlockSpec(memory_space=pltpu.SMEM),
        pl.BlockSpec((t, hp * dq), lambda b, h, i: (b * nq + i, qc + h)),
        pl.BlockSpec((seq, hp * dq), lambda b, h, i: (b, kc + h)),
        pl.BlockSpec((hp, nq, dvp, t), lambda b, h, i: (h, b, 0, 0)),
        *extra_specs,
    ]
    n_chain = 2 * hp if mode == "diff" else hp
    scratch = [pltpu.VMEM((n_chain, t, 2 * LANES), BF16),
               pltpu.VMEM((n_chain, 1, t), F32), pltpu.VMEM((n_chain, dvp, t), F32)]
    body = functools.partial(_attn_body, mode=mode, t=t, hp=hp, dq=dq, dv=dv, **(body_kwargs or {}))
    return pl.pallas_call(
        body, grid=(batch, n_heads // hp, nq), in_specs=in_specs,
        out_specs=pl.BlockSpec((t, hp * dv), lambda b, h, i: (b * nq + i, h)),
        out_shape=jax.ShapeDtypeStruct((tokens, n_heads * dv), BF16),
        scratch_shapes=scratch, compiler_params=_cparams(3), name=name)(
            safe, first_block, q_arr, k_arr, vt_arr, *extra)


def _pick(n, prefs):
    for p in prefs:
        if n % p == 0:
            return p
    raise ValueError((n, prefs))


SKIP_LOG2 = 160.0
SAFE_DEPTH = 110.0


def _round_up_bf16(c):
    return (c * (1.0 + 2.0 ** -6)).astype(BF16).astype(F32)


def _safe_flag(depth):
    return (depth <= SAFE_DEPTH).astype(jnp.int32).reshape(1)


def kernel(x, positions, attn_norm, w_in, b_forget, a_qk_norm, a_lambda, a_head_norm, rel_bias,
           b_qk_norm, c_q_a_norm, c_kv_a_norm, c_w_uq, c_w_ukv, c_qk_norm_nope, c_qk_norm_rope,
           w_out, ffn_norm, dense_w_gate, dense_w_up, dense_w_down, moe_router, moe_w_gate,
           moe_w_up, moe_w_down):
    batch, seq, d_model = x.shape
    depth = w_in.shape[0]
    tokens = batch * seq
    n_slots = d_model // LANES
    a_heads = n_slots // 4
    b_heads = n_slots // 4
    c_heads = n_slots // 2
    a_qk = a_qk_norm.shape[-1]
    b_dim = b_qk_norm.shape[-1]
    c_q_lora = c_q_a_norm.shape[-1]
    c_kv_lora = c_kv_a_norm.shape[-1]
    c_nope = c_qk_norm_nope.shape[-1]
    c_rope = c_qk_norm_rope.shape[-1]
    c_v = c_w_ukv.shape[-1] // c_heads - c_nope
    n_exp = moe_router.shape[-1]
    assert 2 * a_qk == LANES and b_dim == LANES and c_nope == LANES and c_v == LANES and 2 * c_rope == LANES

    t = _pick(seq, (512, 256, 128))
    tm = _pick(tokens, (1024, 512, 256))
    tm_n = _pick(tokens, (256, 128))
    hp_a, hp_b, hp_c = 1, 2, 2

    sizes = (a_heads * 2 * a_qk, a_heads * 2 * a_qk, a_heads * LANES, b_heads * b_dim, b_heads * b_dim,
             b_heads * b_dim, b_heads, c_q_lora, c_kv_lora, c_rope)
    offs = [0]
    for s in sizes:
        offs.append(offs[-1] + s)
    o_aq, o_ak, o_av, o_bq, o_bk, o_bv, o_f, o_cq, o_ckv, o_kpe, o_end = offs

    xf = x.reshape(tokens, d_model)
    cos_t, sin_t = _rope_tables(positions, half=c_rope // 2, tm=tm)
    bias_tiles = _bias_tiles(rel_bias, t=t)
    bias_log2 = (rel_bias.astype(F32) - rel_bias[-1:].astype(F32)) * LOG2E
    bias_max = jnp.max(bias_log2, axis=0)
    qx_a = jnp.zeros((t, LANES), F32).at[:, 0].set(1.0).astype(BF16)

    row_spec = lambda width: pl.BlockSpec((1, width), lambda i, j: (0, j))
    fixed_row = pl.BlockSpec((1, LANES), lambda i, j: (0, 0))
    tok_lane = pl.BlockSpec((tm, LANES), lambda i, j: (i, 0))

    def vt_call(h_bf, w, nh, name, a_cols=None, w_cols=None):
        tn = _pick(nh * LANES, (512, 256, 128))
        return _matmul(
            h_bf, w, a_cols=a_cols, w_cols=w_cols, tm=tm, tn=tn,
            epilogue=functools.partial(_ep_transpose_heads, t=t),
            out_shape=jax.ShapeDtypeStruct((nh, tokens // t, LANES + ONES_ROWS, t), BF16),
            out_specs=pl.BlockSpec((tn // LANES, tm // t, LANES + ONES_ROWS, t), lambda i, j: (j, i, 0, 0)),
            name=name)

    for layer in range(depth):
        lambda_init = 0.8 - 0.6 * math.exp(-0.3 * layer)
        w = w_in[layer].astype(BF16)

        h = _rms_norm(xf, attn_norm[layer], tm=tm_n, name="attn_norm")

        a_scale = a_qk ** -0.5 * LOG2E
        gain_a = jnp.concatenate([jnp.tile(a_qk_norm[layer, 0] * a_scale, 2 * a_heads),
                                  jnp.tile(a_qk_norm[layer, 1], 2 * a_heads)]).reshape(1, -1)
        tn = 512
        qk_a = _matmul(h, w, w_cols=(o_aq, o_av - o_aq), tm=tm, tn=tn,
                       epilogue=functools.partial(_ep_group_norm, group=a_qk),
                       aux=(gain_a,), aux_specs=(row_spec(tn),),
                       out_shape=jax.ShapeDtypeStruct((tokens, o_av - o_aq), BF16),
                       out_specs=pl.BlockSpec((tm, tn), lambda i, j: (i, j)), name="proj_a_qk")
        vt_a = vt_call(h, w, a_heads, "proj_a_v", w_cols=(o_av, o_bq - o_av))
        hn = jnp.broadcast_to(a_head_norm[layer].reshape(LANES, 1), (LANES, t))
        bound_a = a_qk * jnp.max(jnp.abs(a_qk_norm[layer, 0] * a_scale)) * jnp.max(jnp.abs(a_qk_norm[layer, 1]))
        shift_a = _round_up_bf16(bound_a + bias_max)
        safe_a = _safe_flag(2.0 * bound_a + jnp.max(bias_max - bias_log2[0]))
        kx_a = jnp.zeros((a_heads, t, LANES), F32).at[:, :, 0].set(-shift_a[:, None]).astype(BF16)
        o_a = _attention(
            "diff", safe_a, qk_a, qk_a, vt_a, batch=batch, seq=seq, n_heads=a_heads, t=t, hp=hp_a, dq=LANES,
            q_col0=0, k_col0=a_heads,
            extra=(qx_a, kx_a, bias_tiles, a_lambda[layer], hn),
            extra_specs=(pl.BlockSpec((t, LANES), lambda b, hh, i: (0, 0)),
                         pl.BlockSpec((hp_a, t, LANES), lambda b, hh, i: (hh, 0, 0)),
                         pl.BlockSpec((hp_a, 2, t, t), lambda b, hh, i: (hh, 0, 0, 0)),
                         pl.BlockSpec((4, a_qk), lambda b, hh, i: (0, 0)),
                         pl.BlockSpec((LANES, t), lambda b, hh, i: (0, 0))),
            body_kwargs=dict(qk_dim=a_qk, lambda_init=lambda_init), name="attn_diff")

        b_scale = b_dim ** -0.5 * LOG2E
        gain_b = jnp.concatenate([jnp.tile(b_qk_norm[layer, 0] * b_scale, b_heads),
                                  jnp.tile(b_qk_norm[layer, 1], b_heads)]).reshape(1, -1)
        qk_b = _matmul(h, w, w_cols=(o_bq, o_bv - o_bq), tm=tm, tn=tn,
                       epilogue=functools.partial(_ep_group_norm, group=b_dim),
                       aux=(gain_b,), aux_specs=(row_spec(tn),),
                       out_shape=jax.ShapeDtypeStruct((tokens, o_bv - o_bq), BF16),
                       out_specs=pl.BlockSpec((tm, tn), lambda i, j: (i, j)), name="proj_b_qk")
        vt_b = vt_call(h, w, b_heads, "proj_b_v", w_cols=(o_bv, o_f - o_bv))

        w_tail = jnp.concatenate([w[:, o_kpe:o_end], w[:, o_f:o_cq],
                                  jnp.zeros((d_model, LANES - c_rope - b_heads), BF16)], axis=1)
        gain_tail = jnp.concatenate([c_qk_norm_rope[layer, 1], jnp.zeros((LANES - c_rope,), F32)]).reshape(1, LANES)
        bias_tail = jnp.zeros((LANES,), F32).at[c_rope:c_rope + b_heads].set(b_forget[layer]).reshape(1, LANES)
        c_scale = (c_nope + c_rope) ** -0.5 * LOG2E
        gq_n = c_qk_norm_nope[layer, 0] * c_scale
        gq_r = c_qk_norm_rope[layer, 0] * c_scale
        gk_n = c_qk_norm_nope[layer, 1]
        gk_r = c_qk_norm_rope[layer, 1]
        bound_c = (jnp.sqrt(c_nope * jnp.max(gq_n * gq_n) + c_rope * jnp.max(gq_r * gq_r))
                   * jnp.sqrt(c_nope * jnp.max(gk_n * gk_n) + c_rope * jnp.max(gk_r * gk_r)))
        shift_c = _round_up_bf16(bound_c)
        safe_c = _safe_flag(2.0 * bound_c)
        shift_col = jnp.zeros((1, LANES), F32).at[0, c_rope].set(-shift_c)
        one_col = jnp.zeros((1, LANES), F32).at[0, c_rope].set(1.0)
        kpe, logf = _matmul(
            h, w_tail, tm=tm, tn=LANES,
            epilogue=functools.partial(_ep_tail, rope_dim=c_rope, n_gate=b_heads),
            aux=(gain_tail, bias_tail, cos_t, sin_t, shift_col),
            aux_specs=(fixed_row, fixed_row, tok_lane, tok_lane, fixed_row),
            out_shape=[jax.ShapeDtypeStruct((tokens, LANES), BF16), jax.ShapeDtypeStruct((tokens, LANES), F32)],
            out_specs=[tok_lane, tok_lane], name="proj_tail")
        bound_b = b_dim * jnp.max(jnp.abs(b_qk_norm[layer, 0] * b_scale)) * jnp.max(jnp.abs(b_qk_norm[layer, 1]))
        safe_b = _safe_flag(2.0 * bound_b)
        x_b, ends = _forget_columns(logf, bound_b, batch=batch, seq=seq, n_heads=b_heads, lane0=c_rope, tb=t)
        nq = seq // t
        nbh = b_heads // hp_b
        e_b = ends[:, 0, c_rope:c_rope + b_heads].reshape(batch, nq, b_heads)
        e_q = jnp.concatenate([jnp.zeros((batch, 1, b_heads), F32), e_b[:, :-1]], axis=1)
        dead = (e_q[:, :, None, :] - e_b[:, None, :, :]) < -SKIP_LOG2
        dead = dead & (jnp.arange(nq)[None, None, :, None] + 1 < jnp.arange(nq)[None, :, None, None])
        first_b = jnp.sum(dead.astype(jnp.int32), axis=2)
        first_b = jnp.min(first_b.reshape(batch, nq, nbh, hp_b), axis=3)
        first_b = jnp.transpose(first_b, (0, 2, 1)).reshape(-1)
        o_b = _attention(
            "forget", safe_b, qk_b, qk_b, vt_b, batch=batch, seq=seq, n_heads=b_heads, t=t, hp=hp_b, dq=LANES,
            q_col0=0, k_col0=b_heads,
            extra=(x_b, x_b),
            extra_specs=(pl.BlockSpec((t, hp_b * LANES), lambda b, hh, i: (b * nq + i, hh)),
                         pl.BlockSpec((seq, hp_b * LANES), lambda b, hh, i: (b, nbh + hh))),
            first_block=first_b, name="attn_forget")

        n_lat = c_q_lora + c_kv_lora
        gain_lat = jnp.concatenate([c_q_a_norm[layer], c_kv_a_norm[layer]]).reshape(1, n_lat)
        tm_lat = _pick(tokens, (512, 256))
        lat = _matmul(h, w[:, o_cq:o_kpe], tm=tm_lat, tn=n_lat,
                      epilogue=functools.partial(_ep_latent_norms, n_q=c_q_lora),
                      aux=(gain_lat,), aux_specs=(row_spec(n_lat),),
                      out_shape=jax.ShapeDtypeStruct((tokens, n_lat), BF16),
                      out_specs=pl.BlockSpec((tm_lat, n_lat), lambda i, j: (i, j)), name="proj_c_latent")
        wq = c_w_uq[layer].astype(BF16).reshape(c_q_lora, c_heads, c_nope + c_rope)
        wq = jnp.concatenate([wq, jnp.zeros((c_q_lora, c_heads, 2 * LANES - c_nope - c_rope), BF16)], axis=2)
        wq = wq.reshape(c_q_lora, c_heads * 2 * LANES)
        gq_nope = gq_n.reshape(1, LANES)
        gq_rope = jnp.concatenate([gq_r, jnp.zeros((LANES - c_rope,), F32)]).reshape(1, LANES)
        tn_q = 512
        lat_q_cols, lat_kv_cols = (0, c_q_lora), (c_q_lora, c_kv_lora)
        q_c = _matmul(lat, wq, a_cols=lat_q_cols, tm=tm, tn=tn_q, epilogue=_ep_latent_q,
                      aux=(gq_nope, gq_rope, cos_t, sin_t, one_col),
                      aux_specs=(fixed_row, fixed_row, tok_lane, tok_lane, fixed_row),
                      out_shape=jax.ShapeDtypeStruct((tokens, c_heads * 2 * LANES), BF16),
                      out_specs=pl.BlockSpec((tm, tn_q), lambda i, j: (i, j)), name="proj_c_q")
        wkv = c_w_ukv[layer].astype(BF16).reshape(c_kv_lora, c_heads, c_nope + c_v)
        wk = wkv[:, :, :c_nope].reshape(c_kv_lora, c_heads * c_nope)
        wv = wkv[:, :, c_nope:].reshape(c_kv_lora, c_heads * c_v)
        gk_nope = c_qk_norm_nope[layer, 1].reshape(1, LANES)
        tn_k = 256
        k_c = _matmul(lat, wk, a_cols=lat_kv_cols, tm=tm, tn=tn_k, epilogue=_ep_latent_k,
                      aux=(gk_nope, kpe), aux_specs=(fixed_row, tok_lane),
                      out_shape=jax.ShapeDtypeStruct((tokens, c_heads * 2 * LANES), BF16),
                      out_specs=pl.BlockSpec((tm, 2 * tn_k), lambda i, j: (i, j)), name="proj_c_k")
        vt_c = vt_call(lat, wv, c_heads, "proj_c_v", a_cols=lat_kv_cols)
        o_c = _attention(
            "latent", safe_c, q_c, k_c, vt_c, batch=batch, seq=seq, n_heads=c_heads, t=t, hp=hp_c, dq=2 * LANES,
            q_col0=0, k_col0=0, name="attn_latent")

        tn_o = 512
        xf = _out_proj((o_a, o_b, o_c), w_out[layer].astype(BF16), xf, tm=tm, tn=tn_o)

        idx = layer // 2
        th = 256
        if layer % 2 == 0:
            h2 = _rms_norm(xf, ffn_norm[layer], tm=tm_n, name="ffn_norm")
            act = _swiglu_up(h2, dense_w_gate[idx].astype(BF16), dense_w_up[idx].astype(BF16),
                             tm=tm, th=th, name="ffn_up")
            tm_d = _pick(tokens, (512, 256))
            res_d = pl.BlockSpec((tm_d, tn_o), lambda i, j: (i, j))
            xf = _matmul(act, dense_w_down[idx].astype(BF16), tm=tm_d, tn=tn_o, epilogue=_ep_residual,
                         aux=(xf,), aux_specs=(res_d,),
                         out_shape=jax.ShapeDtypeStruct((tokens, d_model), F32), out_specs=res_d,
                         name="ffn_down")
        else:
            h2, route, counts = _rms_norm_router(xf, ffn_norm[layer], moe_router[idx], tm=tm_n,
                                                 name="ffn_norm_router")
            tmo = _pick(tokens, (512, 256, 128))
            n_rows = TOP_K * tokens + n_exp * tmo
            cnt = counts[0, :n_exp].astype(jnp.int32)
            padded = (cnt + tmo - 1) // tmo * tmo
            ends = jnp.cumsum(padded)
            starts = ends - padded
            n_valid = (ends[-1] // tmo).reshape(1).astype(jnp.int32)
            tile_expert = jnp.minimum(
                jnp.searchsorted(ends, jnp.arange(n_rows // tmo, dtype=jnp.int32) * tmo, side="right"),
                n_exp - 1).astype(jnp.int32)
            e1 = route[:, ROUTE_I1].astype(jnp.int32)
            e2 = route[:, ROUTE_I2].astype(jnp.int32)
            pos1 = starts[e1] + route[:, ROUTE_R1].astype(jnp.int32)
            pos2 = starts[e2] + route[:, ROUTE_R2].astype(jnp.int32)
            tm_r = _pick(tokens, (256, 128))
            per_tile = lambda a, b: jnp.concatenate(
                [a.reshape(tokens // tm_r, 1, tm_r), b.reshape(tokens // tm_r, 1, tm_r)], axis=2)
            pos = per_tile(pos1, pos2)
            gates = per_tile(route[:, ROUTE_G1], route[:, ROUTE_G2])

            tok = jnp.arange(tokens, dtype=jnp.int32)
            src = jnp.zeros((n_rows,), jnp.int32).at[jnp.concatenate([pos1, pos2])].set(
                jnp.concatenate([tok, tok]), unique_indices=True)
            xs = _dispatch(_row_view(h2), src.reshape(n_rows // tm_r, 1, tm_r), tm=tm_r).reshape(n_rows, d_model)
            act = _grouped_matmul(_grouped_up_body, xs,
                                  (moe_w_gate[idx].astype(BF16), moe_w_up[idx].astype(BF16)),
                                  tile_expert, n_valid, tmo=tmo, tn=512, name="moe_up")
            ys = _grouped_matmul(_grouped_down_body, act, (moe_w_down[idx].astype(BF16),),
                                 tile_expert, n_valid, tmo=tmo, tn=1024, name="moe_down")
            xf = _combine(_row_view(ys), pos, gates, _row_view(xf), tm=tm_r).reshape(tokens, d_model)

    return xf.reshape(batch, seq, d_model)
```

```python
import functools
import math

import jax
import jax.numpy as jnp
from jax import lax
from jax.experimental import pallas as pl
from jax.experimental.pallas import tpu as pltpu

F32 = jnp.float32
BF16 = jnp.bfloat16

LANES = 128
ONES_ROWS = 16
LOG2E = 1.4426950408889634
NEG = -1e30
EPS = 1e-6
VMEM_LIMIT = 52 * 1024 * 1024

NUM_BUCKETS = 32
MAX_DISTANCE = 128
ROPE_BASE = 10000.0
TOP_K = 2

_HI = lax.Precision.HIGHEST


def _cparams(n_axes):
    return pltpu.CompilerParams(
        dimension_semantics=("arbitrary",) * n_axes, vmem_limit_bytes=VMEM_LIMIT)


def _mm_body(a_ref, w_ref, *rest, epilogue, n_aux):
    acc = jnp.dot(a_ref[...], w_ref[...], preferred_element_type=F32)
    epilogue(acc, rest[:n_aux], rest[n_aux:])


def _matmul(a, w, *, tm, tn, epilogue, out_shape, out_specs, aux=(), aux_specs=(), name,
            a_cols=None, w_cols=None):
    m = a.shape[0]
    a0, k = a_cols if a_cols is not None else (0, a.shape[1])
    w0, n = w_cols if w_cols is not None else (0, w.shape[1])
    if a0 % k:
        a, a0 = a[:, a0:a0 + k], 0
    if w0 % tn:
        w, w0 = w[:, w0:w0 + n], 0
    assert k == w.shape[0]
    assert m % tm == 0 and n % tn == 0, (m, n, tm, tn)
    ab, wb = a0 // k, w0 // tn
    return pl.pallas_call(
        functools.partial(_mm_body, epilogue=epilogue, n_aux=len(aux)),
        grid=(m // tm, n // tn),
        in_specs=[pl.BlockSpec((tm, k), lambda i, j: (i, ab)),
                  pl.BlockSpec((k, tn), lambda i, j: (0, wb + j)), *aux_specs],
        out_specs=out_specs, out_shape=out_shape,
        compiler_params=_cparams(2), name=name)(a, w, *aux)


def _sumsq_lanes(x):
    return jnp.sum(x * x, axis=-1, keepdims=True)


def _group_ones(group):
    r = lax.broadcasted_iota(jnp.int32, (LANES, LANES), 0)
    c = lax.broadcasted_iota(jnp.int32, (LANES, LANES), 1)
    same = (r < group) == (c < group) if group < LANES else (r >= 0)
    return jnp.where(same, 1.0, 0.0).astype(BF16)


def _group_sumsq(x, ones):
    return jnp.dot((x * x).astype(BF16), ones, preferred_element_type=F32)


def _ep_residual(acc, aux, outs):
    outs[0][...] = aux[0][...] + acc


def _out_proj_body(*refs, widths):
    part_refs = refs[:len(widths)]
    w_ref, x_ref, o_ref = refs[len(widths):]
    acc = x_ref[...]
    row = 0
    for p_ref, width in zip(part_refs, widths):
        acc = acc + jnp.dot(p_ref[...], w_ref[row:row + width, :], preferred_element_type=F32)
        row += width
    o_ref[...] = acc


def _out_proj(parts, w, x, *, tm, tn):
    m, n = x.shape
    widths = tuple(p.shape[1] for p in parts)
    res = pl.BlockSpec((tm, tn), lambda i, j: (i, j))
    return pl.pallas_call(
        functools.partial(_out_proj_body, widths=widths), grid=(m // tm, n // tn),
        in_specs=[pl.BlockSpec((tm, wd), lambda i, j: (i, 0)) for wd in widths]
        + [pl.BlockSpec((w.shape[0], tn), lambda i, j: (0, j)), res],
        out_specs=res, out_shape=jax.ShapeDtypeStruct((m, n), F32),
        compiler_params=_cparams(2), name="out_proj")(*parts, w, x)


def _swiglu_up_body(a_ref, wg_ref, wu_ref, o_ref):
    a = a_ref[...]
    g = jnp.dot(a, wg_ref[...], preferred_element_type=F32)
    u = jnp.dot(a, wu_ref[...], preferred_element_type=F32)
    o_ref[...] = (g * (1.0 / (1.0 + jnp.exp(-g))) * u).astype(o_ref.dtype)


def _swiglu_up(a, wg, wu, *, tm, th, name):
    m, k = a.shape
    f = wg.shape[1]
    w_spec = pl.BlockSpec((k, th), lambda i, j: (0, j))
    return pl.pallas_call(
        _swiglu_up_body, grid=(m // tm, f // th),
        in_specs=[pl.BlockSpec((tm, k), lambda i, j: (i, 0)), w_spec, w_spec],
        out_specs=pl.BlockSpec((tm, th), lambda i, j: (i, j)),
        out_shape=jax.ShapeDtypeStruct((m, f), BF16),
        compiler_params=_cparams(2), name=name)(a, wg, wu)


def _ep_group_norm(acc, aux, outs, *, group):
    gain = aux[0][...]
    tn = acc.shape[1]
    ones = _group_ones(group)
    for c in range(tn // LANES):
        x = acc[:, c * LANES:(c + 1) * LANES]
        g = gain[:, c * LANES:(c + 1) * LANES]
        r = lax.rsqrt(_group_sumsq(x, ones) * (1.0 / group) + EPS)
        outs[0][:, c * LANES:(c + 1) * LANES] = (x * r * g).astype(outs[0].dtype)


def _ep_transpose_heads(acc, aux, outs, *, t):
    tm, tn = acc.shape
    ones = jnp.ones((ONES_ROWS, t), outs[0].dtype)
    for hh in range(tn // LANES):
        for s in range(tm // t):
            blk = acc[s * t:(s + 1) * t, hh * LANES:(hh + 1) * LANES]
            outs[0][hh, s, :LANES, :] = blk.T.astype(outs[0].dtype)
            outs[0][hh, s, LANES:, :] = ones


def _ep_latent_norms(acc, aux, outs, *, n_q):
    gain = aux[0][...]
    n = acc.shape[1]
    xq = acc[:, :n_q]
    xk = acc[:, n_q:]
    rq = lax.rsqrt(_sumsq_lanes(xq) * (1.0 / n_q) + EPS)
    rk = lax.rsqrt(_sumsq_lanes(xk) * (1.0 / (n - n_q)) + EPS)
    outs[0][:, :n_q] = (xq * rq * gain[:, :n_q]).astype(outs[0].dtype)
    outs[0][:, n_q:] = (xk * rk * gain[:, n_q:]).astype(outs[0].dtype)


def _rotate_half_matrix():
    src = lax.broadcasted_iota(jnp.int32, (LANES, LANES), 0)
    dst = lax.broadcasted_iota(jnp.int32, (LANES, LANES), 1)
    neg = (dst < 32) & (src == dst + 32)
    pos = (dst >= 32) & (dst < 64) & (src == dst - 32)
    return jnp.where(neg, -1.0, jnp.where(pos, 1.0, 0.0)).astype(BF16)


def _rope_half_block(y, cos, sin):
    lane = lax.broadcasted_iota(jnp.int32, (1, LANES), 1)
    ra = pltpu.roll(y, 32, axis=1)
    rb = pltpu.roll(y, 96, axis=1)
    rot = jnp.where(lane < 32, -rb, ra)
    return y * cos + rot * sin


def _ep_tail(acc, aux, outs, *, rope_dim, n_gate):
    gain = aux[0][...]
    bias = aux[1][...]
    cos = aux[2][...]
    sin = aux[3][...]
    shift_col = aux[4][...]
    lane = lax.broadcasted_iota(jnp.int32, (1, LANES), 1)
    is_pe = lane < rope_dim
    xpe = jnp.where(is_pe, acc, 0.0)
    r = lax.rsqrt(_sumsq_lanes(xpe) * (1.0 / rope_dim) + EPS)
    y = xpe * r * gain
    outs[0][...] = (_rope_half_block(y, cos, sin) + shift_col).astype(outs[0].dtype)
    z = acc + bias
    logsig = -(jnp.maximum(-z, 0.0) + jnp.log(1.0 + jnp.exp(-jnp.abs(z))))
    is_gate = (lane >= rope_dim) & (lane < rope_dim + n_gate)
    outs[1][...] = jnp.where(is_gate, logsig, 0.0)


def _ep_latent_q(acc, aux, outs):
    g_nope = aux[0][...]
    g_rope = aux[1][...]
    cos = aux[2][...]
    sin = aux[3][...]
    one_col = aux[4][...]
    ones = _group_ones(LANES)
    rot_mat = _rotate_half_matrix()
    for hh in range(acc.shape[1] // (2 * LANES)):
        xn = acc[:, hh * 256:hh * 256 + LANES]
        xr = acc[:, hh * 256 + LANES:(hh + 1) * 256]
        rn = lax.rsqrt(_group_sumsq(xn, ones) * (1.0 / LANES) + EPS)
        outs[0][:, hh * 256:hh * 256 + LANES] = (xn * rn * g_nope).astype(outs[0].dtype)
        rr = lax.rsqrt(_group_sumsq(xr, ones) * (1.0 / 64) + EPS)
        y = xr * rr * g_rope
        rot = jnp.dot(y.astype(BF16), rot_mat, preferred_element_type=F32)
        outs[0][:, hh * 256 + LANES:(hh + 1) * 256] = (y * cos + rot * sin + one_col).astype(outs[0].dtype)


def _ep_latent_k(acc, aux, outs):
    g_nope = aux[0][...]
    kpe = aux[1][...]
    ones = _group_ones(LANES)
    for hh in range(acc.shape[1] // LANES):
        xn = acc[:, hh * LANES:(hh + 1) * LANES]
        rn = lax.rsqrt(_group_sumsq(xn, ones) * (1.0 / LANES) + EPS)
        outs[0][:, hh * 256:hh * 256 + LANES] = (xn * rn * g_nope).astype(outs[0].dtype)
        outs[0][:, hh * 256 + LANES:(hh + 1) * 256] = kpe


def _norm_body(x_ref, g_ref, h_ref):
    x = x_ref[...]
    d = x.shape[1]
    r = lax.rsqrt(_sumsq_lanes(x) * (1.0 / d) + EPS)
    h_ref[...] = (x * r * g_ref[...]).astype(h_ref.dtype)


def _rms_norm(x, gain, *, tm, name):
    t, d = x.shape
    return pl.pallas_call(
        _norm_body, grid=(t // tm,),
        in_specs=[pl.BlockSpec((tm, d), lambda i: (i, 0)), pl.BlockSpec((1, d), lambda i: (0, 0))],
        out_specs=pl.BlockSpec((tm, d), lambda i: (i, 0)),
        out_shape=jax.ShapeDtypeStruct((t, d), BF16),
        compiler_params=_cparams(1), name=name)(x, gain.reshape(1, d))


ROUTE_I1, ROUTE_I2, ROUTE_G1, ROUTE_G2, ROUTE_R1, ROUTE_R2 = range(6)


def _norm_router_body(x_ref, g_ref, wr_ref, h_ref, route_ref, count_ref, carry_ref, *, n_exp):
    @pl.when(pl.program_id(0) == 0)
    def _():
        carry_ref[...] = jnp.zeros_like(carry_ref)

    x = x_ref[...]
    tm, d = x.shape
    r = lax.rsqrt(_sumsq_lanes(x) * (1.0 / d) + EPS)
    h = x * r * g_ref[...]
    h_ref[...] = h.astype(h_ref.dtype)
    logits = jnp.dot(h, wr_ref[...], preferred_element_type=F32, precision=_HI)
    lane = lax.broadcasted_iota(jnp.int32, logits.shape, 1).astype(F32)
    lg = jnp.where(lane < n_exp, logits, -jnp.inf)
    m1 = jnp.max(lg, axis=-1, keepdims=True)
    i1 = jnp.min(jnp.where(lg == m1, lane, float(LANES)), axis=-1, keepdims=True)
    lg2 = jnp.where(lane == i1, -jnp.inf, lg)
    m2 = jnp.max(lg2, axis=-1, keepdims=True)
    i2 = jnp.min(jnp.where(lg2 == m2, lane, float(LANES)), axis=-1, keepdims=True)
    e2 = jnp.exp(m2 - m1)
    g1 = 1.0 / (1.0 + e2)
    g2 = e2 * g1

    chosen = jnp.where((lane == i1) | (lane == i2), 1.0, 0.0)
    rr = lax.broadcasted_iota(jnp.int32, (tm, tm), 0)
    cc = lax.broadcasted_iota(jnp.int32, (tm, tm), 1)
    strict = jnp.where(rr > cc, 1.0, 0.0).astype(BF16)
    before = jnp.dot(strict, chosen.astype(BF16), preferred_element_type=F32) + carry_ref[...]
    rank1 = jnp.sum(jnp.where(lane == i1, before, 0.0), axis=-1, keepdims=True)
    rank2 = jnp.sum(jnp.where(lane == i2, before, 0.0), axis=-1, keepdims=True)
    carry_ref[...] += jnp.sum(chosen, axis=0, keepdims=True)
    count_ref[...] = carry_ref[...]

    rec = jnp.zeros((tm, LANES), F32)
    for k, v in ((ROUTE_I1, i1), (ROUTE_I2, i2), (ROUTE_G1, g1), (ROUTE_G2, g2), (ROUTE_R1, rank1), (ROUTE_R2, rank2)):
        rec = jnp.where(lane == float(k), v, rec)
    route_ref[...] = rec


def _rms_norm_router(x, gain, router, *, tm, name):
    t, d = x.shape
    n_exp = router.shape[1]
    wr = jnp.zeros((d, LANES), F32).at[:, :n_exp].set(router.astype(F32))
    return pl.pallas_call(
        functools.partial(_norm_router_body, n_exp=n_exp), grid=(t // tm,),
        in_specs=[pl.BlockSpec((tm, d), lambda i: (i, 0)), pl.BlockSpec((1, d), lambda i: (0, 0)),
                  pl.BlockSpec((d, LANES), lambda i: (0, 0))],
        out_specs=[pl.BlockSpec((tm, d), lambda i: (i, 0)),
                   pl.BlockSpec((tm, LANES), lambda i: (i, 0)),
                   pl.BlockSpec((1, LANES), lambda i: (0, 0))],
        out_shape=[jax.ShapeDtypeStruct((t, d), BF16),
                   jax.ShapeDtypeStruct((t, LANES), F32),
                   jax.ShapeDtypeStruct((1, LANES), F32)],
        scratch_shapes=[pltpu.VMEM((1, LANES), F32)],
        compiler_params=_cparams(1), name=name)(x, gain.reshape(1, d), wr)


def _row_view(a):
    rows, d = a.shape
    return a.reshape(rows, d // LANES, LANES)


def _dispatch_body(src_ref, h_ref, o_ref, sem, *, tm):
    def issue(pair, carry):
        for prio in range(2):
            r = 2 * pair + prio
            pltpu.make_async_copy(h_ref.at[src_ref[0, 0, r]], o_ref.at[r], sem).start(priority=prio)
        return carry

    def drain(r, carry):
        pltpu.make_async_copy(h_ref.at[0], o_ref.at[0], sem).wait()
        return carry

    lax.fori_loop(0, tm // 2, issue, 0)
    lax.fori_loop(0, tm, drain, 0)


def _dispatch(h_rows, src, *, tm):
    n_tiles = src.shape[0]
    chunks = h_rows.shape[1]
    return pl.pallas_call(
        functools.partial(_dispatch_body, tm=tm), grid=(n_tiles,),
        in_specs=[pl.BlockSpec((1, 1, tm), lambda i: (i, 0, 0), memory_space=pltpu.SMEM),
                  pl.BlockSpec(memory_space=pl.ANY)],
        out_specs=pl.BlockSpec((tm, chunks, LANES), lambda i: (i, 0, 0)),
        out_shape=jax.ShapeDtypeStruct((n_tiles * tm, chunks, LANES), h_rows.dtype),
        scratch_shapes=[pltpu.SemaphoreType.DMA(())],
        compiler_params=_cparams(1), name="moe_dispatch")(src, h_rows)


def _combine_body(pos_ref, gate_ref, ys_ref, x_ref, o_ref, buf, sem, *, tm):
    def issue(r, carry):
        pltpu.make_async_copy(ys_ref.at[pos_ref[0, 0, r]], buf.at[0, r], sem).start(priority=0)
        pltpu.make_async_copy(ys_ref.at[pos_ref[0, 0, tm + r]], buf.at[1, r], sem).start(priority=1)
        return carry

    def drain(r, carry):
        pltpu.make_async_copy(ys_ref.at[0], buf.at[0, 0], sem).wait()
        pltpu.make_async_copy(ys_ref.at[0], buf.at[1, 0], sem).wait()
        return carry

    def mix(r, carry):
        o_ref[r] = (x_ref[r] + gate_ref[0, 0, r] * buf[0, r].astype(F32)
                    + gate_ref[0, 0, tm + r] * buf[1, r].astype(F32))
        return carry

    lax.fori_loop(0, tm, issue, 0)
    lax.fori_loop(0, tm, drain, 0)
    lax.fori_loop(0, tm, mix, 0)


def _combine(ys_rows, pos, gates, x_rows, *, tm):
    n_tiles = pos.shape[0]
    chunks = x_rows.shape[1]
    tile = pl.BlockSpec((tm, chunks, LANES), lambda i: (i, 0, 0))
    smem_tile = pl.BlockSpec((1, 1, 2 * tm), lambda i: (i, 0, 0), memory_space=pltpu.SMEM)
    return pl.pallas_call(
        functools.partial(_combine_body, tm=tm), grid=(n_tiles,),
        in_specs=[smem_tile, smem_tile, pl.BlockSpec(memory_space=pl.ANY), tile],
        out_specs=tile,
        out_shape=jax.ShapeDtypeStruct(x_rows.shape, F32),
        scratch_shapes=[pltpu.VMEM((2, tm, chunks, LANES), ys_rows.dtype), pltpu.SemaphoreType.DMA(())],
        compiler_params=_cparams(1), name="moe_combine")(pos, gates, ys_rows, x_rows)


def _grouped_up_body(te_ref, nv_ref, a_ref, wg_ref, wu_ref, o_ref):
    @pl.when(pl.program_id(1) < nv_ref[0])
    def _():
        a = a_ref[...]
        g = jnp.dot(a, wg_ref[...], preferred_element_type=F32)
        u = jnp.dot(a, wu_ref[...], preferred_element_type=F32)
        o_ref[...] = (g * (1.0 / (1.0 + jnp.exp(-g))) * u).astype(o_ref.dtype)

    @pl.when(pl.program_id(1) >= nv_ref[0])
    def _():
        o_ref[...] = jnp.zeros_like(o_ref)


def _grouped_down_body(te_ref, nv_ref, a_ref, w_ref, o_ref):
    @pl.when(pl.program_id(1) < nv_ref[0])
    def _():
        o_ref[...] = jnp.dot(a_ref[...], w_ref[...], preferred_element_type=F32).astype(o_ref.dtype)

    @pl.when(pl.program_id(1) >= nv_ref[0])
    def _():
        o_ref[...] = jnp.zeros_like(o_ref)


def _grouped_matmul(body, a, weights, tile_expert, n_valid, *, tmo, tn, name):
    r, k = a.shape
    n = weights[0].shape[2]

    def row(j, i, te, nv):
        return jnp.minimum(i, nv[0] - 1)

    grid_spec = pltpu.PrefetchScalarGridSpec(
        num_scalar_prefetch=2, grid=(n // tn, r // tmo),
        in_specs=[pl.BlockSpec((tmo, k), lambda j, i, te, nv: (row(j, i, te, nv), 0))]
        + [pl.BlockSpec((None, k, tn), lambda j, i, te, nv: (te[row(j, i, te, nv)], 0, j)) for _ in weights],
        out_specs=pl.BlockSpec((tmo, tn), lambda j, i, te, nv: (i, j)))
    return pl.pallas_call(
        body, grid_spec=grid_spec, out_shape=jax.ShapeDtypeStruct((r, n), BF16),
        compiler_params=_cparams(2), name=name)(tile_expert, n_valid, a, *weights)


def _rope_body(pos_ref, f_ref, cos_ref, sin_ref):
    ang = pos_ref[...] * f_ref[...]
    live = f_ref[...] > 0.0
    cos_ref[...] = jnp.where(live, jnp.cos(ang), 0.0)
    sin_ref[...] = jnp.where(live, jnp.sin(ang), 0.0)


def _rope_tables(positions, *, half, tm):
    t = positions.size
    inv_freq = ROPE_BASE ** (-jnp.arange(half, dtype=F32) / half)
    f_row = jnp.concatenate([inv_freq, inv_freq, jnp.zeros((LANES - 2 * half,), F32)]).reshape(1, LANES)
    pos_rep = jnp.broadcast_to(positions.astype(F32).reshape(t, 1), (t, LANES))
    spec = pl.BlockSpec((tm, LANES), lambda i: (i, 0))
    return pl.pallas_call(
        _rope_body, grid=(t // tm,),
        in_specs=[spec, pl.BlockSpec((1, LANES), lambda i: (0, 0))],
        out_specs=[spec, spec],
        out_shape=[jax.ShapeDtypeStruct((t, LANES), F32)] * 2,
        compiler_params=_cparams(1), name="rope_tables")(pos_rep, f_row)


def _bias_body(rb_ref, o_ref, *, t):
    h = pl.program_id(0)
    d = pl.program_id(1)
    j = lax.broadcasted_iota(jnp.int32, (t, t), 0)
    i = lax.broadcasted_iota(jnp.int32, (t, t), 1)
    rel = d * t + i - j
    n = jnp.maximum(rel, 0)
    max_exact = NUM_BUCKETS // 2
    large = max_exact + (
        jnp.log(jnp.maximum(n, max_exact).astype(F32) / max_exact)
        / math.log(MAX_DISTANCE / max_exact) * (NUM_BUCKETS - max_exact)
    ).astype(jnp.int32)
    large = jnp.minimum(large, NUM_BUCKETS - 1)
    bucket = jnp.where(n < max_exact, n, large)
    far = rb_ref[NUM_BUCKETS - 1, h]
    val = jnp.zeros((t, t), F32)
    for b in range(NUM_BUCKETS - 1):
        val = jnp.where(bucket == b, rb_ref[b, h] - far, val)
    o_ref[...] = jnp.where(rel >= 0, val * LOG2E, NEG)


def _bias_tiles(rel_bias, *, t):
    assert t + 1 >= MAX_DISTANCE, "tiles beyond the first sub-diagonal must be in the last bucket"
    nh = rel_bias.shape[1]
    return pl.pallas_call(
        functools.partial(_bias_body, t=t), grid=(nh, 2),
        in_specs=[pl.BlockSpec(memory_space=pltpu.SMEM)],
        out_specs=pl.BlockSpec((None, None, t, t), lambda h, d: (h, d, 0, 0)),
        out_shape=jax.ShapeDtypeStruct((nh, 2, t, t), F32),
        compiler_params=_cparams(2), name="t5_bias_tiles")(rel_bias.astype(F32))


N_PIECES = 3


def _split_pieces(x):
    pieces = []
    rest = x
    for _ in range(N_PIECES):
        p = rest.astype(BF16).astype(F32)
        pieces.append(p)
        rest = rest - p
    return pieces


def _cum_body(shift_ref, lf_ref, o_ref, end_ref, carry_ref, *, n_heads, lane0):
    @pl.when(pl.program_id(1) == 0)
    def _():
        carry_ref[...] = jnp.zeros_like(carry_ref)

    x = lf_ref[...]
    tb = x.shape[0]
    r = lax.broadcasted_iota(jnp.int32, (tb, tb), 0)
    c = lax.broadcasted_iota(jnp.int32, (tb, tb), 1)
    tri = jnp.where(r >= c, 1.0, 0.0).astype(F32)
    cum = jnp.dot(tri, x, preferred_element_type=F32, precision=_HI) + carry_ref[...]
    carry_ref[...] = cum[tb - 1:tb, :]
    cum2 = cum * LOG2E
    end_ref[0] = cum2[tb - 1:tb, :]
    lane = lax.broadcasted_iota(jnp.int32, (tb, LANES), 1)
    shift = shift_ref[0]
    for hh in range(n_heads):
        col = cum2[:, lane0 + hh:lane0 + hh + 1]
        qcols = jnp.where(lane < N_PIECES, 1.0, 0.0)
        kcols = jnp.where((lane >= N_PIECES) & (lane < 2 * N_PIECES), 1.0, 0.0)
        for n, (pq, pk) in enumerate(zip(_split_pieces(col - shift), _split_pieces(-col))):
            qcols = jnp.where(lane == N_PIECES + n, pq, qcols)
            kcols = jnp.where(lane == n, pk, kcols)
        o_ref[:, hh * LANES:(hh + 1) * LANES] = qcols.astype(o_ref.dtype)
        o_ref[:, (n_heads + hh) * LANES:(n_heads + hh + 1) * LANES] = kcols.astype(o_ref.dtype)


def _forget_columns(logf, shift, *, batch, seq, n_heads, lane0, tb):
    nb = seq // tb
    width = 2 * n_heads * LANES
    return pl.pallas_call(
        functools.partial(_cum_body, n_heads=n_heads, lane0=lane0), grid=(batch, nb),
        in_specs=[pl.BlockSpec(memory_space=pltpu.SMEM),
                  pl.BlockSpec((tb, LANES), lambda b, i: (b * nb + i, 0))],
        out_specs=[pl.BlockSpec((tb, width), lambda b, i: (b * nb + i, 0)),
                   pl.BlockSpec((1, 1, LANES), lambda b, i: (b * nb + i, 0, 0))],
        out_shape=[jax.ShapeDtypeStruct((batch * seq, width), BF16),
                   jax.ShapeDtypeStruct((batch * nb, 1, LANES), F32)],
        scratch_shapes=[pltpu.VMEM((1, LANES), F32)],
        compiler_params=_cparams(2), name="forget_cumsum")(shift.reshape(1).astype(F32), logf)


def _online_block(s, vb, m_ref, acc_ref):
    m_prev = m_ref[...]
    m_new = jnp.maximum(m_prev, jnp.max(s, axis=0, keepdims=True))
    alpha = jnp.exp2(m_prev - m_new)
    p = jnp.exp2(s - m_new).astype(BF16)
    acc_ref[...] = alpha * acc_ref[...] + jnp.dot(vb, p, preferred_element_type=F32)
    m_ref[...] = m_new


def _shifted_block(s, vb, acc_ref):
    p = jnp.exp2(s).astype(BF16)
    acc_ref[...] += jnp.dot(vb, p, preferred_element_type=F32)


def _causal_mask(t):
    j = lax.broadcasted_iota(jnp.int32, (t, t), 0)
    i = lax.broadcasted_iota(jnp.int32, (t, t), 1)
    return j <= i


_NT = (((1,), (1,)), ((), ()))
FAR_UNROLL = 4


def _init_state(m_ref, acc_ref):
    m_ref[...] = jnp.full(m_ref.shape, NEG, F32)
    acc_ref[...] = jnp.zeros(acc_ref.shape, F32)


def _attn_body(*refs, mode, t, hp, dq, dv, qk_dim=None, lambda_init=None):
    safe_ref, start_ref, q_ref, k_ref, vt_ref = refs[:5]
    if mode == "latent":
        o_ref, qc_ref, m_ref, acc_ref = refs[5:]
    elif mode == "forget":
        qx_ref, kx_ref, o_ref, qc_ref, m_ref, acc_ref = refs[5:]
    else:
        qx_ref, kx_ref, bias_ref, lam_ref, hn_ref, o_ref, qc_ref, m_ref, acc_ref = refs[5:]
    n_map = 2 if mode == "diff" else 1
    qi = pl.program_id(2)
    step = (pl.program_id(0) * pl.num_programs(1) + pl.program_id(1)) * pl.num_programs(2) + qi
    _init_state(m_ref, acc_ref)

    for hh in range(hp):
        q = q_ref[:, hh * dq:(hh + 1) * dq]
        if mode == "latent":
            qc_ref[hh] = q
        elif mode == "forget":
            qc_ref[hh] = jnp.concatenate([q, qx_ref[:, hh * LANES:(hh + 1) * LANES]], axis=1)
        else:
            lane = lax.broadcasted_iota(jnp.int32, (t, dq), 1)
            zero = jnp.zeros_like(q)
            qc_ref[2 * hh] = jnp.concatenate([jnp.where(lane < qk_dim, q, zero), qx_ref[...]], axis=1)
            qc_ref[2 * hh + 1] = jnp.concatenate([jnp.where(lane < qk_dim, zero, q), qx_ref[...]], axis=1)

    def keys(kj, hh):
        start = pl.multiple_of(kj * t, t)
        kb = k_ref[pl.ds(start, t), hh * dq:(hh + 1) * dq]
        if mode == "forget":
            kb = jnp.concatenate([kb, kx_ref[pl.ds(start, t), hh * LANES:(hh + 1) * LANES]], axis=1)
        elif mode == "diff":
            kb = jnp.concatenate([kb, kx_ref[hh]], axis=1)
        return kb

    def run(update, skip_dead):
        def scores(kj, kind, hh, mm):
            s = lax.dot_general(keys(kj, hh), qc_ref[n_map * hh + mm], _NT, preferred_element_type=F32)
            if mode == "diff":
                if kind != "far":
                    s = s + bias_ref[hh, 0 if kind == "diag" else 1]
            elif kind == "diag":
                s = jnp.where(_causal_mask(t), s, NEG)
            return s

        def blocks(items):
            work = [(kj, kind, hh, mm) for kj, kind in items for hh in range(hp) for mm in range(n_map)]
            s_next = scores(*work[0])
            for n, (kj, kind, hh, mm) in enumerate(work):
                s_cur = s_next
                if n + 1 < len(work):
                    s_next = scores(*work[n + 1])
                update(s_cur, vt_ref[hh, kj], n_map * hh + mm)

        n_far = jnp.maximum(qi - 1, 0) if mode == "diff" else qi
        k0 = jnp.minimum(start_ref[step], n_far) if skip_dead else 0
        n_groups = (n_far - k0) // FAR_UNROLL

        def far_group(g, carry):
            blocks([(k0 + g * FAR_UNROLL + u, "far") for u in range(FAR_UNROLL)])
            return carry

        def far_single(kj, carry):
            blocks([(kj, "far")])
            return carry

        lax.fori_loop(0, n_groups, far_group, 0)
        lax.fori_loop(k0 + n_groups * FAR_UNROLL, n_far, far_single, 0)

        if mode == "diff":
            @pl.when(qi >= 1)
            def _():
                blocks([(qi - 1, "sub")])
        blocks([(qi, "diag")])

    @pl.when(safe_ref[0] != 0)
    def _():
        run(lambda s, vb, c: _shifted_block(s, vb, acc_ref.at[c]), True)

    @pl.when(safe_ref[0] == 0)
    def _():
        run(lambda s, vb, c: _online_block(s, vb, m_ref.at[c], acc_ref.at[c]), False)

    if mode == "diff":
        lp = lam_ref[...]
        lam = (jnp.exp(jnp.sum(lp[0:1] * lp[1:2], axis=-1, keepdims=True))
               - jnp.exp(jnp.sum(lp[2:3] * lp[3:4], axis=-1, keepdims=True)) + lambda_init)
    for hh in range(hp):
        if mode == "diff":
            a0 = acc_ref[2 * hh]
            a1 = acc_ref[2 * hh + 1]
            o = a0[:dv] * (1.0 / a0[dv:dv + 1]) - lam * (a1[:dv] * (1.0 / a1[dv:dv + 1]))
            r = lax.rsqrt(jnp.sum(o * o, axis=0, keepdims=True) * (1.0 / dv) + EPS)
            o = o * r * hn_ref[...] * (1.0 - lambda_init)
        else:
            a = acc_ref[hh]
            o = a[:dv] * (1.0 / a[dv:dv + 1])
        o_ref[:, hh * dv:(hh + 1) * dv] = o.T.astype(o_ref.dtype)


def _attention(mode, safe, q_arr, k_arr, vt_arr, *, batch, seq, n_heads, t, hp, dq, q_col0, k_col0,
               extra=(), extra_specs=(), body_kwargs=None, first_block=None, name):
    nq = seq // t
    dvp = vt_arr.shape[2]
    dv = dvp - ONES_ROWS
    tokens = batch * seq
    assert n_heads % hp == 0 and q_col0 % hp == 0 and k_col0 % hp == 0
    qc, kc = q_col0 // hp, k_col0 // hp
    if first_block is None:
        first_block = jnp.zeros((batch * (n_heads // hp) * nq,), jnp.int32)
    in_specs = [
        pl.BlockSpec(memory_space=pltpu.SMEM),
        pl.BlockSpec(memory_space=pltpu.SMEM),
        pl.BlockSpec((t, hp * dq), lambda b, h, i: (b * nq + i, qc + h)),
        pl.BlockSpec((seq, hp * dq), lambda b, h, i: (b, kc + h)),
        pl.BlockSpec((hp, nq, dvp, t), lambda b, h, i: (h, b, 0, 0)),
        *extra_specs,
    ]
    n_chain = 2 * hp if mode == "diff" else hp
    scratch = [pltpu.VMEM((n_chain, t, 2 * LANES), BF16),
               pltpu.VMEM((n_chain, 1, t), F32), pltpu.VMEM((n_chain, dvp, t), F32)]
    body = functools.partial(_attn_body, mode=mode, t=t, hp=hp, dq=dq, dv=dv, **(body_kwargs or {}))
    return pl.pallas_call(
        body, grid=(batch, n_heads // hp, nq), in_specs=in_specs,
        out_specs=pl.BlockSpec((t, hp * dv), lambda b, h, i: (b * nq + i, h)),
        out_shape=jax.ShapeDtypeStruct((tokens, n_heads * dv), BF16),
        scratch_shapes=scratch, compiler_params=_cparams(3), name=name)(
            safe, first_block, q_arr, k_arr, vt_arr, *extra)


def _pick(n, prefs):
    for p in prefs:
        if n % p == 0:
            return p
    raise ValueError((n, prefs))


SKIP_LOG2 = 160.0
SAFE_DEPTH = 110.0


def _round_up_bf16(c):
    return (c * (1.0 + 2.0 ** -6)).astype(BF16).astype(F32)


def _safe_flag(depth):
    return (depth <= SAFE_DEPTH).astype(jnp.int32).reshape(1)


def kernel(x, positions, attn_norm, w_in, b_forget, a_qk_norm, a_lambda, a_head_norm, rel_bias,
           b_qk_norm, c_q_a_norm, c_kv_a_norm, c_w_uq, c_w_ukv, c_qk_norm_nope, c_qk_norm_rope,
           w_out, ffn_norm, dense_w_gate, dense_w_up, dense_w_down, moe_router, moe_w_gate,
           moe_w_up, moe_w_down):
    batch, seq, d_model = x.shape
    depth = w_in.shape[0]
    tokens = batch * seq
    n_slots = d_model // LANES
    a_heads = n_slots // 4
    b_heads = n_slots // 4
    c_heads = n_slots // 2
    a_qk = a_qk_norm.shape[-1]
    b_dim = b_qk_norm.shape[-1]
    c_q_lora = c_q_a_norm.shape[-1]
    c_kv_lora = c_kv_a_norm.shape[-1]
    c_nope = c_qk_norm_nope.shape[-1]
    c_rope = c_qk_norm_rope.shape[-1]
    c_v = c_w_ukv.shape[-1] // c_heads - c_nope
    n_exp = moe_router.shape[-1]
    assert 2 * a_qk == LANES and b_dim == LANES and c_nope == LANES and c_v == LANES and 2 * c_rope == LANES

    t = _pick(seq, (512, 256, 128))
    tm = _pick(tokens, (1024, 512, 256))
    tm_n = _pick(tokens, (256, 128))
    hp_a, hp_b, hp_c = 1, 2, 2

    sizes = (a_heads * 2 * a_qk, a_heads * 2 * a_qk, a_heads * LANES, b_heads * b_dim, b_heads * b_dim,
             b_heads * b_dim, b_heads, c_q_lora, c_kv_lora, c_rope)
    offs = [0]
    for s in sizes:
        offs.append(offs[-1] + s)
    o_aq, o_ak, o_av, o_bq, o_bk, o_bv, o_f, o_cq, o_ckv, o_kpe, o_end = offs

    xf = x.reshape(tokens, d_model)
    cos_t, sin_t = _rope_tables(positions, half=c_rope // 2, tm=tm)
    bias_tiles = _bias_tiles(rel_bias, t=t)
    bias_log2 = (rel_bias.astype(F32) - rel_bias[-1:].astype(F32)) * LOG2E
    bias_max = jnp.max(bias_log2, axis=0)
    qx_a = jnp.zeros((t, LANES), F32).at[:, 0].set(1.0).astype(BF16)

    row_spec = lambda width: pl.BlockSpec((1, width), lambda i, j: (0, j))
    fixed_row = pl.BlockSpec((1, LANES), lambda i, j: (0, 0))
    tok_lane = pl.BlockSpec((tm, LANES), lambda i, j: (i, 0))

    def vt_call(h_bf, w, nh, name, a_cols=None, w_cols=None):
        tn = _pick(nh * LANES, (512, 256, 128))
        return _matmul(
            h_bf, w, a_cols=a_cols, w_cols=w_cols, tm=tm, tn=tn,
            epilogue=functools.partial(_ep_transpose_heads, t=t),
            out_shape=jax.ShapeDtypeStruct((nh, tokens // t, LANES + ONES_ROWS, t), BF16),
            out_specs=pl.BlockSpec((tn // LANES, tm // t, LANES + ONES_ROWS, t), lambda i, j: (j, i, 0, 0)),
            name=name)

    for layer in range(depth):
        lambda_init = 0.8 - 0.6 * math.exp(-0.3 * layer)
        w = w_in[layer].astype(BF16)

        h = _rms_norm(xf, attn_norm[layer], tm=tm_n, name="attn_norm")

        a_scale = a_qk ** -0.5 * LOG2E
        gain_a = jnp.concatenate([jnp.tile(a_qk_norm[layer, 0] * a_scale, 2 * a_heads),
                                  jnp.tile(a_qk_norm[layer, 1], 2 * a_heads)]).reshape(1, -1)
        tn = 512
        qk_a = _matmul(h, w, w_cols=(o_aq, o_av - o_aq), tm=tm, tn=tn,
                       epilogue=functools.partial(_ep_group_norm, group=a_qk),
                       aux=(gain_a,), aux_specs=(row_spec(tn),),
                       out_shape=jax.ShapeDtypeStruct((tokens, o_av - o_aq), BF16),
                       out_specs=pl.BlockSpec((tm, tn), lambda i, j: (i, j)), name="proj_a_qk")
        vt_a = vt_call(h, w, a_heads, "proj_a_v", w_cols=(o_av, o_bq - o_av))
        hn = jnp.broadcast_to(a_head_norm[layer].reshape(LANES, 1), (LANES, t))
        bound_a = a_qk * jnp.max(jnp.abs(a_qk_norm[layer, 0] * a_scale)) * jnp.max(jnp.abs(a_qk_norm[layer, 1]))
        shift_a = _round_up_bf16(bound_a + bias_max)
        safe_a = _safe_flag(2.0 * bound_a + jnp.max(bias_max - bias_log2[0]))
        kx_a = jnp.zeros((a_heads, t, LANES), F32).at[:, :, 0].set(-shift_a[:, None]).astype(BF16)
        o_a = _attention(
            "diff", safe_a, qk_a, qk_a, vt_a, batch=batch, seq=seq, n_heads=a_heads, t=t, hp=hp_a, dq=LANES,
            q_col0=0, k_col0=a_heads,
            extra=(qx_a, kx_a, bias_tiles, a_lambda[layer], hn),
            extra_specs=(pl.BlockSpec((t, LANES), lambda b, hh, i: (0, 0)),
                         pl.BlockSpec((hp_a, t, LANES), lambda b, hh, i: (hh, 0, 0)),
                         pl.BlockSpec((hp_a, 2, t, t), lambda b, hh, i: (hh, 0, 0, 0)),
                         pl.BlockSpec((4, a_qk), lambda b, hh, i: (0, 0)),
                         pl.BlockSpec((LANES, t), lambda b, hh, i: (0, 0))),
            body_kwargs=dict(qk_dim=a_qk, lambda_init=lambda_init), name="attn_diff")

        b_scale = b_dim ** -0.5 * LOG2E
        gain_b = jnp.concatenate([jnp.tile(b_qk_norm[layer, 0] * b_scale, b_heads),
                                  jnp.tile(b_qk_norm[layer, 1], b_heads)]).reshape(1, -1)
        qk_b = _matmul(h, w, w_cols=(o_bq, o_bv - o_bq), tm=tm, tn=tn,
                       epilogue=functools.partial(_ep_group_norm, group=b_dim),
                       aux=(gain_b,), aux_specs=(row_spec(tn),),
                       out_shape=jax.ShapeDtypeStruct((tokens, o_bv - o_bq), BF16),
                       out_specs=pl.BlockSpec((tm, tn), lambda i, j: (i, j)), name="proj_b_qk")
        vt_b = vt_call(h, w, b_heads, "proj_b_v", w_cols=(o_bv, o_f - o_bv))

        w_tail = jnp.concatenate([w[:, o_kpe:o_end], w[:, o_f:o_cq],
                                  jnp.zeros((d_model, LANES - c_rope - b_heads), BF16)], axis=1)
        gain_tail = jnp.concatenate([c_qk_norm_rope[layer, 1], jnp.zeros((LANES - c_rope,), F32)]).reshape(1, LANES)
        bias_tail = jnp.zeros((LANES,), F32).at[c_rope:c_rope + b_heads].set(b_forget[layer]).reshape(1, LANES)
        c_scale = (c_nope + c_rope) ** -0.5 * LOG2E
        gq_n = c_qk_norm_nope[layer, 0] * c_scale
        gq_r = c_qk_norm_rope[layer, 0] * c_scale
        gk_n = c_qk_norm_nope[layer, 1]
        gk_r = c_qk_norm_rope[layer, 1]
        bound_c = (jnp.sqrt(c_nope * jnp.max(gq_n * gq_n) + c_rope * jnp.max(gq_r * gq_r))
                   * jnp.sqrt(c_nope * jnp.max(gk_n * gk_n) + c_rope * jnp.max(gk_r * gk_r)))
        shift_c = _round_up_bf16(bound_c)
        safe_c = _safe_flag(2.0 * bound_c)
        shift_col = jnp.zeros((1, LANES), F32).at[0, c_rope].set(-shift_c)
        one_col = jnp.zeros((1, LANES), F32).at[0, c_rope].set(1.0)
        kpe, logf = _matmul(
            h, w_tail, tm=tm, tn=LANES,
            epilogue=functools.partial(_ep_tail, rope_dim=c_rope, n_gate=b_heads),
            aux=(gain_tail, bias_tail, cos_t, sin_t, shift_col),
            aux_specs=(fixed_row, fixed_row, tok_lane, tok_lane, fixed_row),
            out_shape=[jax.ShapeDtypeStruct((tokens, LANES), BF16), jax.ShapeDtypeStruct((tokens, LANES), F32)],
            out_specs=[tok_lane, tok_lane], name="proj_tail")
        bound_b = b_dim * jnp.max(jnp.abs(b_qk_norm[layer, 0] * b_scale)) * jnp.max(jnp.abs(b_qk_norm[layer, 1]))
        safe_b = _safe_flag(2.0 * bound_b)
        x_b, ends = _forget_columns(logf, bound_b, batch=batch, seq=seq, n_heads=b_heads, lane0=c_rope, tb=t)
        nq = seq // t
        nbh = b_heads // hp_b
        e_b = ends[:, 0, c_rope:c_rope + b_heads].reshape(batch, nq, b_heads)
        e_q = jnp.concatenate([jnp.zeros((batch, 1, b_heads), F32), e_b[:, :-1]], axis=1)
        dead = (e_q[:, :, None, :] - e_b[:, None, :, :]) < -SKIP_LOG2
        dead = dead & (jnp.arange(nq)[None, None, :, None] + 1 < jnp.arange(nq)[None, :, None, None])
        first_b = jnp.sum(dead.astype(jnp.int32), axis=2)
        first_b = jnp.min(first_b.reshape(batch, nq, nbh, hp_b), axis=3)
        first_b = jnp.transpose(first_b, (0, 2, 1)).reshape(-1)
        o_b = _attention(
            "forget", safe_b, qk_b, qk_b, vt_b, batch=batch, seq=seq, n_heads=b_heads, t=t, hp=hp_b, dq=LANES,
            q_col0=0, k_col0=b_heads,
            extra=(x_b, x_b),
            extra_specs=(pl.BlockSpec((t, hp_b * LANES), lambda b, hh, i: (b * nq + i, hh)),
                         pl.BlockSpec((seq, hp_b * LANES), lambda b, hh, i: (b, nbh + hh))),
            first_block=first_b, name="attn_forget")

        n_lat = c_q_lora + c_kv_lora
        gain_lat = jnp.concatenate([c_q_a_norm[layer], c_kv_a_norm[layer]]).reshape(1, n_lat)
        tm_lat = _pick(tokens, (512, 256))
        lat = _matmul(h, w[:, o_cq:o_kpe], tm=tm_lat, tn=n_lat,
                      epilogue=functools.partial(_ep_latent_norms, n_q=c_q_lora),
                      aux=(gain_lat,), aux_specs=(row_spec(n_lat),),
                      out_shape=jax.ShapeDtypeStruct((tokens, n_lat), BF16),
                      out_specs=pl.BlockSpec((tm_lat, n_lat), lambda i, j: (i, j)), name="proj_c_latent")
        wq = c_w_uq[layer].astype(BF16).reshape(c_q_lora, c_heads, c_nope + c_rope)
        wq = jnp.concatenate([wq, jnp.zeros((c_q_lora, c_heads, 2 * LANES - c_nope - c_rope), BF16)], axis=2)
        wq = wq.reshape(c_q_lora, c_heads * 2 * LANES)
        gq_nope = gq_n.reshape(1, LANES)
        gq_rope = jnp.concatenate([gq_r, jnp.zeros((LANES - c_rope,), F32)]).reshape(1, LANES)
        tn_q = 512
        lat_q_cols, lat_kv_cols = (0, c_q_lora), (c_q_lora, c_kv_lora)
        q_c = _matmul(lat, wq, a_cols=lat_q_cols, tm=tm, tn=tn_q, epilogue=_ep_latent_q,
                      aux=(gq_nope, gq_rope, cos_t, sin_t, one_col),
                      aux_specs=(fixed_row, fixed_row, tok_lane, tok_lane, fixed_row),
                      out_shape=jax.ShapeDtypeStruct((tokens, c_heads * 2 * LANES), BF16),
                      out_specs=pl.BlockSpec((tm, tn_q), lambda i, j: (i, j)), name="proj_c_q")
        wkv = c_w_ukv[layer].astype(BF16).reshape(c_kv_lora, c_heads, c_nope + c_v)
        wk = wkv[:, :, :c_nope].reshape(c_kv_lora, c_heads * c_nope)
        wv = wkv[:, :, c_nope:].reshape(c_kv_lora, c_heads * c_v)
        gk_nope = c_qk_norm_nope[layer, 1].reshape(1, LANES)
        tn_k = 256
        k_c = _matmul(lat, wk, a_cols=lat_kv_cols, tm=tm, tn=tn_k, epilogue=_ep_latent_k,
                      aux=(gk_nope, kpe), aux_specs=(fixed_row, tok_lane),
                      out_shape=jax.ShapeDtypeStruct((tokens, c_heads * 2 * LANES), BF16),
                      out_specs=pl.BlockSpec((tm, 2 * tn_k), lambda i, j: (i, j)), name="proj_c_k")
        vt_c = vt_call(lat, wv, c_heads, "proj_c_v", a_cols=lat_kv_cols)
        o_c = _attention(
            "latent", safe_c, q_c, k_c, vt_c, batch=batch, seq=seq, n_heads=c_heads, t=t, hp=hp_c, dq=2 * LANES,
            q_col0=0, k_col0=0, name="attn_latent")

        tn_o = 512
        xf = _out_proj((o_a, o_b, o_c), w_out[layer].astype(BF16), xf, tm=tm, tn=tn_o)

        idx = layer // 2
        th = 256
        if layer % 2 == 0:
            h2 = _rms_norm(xf, ffn_norm[layer], tm=tm_n, name="ffn_norm")
            act = _swiglu_up(h2, dense_w_gate[idx].astype(BF16), dense_w_up[idx].astype(BF16),
                             tm=tm, th=th, name="ffn_up")
            tm_d = _pick(tokens, (512, 256))
            res_d = pl.BlockSpec((tm_d, tn_o), lambda i, j: (i, j))
            xf = _matmul(act, dense_w_down[idx].astype(BF16), tm=tm_d, tn=tn_o, epilogue=_ep_residual,
                         aux=(xf,), aux_specs=(res_d,),
                         out_shape=jax.ShapeDtypeStruct((tokens, d_model), F32), out_specs=res_d,
                         name="ffn_down")
        else:
            h2, route, counts = _rms_norm_router(xf, ffn_norm[layer], moe_router[idx], tm=tm_n,
                                                 name="ffn_norm_router")
            tmo = _pick(tokens, (512, 256, 128))
            n_rows = TOP_K * tokens + n_exp * tmo
            cnt = counts[0, :n_exp].astype(jnp.int32)
            padded = (cnt + tmo - 1) // tmo * tmo
            ends = jnp.cumsum(padded)
            starts = ends - padded
            n_valid = (ends[-1] // tmo).reshape(1).astype(jnp.int32)
            tile_expert = jnp.minimum(
                jnp.searchsorted(ends, jnp.arange(n_rows // tmo, dtype=jnp.int32) * tmo, side="right"),
                n_exp - 1).astype(jnp.int32)
            e1 = route[:, ROUTE_I1].astype(jnp.int32)
            e2 = route[:, ROUTE_I2].astype(jnp.int32)
            pos1 = starts[e1] + route[:, ROUTE_R1].astype(jnp.int32)
            pos2 = starts[e2] + route[:, ROUTE_R2].astype(jnp.int32)
            tm_r = _pick(tokens, (512, 256, 128))
            per_tile = lambda a, b: jnp.concatenate(
                [a.reshape(tokens // tm_r, 1, tm_r), b.reshape(tokens // tm_r, 1, tm_r)], axis=2)
            pos = per_tile(pos1, pos2)
            gates = per_tile(route[:, ROUTE_G1], route[:, ROUTE_G2])

            tok = jnp.arange(tokens, dtype=jnp.int32)
            src = jnp.zeros((n_rows,), jnp.int32).at[jnp.concatenate([pos1, pos2])].set(
                jnp.concatenate([tok, tok]), unique_indices=True)
            xs = _dispatch(_row_view(h2), src.reshape(n_rows // tm_r, 1, tm_r), tm=tm_r).reshape(n_rows, d_model)
            act = _grouped_matmul(_grouped_up_body, xs,
                                  (moe_w_gate[idx].astype(BF16), moe_w_up[idx].astype(BF16)),
                                  tile_expert, n_valid, tmo=tmo, tn=512, name="moe_up")
            ys = _grouped_matmul(_grouped_down_body, act, (moe_w_down[idx].astype(BF16),),
                                 tile_expert, n_valid, tmo=tmo, tn=1024, name="moe_down")
            xf = _combine(_row_view(ys), pos, gates, _row_view(xf), tm=tm_r).reshape(tokens, d_model)

    return xf.reshape(batch, seq, d_model)
```

```python
import functools
import math

import jax
import jax.numpy as jnp
from jax import lax
from jax.experimental import pallas as pl
from jax.experimental.pallas import tpu as pltpu

F32 = jnp.float32
BF16 = jnp.bfloat16

LANES = 128
ONES_ROWS = 16
LOG2E = 1.4426950408889634
NEG = -1e30
EPS = 1e-6
VMEM_LIMIT = 52 * 1024 * 1024

NUM_BUCKETS = 32
MAX_DISTANCE = 128
ROPE_BASE = 10000.0
TOP_K = 2

_HI = lax.Precision.HIGHEST


def _cparams(n_axes):
    return pltpu.CompilerParams(
        dimension_semantics=("arbitrary",) * n_axes, vmem_limit_bytes=VMEM_LIMIT)


def _mm_body(a_ref, w_ref, *rest, epilogue, n_aux):
    acc = jnp.dot(a_ref[...], w_ref[...], preferred_element_type=F32)
    epilogue(acc, rest[:n_aux], rest[n_aux:])


def _matmul(a, w, *, tm, tn, epilogue, out_shape, out_specs, aux=(), aux_specs=(), name,
            a_cols=None, w_cols=None):
    m = a.shape[0]
    a0, k = a_cols if a_cols is not None else (0, a.shape[1])
    w0, n = w_cols if w_cols is not None else (0, w.shape[1])
    if a0 % k:
        a, a0 = a[:, a0:a0 + k], 0
    if w0 % tn:
        w, w0 = w[:, w0:w0 + n], 0
    assert k == w.shape[0]
    assert m % tm == 0 and n % tn == 0, (m, n, tm, tn)
    ab, wb = a0 // k, w0 // tn
    return pl.pallas_call(
        functools.partial(_mm_body, epilogue=epilogue, n_aux=len(aux)),
        grid=(m // tm, n // tn),
        in_specs=[pl.BlockSpec((tm, k), lambda i, j: (i, ab)),
                  pl.BlockSpec((k, tn), lambda i, j: (0, wb + j)), *aux_specs],
        out_specs=out_specs, out_shape=out_shape,
        compiler_params=_cparams(2), name=name)(a, w, *aux)


def _sumsq_lanes(x):
    return jnp.sum(x * x, axis=-1, keepdims=True)


def _group_ones(group):
    r = lax.broadcasted_iota(jnp.int32, (LANES, LANES), 0)
    c = lax.broadcasted_iota(jnp.int32, (LANES, LANES), 1)
    same = (r < group) == (c < group) if group < LANES else (r >= 0)
    return jnp.where(same, 1.0, 0.0).astype(BF16)


def _group_sumsq(x, ones):
    return jnp.dot((x * x).astype(BF16), ones, preferred_element_type=F32)


def _ep_residual(acc, aux, outs):
    outs[0][...] = aux[0][...] + acc


def _out_proj_body(*refs, widths):
    part_refs = refs[:len(widths)]
    w_ref, x_ref, o_ref = refs[len(widths):]
    acc = x_ref[...]
    row = 0
    for p_ref, width in zip(part_refs, widths):
        acc = acc + jnp.dot(p_ref[...], w_ref[row:row + width, :], preferred_element_type=F32)
        row += width
    o_ref[...] = acc


def _out_proj(parts, w, x, *, tm, tn):
    m, n = x.shape
    widths = tuple(p.shape[1] for p in parts)
    res = pl.BlockSpec((tm, tn), lambda i, j: (i, j))
    return pl.pallas_call(
        functools.partial(_out_proj_body, widths=widths), grid=(m // tm, n // tn),
        in_specs=[pl.BlockSpec((tm, wd), lambda i, j: (i, 0)) for wd in widths]
        + [pl.BlockSpec((w.shape[0], tn), lambda i, j: (0, j)), res],
        out_specs=res, out_shape=jax.ShapeDtypeStruct((m, n), F32),
        compiler_params=_cparams(2), name="out_proj")(*parts, w, x)


def _swiglu_up_body(a_ref, wg_ref, wu_ref, o_ref):
    a = a_ref[...]
    g = jnp.dot(a, wg_ref[...], preferred_element_type=F32)
    u = jnp.dot(a, wu_ref[...], preferred_element_type=F32)
    o_ref[...] = (g * (1.0 / (1.0 + jnp.exp(-g))) * u).astype(o_ref.dtype)


def _swiglu_up(a, wg, wu, *, tm, th, name):
    m, k = a.shape
    f = wg.shape[1]
    w_spec = pl.BlockSpec((k, th), lambda i, j: (0, j))
    return pl.pallas_call(
        _swiglu_up_body, grid=(m // tm, f // th),
        in_specs=[pl.BlockSpec((tm, k), lambda i, j: (i, 0)), w_spec, w_spec],
        out_specs=pl.BlockSpec((tm, th), lambda i, j: (i, j)),
        out_shape=jax.ShapeDtypeStruct((m, f), BF16),
        compiler_params=_cparams(2), name=name)(a, wg, wu)


def _ep_group_norm(acc, aux, outs, *, group):
    gain = aux[0][...]
    tn = acc.shape[1]
    ones = _group_ones(group)
    for c in range(tn // LANES):
        x = acc[:, c * LANES:(c + 1) * LANES]
        g = gain[:, c * LANES:(c + 1) * LANES]
        r = lax.rsqrt(_group_sumsq(x, ones) * (1.0 / group) + EPS)
        outs[0][:, c * LANES:(c + 1) * LANES] = (x * r * g).astype(outs[0].dtype)


def _ep_transpose_heads(acc, aux, outs, *, t):
    tm, tn = acc.shape
    ones = jnp.ones((ONES_ROWS, t), outs[0].dtype)
    for hh in range(tn // LANES):
        for s in range(tm // t):
            blk = acc[s * t:(s + 1) * t, hh * LANES:(hh + 1) * LANES]
            outs[0][hh, s, :LANES, :] = blk.T.astype(outs[0].dtype)
            outs[0][hh, s, LANES:, :] = ones


def _ep_latent_norms(acc, aux, outs, *, n_q):
    gain = aux[0][...]
    n = acc.shape[1]
    xq = acc[:, :n_q]
    xk = acc[:, n_q:]
    rq = lax.rsqrt(_sumsq_lanes(xq) * (1.0 / n_q) + EPS)
    rk = lax.rsqrt(_sumsq_lanes(xk) * (1.0 / (n - n_q)) + EPS)
    outs[0][:, :n_q] = (xq * rq * gain[:, :n_q]).astype(outs[0].dtype)
    outs[0][:, n_q:] = (xk * rk * gain[:, n_q:]).astype(outs[0].dtype)


def _rotate_half_matrix():
    src = lax.broadcasted_iota(jnp.int32, (LANES, LANES), 0)
    dst = lax.broadcasted_iota(jnp.int32, (LANES, LANES), 1)
    neg = (dst < 32) & (src == dst + 32)
    pos = (dst >= 32) & (dst < 64) & (src == dst - 32)
    return jnp.where(neg, -1.0, jnp.where(pos, 1.0, 0.0)).astype(BF16)


def _rope_half_block(y, cos, sin):
    lane = lax.broadcasted_iota(jnp.int32, (1, LANES), 1)
    ra = pltpu.roll(y, 32, axis=1)
    rb = pltpu.roll(y, 96, axis=1)
    rot = jnp.where(lane < 32, -rb, ra)
    return y * cos + rot * sin


def _ep_tail(acc, aux, outs, *, rope_dim, n_gate):
    gain = aux[0][...]
    bias = aux[1][...]
    cos = aux[2][...]
    sin = aux[3][...]
    shift_col = aux[4][...]
    lane = lax.broadcasted_iota(jnp.int32, (1, LANES), 1)
    is_pe = lane < rope_dim
    xpe = jnp.where(is_pe, acc, 0.0)
    r = lax.rsqrt(_sumsq_lanes(xpe) * (1.0 / rope_dim) + EPS)
    y = xpe * r * gain
    outs[0][...] = (_rope_half_block(y, cos, sin) + shift_col).astype(outs[0].dtype)
    z = acc + bias
    logsig = -(jnp.maximum(-z, 0.0) + jnp.log(1.0 + jnp.exp(-jnp.abs(z))))
    is_gate = (lane >= rope_dim) & (lane < rope_dim + n_gate)
    outs[1][...] = jnp.where(is_gate, logsig, 0.0)


def _ep_latent_q(acc, aux, outs):
    g_nope = aux[0][...]
    g_rope = aux[1][...]
    cos = aux[2][...]
    sin = aux[3][...]
    one_col = aux[4][...]
    ones = _group_ones(LANES)
    rot_mat = _rotate_half_matrix()
    for hh in range(acc.shape[1] // (2 * LANES)):
        xn = acc[:, hh * 256:hh * 256 + LANES]
        xr = acc[:, hh * 256 + LANES:(hh + 1) * 256]
        rn = lax.rsqrt(_group_sumsq(xn, ones) * (1.0 / LANES) + EPS)
        outs[0][:, hh * 256:hh * 256 + LANES] = (xn * rn * g_nope).astype(outs[0].dtype)
        rr = lax.rsqrt(_group_sumsq(xr, ones) * (1.0 / 64) + EPS)
        y = xr * rr * g_rope
        rot = jnp.dot(y.astype(BF16), rot_mat, preferred_element_type=F32)
        outs[0][:, hh * 256 + LANES:(hh + 1) * 256] = (y * cos + rot * sin + one_col).astype(outs[0].dtype)


def _ep_latent_k(acc, aux, outs):
    g_nope = aux[0][...]
    kpe = aux[1][...]
    ones = _group_ones(LANES)
    for hh in range(acc.shape[1] // LANES):
        xn = acc[:, hh * LANES:(hh + 1) * LANES]
        rn = lax.rsqrt(_group_sumsq(xn, ones) * (1.0 / LANES) + EPS)
        outs[0][:, hh * 256:hh * 256 + LANES] = (xn * rn * g_nope).astype(outs[0].dtype)
        outs[0][:, hh * 256 + LANES:(hh + 1) * 256] = kpe


def _norm_body(x_ref, g_ref, h_ref):
    x = x_ref[...]
    d = x.shape[1]
    r = lax.rsqrt(_sumsq_lanes(x) * (1.0 / d) + EPS)
    h_ref[...] = (x * r * g_ref[...]).astype(h_ref.dtype)


def _rms_norm(x, gain, *, tm, name):
    t, d = x.shape
    return pl.pallas_call(
        _norm_body, grid=(t // tm,),
        in_specs=[pl.BlockSpec((tm, d), lambda i: (i, 0)), pl.BlockSpec((1, d), lambda i: (0, 0))],
        out_specs=pl.BlockSpec((tm, d), lambda i: (i, 0)),
        out_shape=jax.ShapeDtypeStruct((t, d), BF16),
        compiler_params=_cparams(1), name=name)(x, gain.reshape(1, d))


def _add_norm_body(x_ref, d_ref, g_ref, xo_ref, h_ref):
    x = x_ref[...] + d_ref[...].astype(F32)
    xo_ref[...] = x
    r = lax.rsqrt(_sumsq_lanes(x) * (1.0 / x.shape[1]) + EPS)
    h_ref[...] = (x * r * g_ref[...]).astype(h_ref.dtype)


def _add_rms_norm(x, delta, gain, *, tm, name):
    t, d = x.shape
    tile = pl.BlockSpec((tm, d), lambda i: (i, 0))
    return pl.pallas_call(
        _add_norm_body, grid=(t // tm,),
        in_specs=[tile, tile, pl.BlockSpec((1, d), lambda i: (0, 0))],
        out_specs=[tile, tile],
        out_shape=[jax.ShapeDtypeStruct((t, d), F32), jax.ShapeDtypeStruct((t, d), BF16)],
        compiler_params=_cparams(1), name=name)(x, delta, gain.reshape(1, d))


def _add_body(x_ref, d_ref, o_ref):
    o_ref[...] = x_ref[...] + d_ref[...].astype(F32)


def _residual_add(x, delta, *, tm):
    t, d = x.shape
    tile = pl.BlockSpec((tm, d), lambda i: (i, 0))
    return pl.pallas_call(
        _add_body, grid=(t // tm,), in_specs=[tile, tile], out_specs=tile,
        out_shape=jax.ShapeDtypeStruct((t, d), F32),
        compiler_params=_cparams(1), name="residual_add")(x, delta)


ROUTE_I1, ROUTE_I2, ROUTE_G1, ROUTE_G2, ROUTE_R1, ROUTE_R2 = range(6)


def _norm_router_body(x_ref, g_ref, wr_ref, h_ref, route_ref, count_ref, carry_ref, *, n_exp):
    @pl.when(pl.program_id(0) == 0)
    def _():
        carry_ref[...] = jnp.zeros_like(carry_ref)

    x = x_ref[...]
    tm, d = x.shape
    r = lax.rsqrt(_sumsq_lanes(x) * (1.0 / d) + EPS)
    h = x * r * g_ref[...]
    h_ref[...] = h.astype(h_ref.dtype)
    logits = jnp.dot(h, wr_ref[...], preferred_element_type=F32, precision=_HI)
    lane = lax.broadcasted_iota(jnp.int32, logits.shape, 1).astype(F32)
    lg = jnp.where(lane < n_exp, logits, -jnp.inf)
    m1 = jnp.max(lg, axis=-1, keepdims=True)
    i1 = jnp.min(jnp.where(lg == m1, lane, float(LANES)), axis=-1, keepdims=True)
    lg2 = jnp.where(lane == i1, -jnp.inf, lg)
    m2 = jnp.max(lg2, axis=-1, keepdims=True)
    i2 = jnp.min(jnp.where(lg2 == m2, lane, float(LANES)), axis=-1, keepdims=True)
    e2 = jnp.exp(m2 - m1)
    g1 = 1.0 / (1.0 + e2)
    g2 = e2 * g1

    chosen = jnp.where((lane == i1) | (lane == i2), 1.0, 0.0)
    rr = lax.broadcasted_iota(jnp.int32, (tm, tm), 0)
    cc = lax.broadcasted_iota(jnp.int32, (tm, tm), 1)
    strict = jnp.where(rr > cc, 1.0, 0.0).astype(BF16)
    before = jnp.dot(strict, chosen.astype(BF16), preferred_element_type=F32) + carry_ref[...]
    rank1 = jnp.sum(jnp.where(lane == i1, before, 0.0), axis=-1, keepdims=True)
    rank2 = jnp.sum(jnp.where(lane == i2, before, 0.0), axis=-1, keepdims=True)
    carry_ref[...] += jnp.sum(chosen, axis=0, keepdims=True)
    count_ref[...] = carry_ref[...]

    rec = jnp.zeros((tm, LANES), F32)
    for k, v in ((ROUTE_I1, i1), (ROUTE_I2, i2), (ROUTE_G1, g1), (ROUTE_G2, g2), (ROUTE_R1, rank1), (ROUTE_R2, rank2)):
        rec = jnp.where(lane == float(k), v, rec)
    route_ref[...] = rec


def _rms_norm_router(x, gain, router, *, tm, name):
    t, d = x.shape
    n_exp = router.shape[1]
    wr = jnp.zeros((d, LANES), F32).at[:, :n_exp].set(router.astype(F32))
    return pl.pallas_call(
        functools.partial(_norm_router_body, n_exp=n_exp), grid=(t // tm,),
        in_specs=[pl.BlockSpec((tm, d), lambda i: (i, 0)), pl.BlockSpec((1, d), lambda i: (0, 0)),
                  pl.BlockSpec((d, LANES), lambda i: (0, 0))],
        out_specs=[pl.BlockSpec((tm, d), lambda i: (i, 0)),
                   pl.BlockSpec((tm, LANES), lambda i: (i, 0)),
                   pl.BlockSpec((1, LANES), lambda i: (0, 0))],
        out_shape=[jax.ShapeDtypeStruct((t, d), BF16),
                   jax.ShapeDtypeStruct((t, LANES), F32),
                   jax.ShapeDtypeStruct((1, LANES), F32)],
        scratch_shapes=[pltpu.VMEM((1, LANES), F32)],
        compiler_params=_cparams(1), name=name)(x, gain.reshape(1, d), wr)


def _row_view(a):
    rows, d = a.shape
    return a.reshape(rows, d // LANES, LANES)


def _dispatch_body(src_ref, h_ref, o_ref, sem, *, tm):
    def issue(pair, carry):
        for prio in range(2):
            r = 2 * pair + prio
            pltpu.make_async_copy(h_ref.at[src_ref[0, 0, r]], o_ref.at[r], sem).start(priority=prio)
        return carry

    def drain(r, carry):
        pltpu.make_async_copy(h_ref.at[0], o_ref.at[0], sem).wait()
        return carry

    lax.fori_loop(0, tm // 2, issue, 0)
    lax.fori_loop(0, tm, drain, 0)


def _dispatch(h_rows, src, *, tm):
    n_tiles = src.shape[0]
    chunks = h_rows.shape[1]
    return pl.pallas_call(
        functools.partial(_dispatch_body, tm=tm), grid=(n_tiles,),
        in_specs=[pl.BlockSpec((1, 1, tm), lambda i: (i, 0, 0), memory_space=pltpu.SMEM),
                  pl.BlockSpec(memory_space=pl.ANY)],
        out_specs=pl.BlockSpec((tm, chunks, LANES), lambda i: (i, 0, 0)),
        out_shape=jax.ShapeDtypeStruct((n_tiles * tm, chunks, LANES), h_rows.dtype),
        scratch_shapes=[pltpu.SemaphoreType.DMA(())],
        compiler_params=_cparams(1), name="moe_dispatch")(src, h_rows)


def _combine_body(pos_ref, gate_ref, ys_ref, o_ref, buf, sem, *, tm):
    def issue(r, carry):
        pltpu.make_async_copy(ys_ref.at[pos_ref[0, 0, r]], buf.at[0, r], sem).start(priority=0)
        pltpu.make_async_copy(ys_ref.at[pos_ref[0, 0, tm + r]], buf.at[1, r], sem).start(priority=1)
        return carry

    def drain(r, carry):
        pltpu.make_async_copy(ys_ref.at[0], buf.at[0, 0], sem).wait()
        pltpu.make_async_copy(ys_ref.at[0], buf.at[1, 0], sem).wait()
        return carry

    def mix(r, carry):
        o_ref[r] = (gate_ref[0, 0, r] * buf[0, r].astype(F32)
                    + gate_ref[0, 0, tm + r] * buf[1, r].astype(F32)).astype(o_ref.dtype)
        return carry

    lax.fori_loop(0, tm, issue, 0)
    lax.fori_loop(0, tm, drain, 0)
    lax.fori_loop(0, tm, mix, 0)


def _combine(ys_rows, pos, gates, *, tm):
    n_tiles = pos.shape[0]
    chunks = ys_rows.shape[1]
    smem_tile = pl.BlockSpec((1, 1, 2 * tm), lambda i: (i, 0, 0), memory_space=pltpu.SMEM)
    return pl.pallas_call(
        functools.partial(_combine_body, tm=tm), grid=(n_tiles,),
        in_specs=[smem_tile, smem_tile, pl.BlockSpec(memory_space=pl.ANY)],
        out_specs=pl.BlockSpec((tm, chunks, LANES), lambda i: (i, 0, 0)),
        out_shape=jax.ShapeDtypeStruct((n_tiles * tm, chunks, LANES), ys_rows.dtype),
        scratch_shapes=[pltpu.VMEM((2, tm, chunks, LANES), ys_rows.dtype), pltpu.SemaphoreType.DMA(())],
        compiler_params=_cparams(1), name="moe_combine")(pos, gates, ys_rows)


def _grouped_up_body(te_ref, nv_ref, a_ref, wg_ref, wu_ref, o_ref):
    @pl.when(pl.program_id(1) < nv_ref[0])
    def _():
        a = a_ref[...]
        g = jnp.dot(a, wg_ref[...], preferred_element_type=F32)
        u = jnp.dot(a, wu_ref[...], preferred_element_type=F32)
        o_ref[...] = (g * (1.0 / (1.0 + jnp.exp(-g))) * u).astype(o_ref.dtype)

    @pl.when(pl.program_id(1) >= nv_ref[0])
    def _():
        o_ref[...] = jnp.zeros_like(o_ref)


def _grouped_down_body(te_ref, nv_ref, a_ref, w_ref, o_ref):
    @pl.when(pl.program_id(1) < nv_ref[0])
    def _():
        o_ref[...] = jnp.dot(a_ref[...], w_ref[...], preferred_element_type=F32).astype(o_ref.dtype)

    @pl.when(pl.program_id(1) >= nv_ref[0])
    def _():
        o_ref[...] = jnp.zeros_like(o_ref)


def _grouped_matmul(body, a, weights, tile_expert, n_valid, *, tmo, tn, name):
    r, k = a.shape
    n = weights[0].shape[2]

    def row(j, i, te, nv):
        return jnp.minimum(i, nv[0] - 1)

    grid_spec = pltpu.PrefetchScalarGridSpec(
        num_scalar_prefetch=2, grid=(n // tn, r // tmo),
        in_specs=[pl.BlockSpec((tmo, k), lambda j, i, te, nv: (row(j, i, te, nv), 0))]
        + [pl.BlockSpec((None, k, tn), lambda j, i, te, nv: (te[row(j, i, te, nv)], 0, j)) for _ in weights],
        out_specs=pl.BlockSpec((tmo, tn), lambda j, i, te, nv: (i, j)))
    return pl.pallas_call(
        body, grid_spec=grid_spec, out_shape=jax.ShapeDtypeStruct((r, n), BF16),
        compiler_params=_cparams(2), name=name)(tile_expert, n_valid, a, *weights)


def _rope_body(pos_ref, f_ref, cos_ref, sin_ref):
    ang = pos_ref[...] * f_ref[...]
    live = f_ref[...] > 0.0
    cos_ref[...] = jnp.where(live, jnp.cos(ang), 0.0)
    sin_ref[...] = jnp.where(live, jnp.sin(ang), 0.0)


def _rope_tables(positions, *, half, tm):
    t = positions.size
    inv_freq = ROPE_BASE ** (-jnp.arange(half, dtype=F32) / half)
    f_row = jnp.concatenate([inv_freq, inv_freq, jnp.zeros((LANES - 2 * half,), F32)]).reshape(1, LANES)
    pos_rep = jnp.broadcast_to(positions.astype(F32).reshape(t, 1), (t, LANES))
    spec = pl.BlockSpec((tm, LANES), lambda i: (i, 0))
    return pl.pallas_call(
        _rope_body, grid=(t // tm,),
        in_specs=[spec, pl.BlockSpec((1, LANES), lambda i: (0, 0))],
        out_specs=[spec, spec],
        out_shape=[jax.ShapeDtypeStruct((t, LANES), F32)] * 2,
        compiler_params=_cparams(1), name="rope_tables")(pos_rep, f_row)


def _bias_body(rb_ref, o_ref, *, t):
    h = pl.program_id(0)
    d = pl.program_id(1)
    j = lax.broadcasted_iota(jnp.int32, (t, t), 0)
    i = lax.broadcasted_iota(jnp.int32, (t, t), 1)
    rel = d * t + i - j
    n = jnp.maximum(rel, 0)
    max_exact = NUM_BUCKETS // 2
    large = max_exact + (
        jnp.log(jnp.maximum(n, max_exact).astype(F32) / max_exact)
        / math.log(MAX_DISTANCE / max_exact) * (NUM_BUCKETS - max_exact)
    ).astype(jnp.int32)
    large = jnp.minimum(large, NUM_BUCKETS - 1)
    bucket = jnp.where(n < max_exact, n, large)
    far = rb_ref[NUM_BUCKETS - 1, h]
    val = jnp.zeros((t, t), F32)
    for b in range(NUM_BUCKETS - 1):
        val = jnp.where(bucket == b, rb_ref[b, h] - far, val)
    o_ref[...] = jnp.where(rel >= 0, val * LOG2E, NEG)


def _bias_tiles(rel_bias, *, t):
    assert t + 1 >= MAX_DISTANCE, "tiles beyond the first sub-diagonal must be in the last bucket"
    nh = rel_bias.shape[1]
    return pl.pallas_call(
        functools.partial(_bias_body, t=t), grid=(nh, 2),
        in_specs=[pl.BlockSpec(memory_space=pltpu.SMEM)],
        out_specs=pl.BlockSpec((None, None, t, t), lambda h, d: (h, d, 0, 0)),
        out_shape=jax.ShapeDtypeStruct((nh, 2, t, t), F32),
        compiler_params=_cparams(2), name="t5_bias_tiles")(rel_bias.astype(F32))


N_PIECES = 3


def _split_pieces(x):
    pieces = []
    rest = x
    for _ in range(N_PIECES):
        p = rest.astype(BF16).astype(F32)
        pieces.append(p)
        rest = rest - p
    return pieces


def _cum_body(shift_ref, lf_ref, o_ref, end_ref, carry_ref, *, n_heads, lane0):
    @pl.when(pl.program_id(1) == 0)
    def _():
        carry_ref[...] = jnp.zeros_like(carry_ref)

    x = lf_ref[...]
    tb = x.shape[0]
    r = lax.broadcasted_iota(jnp.int32, (tb, tb), 0)
    c = lax.broadcasted_iota(jnp.int32, (tb, tb), 1)
    tri = jnp.where(r >= c, 1.0, 0.0).astype(F32)
    cum = jnp.dot(tri, x, preferred_element_type=F32, precision=_HI) + carry_ref[...]
    carry_ref[...] = cum[tb - 1:tb, :]
    cum2 = cum * LOG2E
    end_ref[0] = cum2[tb - 1:tb, :]
    lane = lax.broadcasted_iota(jnp.int32, (tb, LANES), 1)
    shift = shift_ref[0]
    for hh in range(n_heads):
        col = cum2[:, lane0 + hh:lane0 + hh + 1]
        qcols = jnp.where(lane < N_PIECES, 1.0, 0.0)
        kcols = jnp.where((lane >= N_PIECES) & (lane < 2 * N_PIECES), 1.0, 0.0)
        for n, (pq, pk) in enumerate(zip(_split_pieces(col - shift), _split_pieces(-col))):
            qcols = jnp.where(lane == N_PIECES + n, pq, qcols)
            kcols = jnp.where(lane == n, pk, kcols)
        o_ref[:, hh * LANES:(hh + 1) * LANES] = qcols.astype(o_ref.dtype)
        o_ref[:, (n_heads + hh) * LANES:(n_heads + hh + 1) * LANES] = kcols.astype(o_ref.dtype)


def _forget_columns(logf, shift, *, batch, seq, n_heads, lane0, tb):
    nb = seq // tb
    width = 2 * n_heads * LANES
    return pl.pallas_call(
        functools.partial(_cum_body, n_heads=n_heads, lane0=lane0), grid=(batch, nb),
        in_specs=[pl.BlockSpec(memory_space=pltpu.SMEM),
                  pl.BlockSpec((tb, LANES), lambda b, i: (b * nb + i, 0))],
        out_specs=[pl.BlockSpec((tb, width), lambda b, i: (b * nb + i, 0)),
                   pl.BlockSpec((1, 1, LANES), lambda b, i: (b * nb + i, 0, 0))],
        out_shape=[jax.ShapeDtypeStruct((batch * seq, width), BF16),
                   jax.ShapeDtypeStruct((batch * nb, 1, LANES), F32)],
        scratch_shapes=[pltpu.VMEM((1, LANES), F32)],
        compiler_params=_cparams(2), name="forget_cumsum")(shift.reshape(1).astype(F32), logf)


def _online_block(s, vb, m_ref, acc_ref):
    m_prev = m_ref[...]
    m_new = jnp.maximum(m_prev, jnp.max(s, axis=0, keepdims=True))
    alpha = jnp.exp2(m_prev - m_new)
    p = jnp.exp2(s - m_new).astype(BF16)
    acc_ref[...] = alpha * acc_ref[...] + jnp.dot(vb, p, preferred_element_type=F32)
    m_ref[...] = m_new


def _shifted_block(s, vb, acc_ref):
    p = jnp.exp2(s).astype(BF16)
    acc_ref[...] += jnp.dot(vb, p, preferred_element_type=F32)


def _causal_mask(t):
    j = lax.broadcasted_iota(jnp.int32, (t, t), 0)
    i = lax.broadcasted_iota(jnp.int32, (t, t), 1)
    return j <= i


_NT = (((1,), (1,)), ((), ()))
FAR_UNROLL = 4


def _init_state(m_ref, acc_ref):
    m_ref[...] = jnp.full(m_ref.shape, NEG, F32)
    acc_ref[...] = jnp.zeros(acc_ref.shape, F32)


def _attn_body(*refs, mode, t, hp, dq, dv, qk_dim=None, lambda_init=None):
    safe_ref, start_ref, q_ref, k_ref, vt_ref = refs[:5]
    if mode == "latent":
        o_ref, qc_ref, m_ref, acc_ref = refs[5:]
    elif mode == "forget":
        qx_ref, kx_ref, o_ref, qc_ref, m_ref, acc_ref = refs[5:]
    else:
        qx_ref, kx_ref, bias_ref, lam_ref, hn_ref, o_ref, qc_ref, m_ref, acc_ref = refs[5:]
    n_map = 2 if mode == "diff" else 1
    qi = pl.program_id(2)
    step = (pl.program_id(0) * pl.num_programs(1) + pl.program_id(1)) * pl.num_programs(2) + qi
    _init_state(m_ref, acc_ref)

    for hh in range(hp):
        q = q_ref[:, hh * dq:(hh + 1) * dq]
        if mode == "latent":
            qc_ref[hh] = q
        elif mode == "forget":
            qc_ref[hh] = jnp.concatenate([q, qx_ref[:, hh * LANES:(hh + 1) * LANES]], axis=1)
        else:
            lane = lax.broadcasted_iota(jnp.int32, (t, dq), 1)
            zero = jnp.zeros_like(q)
            qc_ref[2 * hh] = jnp.concatenate([jnp.where(lane < qk_dim, q, zero), qx_ref[...]], axis=1)
            qc_ref[2 * hh + 1] = jnp.concatenate([jnp.where(lane < qk_dim, zero, q), qx_ref[...]], axis=1)

    def keys(kj, hh):
        start = pl.multiple_of(kj * t, t)
        kb = k_ref[pl.ds(start, t), hh * dq:(hh + 1) * dq]
        if mode == "forget":
            kb = jnp.concatenate([kb, kx_ref[pl.ds(start, t), hh * LANES:(hh + 1) * LANES]], axis=1)
        elif mode == "diff":
            kb = jnp.concatenate([kb, kx_ref[hh]], axis=1)
        return kb

    def run(update, skip_dead):
        def scores(kj, kind, hh, mm):
            s = lax.dot_general(keys(kj, hh), qc_ref[n_map * hh + mm], _NT, preferred_element_type=F32)
            if mode == "diff":
                if kind != "far":
                    s = s + bias_ref[hh, 0 if kind == "diag" else 1]
            elif kind == "diag":
                s = jnp.where(_causal_mask(t), s, NEG)
            return s

        def blocks(items):
            work = [(kj, kind, hh, mm) for kj, kind in items for hh in range(hp) for mm in range(n_map)]
            s_next = scores(*work[0])
            for n, (kj, kind, hh, mm) in enumerate(work):
                s_cur = s_next
                if n + 1 < len(work):
                    s_next = scores(*work[n + 1])
                update(s_cur, vt_ref[hh, kj], n_map * hh + mm)

        n_far = jnp.maximum(qi - 1, 0) if mode == "diff" else qi
        k0 = jnp.minimum(start_ref[step], n_far) if skip_dead else 0
        near = [(qi, "diag")] + ([(qi - 1, "sub")] if mode == "diff" else [])
        n_tail = FAR_UNROLL - len(near)
        if skip_dead:
            merge = (n_far - k0 >= n_tail) & (qi >= len(near) - 1)
        else:
            merge = qi < 0
        n_loop = jnp.where(merge, n_far - k0 - n_tail, n_far - k0)
        n_groups = n_loop // FAR_UNROLL

        def far_group(g, carry):
            blocks([(k0 + g * FAR_UNROLL + u, "far") for u in range(FAR_UNROLL)])
            return carry

        def far_single(kj, carry):
            blocks([(kj, "far")])
            return carry

        lax.fori_loop(0, n_groups, far_group, 0)
        lax.fori_loop(k0 + n_groups * FAR_UNROLL, k0 + n_loop, far_single, 0)

        if skip_dead:
            @pl.when(merge)
            def _():
                blocks([(n_far - 1 - u, "far") for u in range(n_tail)] + near[::-1])

        @pl.when(jnp.logical_not(merge))
        def _():
            if mode == "diff":
                @pl.when(qi >= 1)
                def _():
                    blocks([(qi - 1, "sub")])
            blocks([(qi, "diag")])

    @pl.when(safe_ref[0] != 0)
    def _():
        run(lambda s, vb, c: _shifted_block(s, vb, acc_ref.at[c]), True)

    @pl.when(safe_ref[0] == 0)
    def _():
        run(lambda s, vb, c: _online_block(s, vb, m_ref.at[c], acc_ref.at[c]), False)

    if mode == "diff":
        lp = lam_ref[...]
        lam = (jnp.exp(jnp.sum(lp[0:1] * lp[1:2], axis=-1, keepdims=True))
               - jnp.exp(jnp.sum(lp[2:3] * lp[3:4], axis=-1, keepdims=True)) + lambda_init)
    for hh in range(hp):
        if mode == "diff":
            a0 = acc_ref[2 * hh]
            a1 = acc_ref[2 * hh + 1]
            o = a0[:dv] * (1.0 / a0[dv:dv + 1]) - lam * (a1[:dv] * (1.0 / a1[dv:dv + 1]))
            r = lax.rsqrt(jnp.sum(o * o, axis=0, keepdims=True) * (1.0 / dv) + EPS)
            o = o * r * hn_ref[...] * (1.0 - lambda_init)
        else:
            a = acc_ref[hh]
            o = a[:dv] * (1.0 / a[dv:dv + 1])
        o_ref[:, hh * dv:(hh + 1) * dv] = o.T.astype(o_ref.dtype)


def _attention(mode, safe, q_arr, k_arr, vt_arr, *, batch, seq, n_heads, t, hp, dq, q_col0, k_col0,
               extra=(), extra_specs=(), body_kwargs=None, first_block=None, name):
    nq = seq // t
    dvp = vt_arr.shape[2]
    dv = dvp - ONES_ROWS
    tokens = batch * seq
    assert n_heads % hp == 0 and q_col0 % hp == 0 and k_col0 % hp == 0
    qc, kc = q_col0 // hp, k_col0 // hp
    if first_block is None:
        first_block = jnp.zeros((batch * (n_heads // hp) * nq,), jnp.int32)
    in_specs = [
        pl.BlockSpec(memory_space=pltpu.SMEM),
        pl.BlockSpec(memory_space=pltpu.SMEM),
        pl.BlockSpec((t, hp * dq), lambda b, h, i: (b * nq + i, qc + h)),
        pl.BlockSpec((seq, hp * dq), lambda b, h, i: (b, kc + h)),
        pl.BlockSpec((hp, nq, dvp, t), lambda b, h, i: (h, b, 0, 0)),
        *extra_specs,
    ]
    n_chain = 2 * hp if mode == "diff" else hp
    scratch = [pltpu.VMEM((n_chain, t, 2 * LANES), BF16),
               pltpu.VMEM((n_chain, 1, t), F32), pltpu.VMEM((n_chain, dvp, t), F32)]
    body = functools.partial(_attn_body, mode=mode, t=t, hp=hp, dq=dq, dv=dv, **(body_kwargs or {}))
    return pl.pallas_call(
        body, grid=(batch, n_heads // hp, nq), in_specs=in_specs,
        out_specs=pl.BlockSpec((t, hp * dv), lambda b, h, i: (b * nq + i, h)),
        out_shape=jax.ShapeDtypeStruct((tokens, n_heads * dv), BF16),
        scratch_shapes=scratch, compiler_params=_cparams(3), name=name)(
            safe, first_block, q_arr, k_arr, vt_arr, *extra)


def _pick(n, prefs):
    for p in prefs:
        if n % p == 0:
            return p
    raise ValueError((n, prefs))


SKIP_LOG2 = 160.0
SAFE_DEPTH = 110.0


def _round_up_bf16(c):
    return (c * (1.0 + 2.0 ** -6)).astype(BF16).astype(F32)


def _safe_flag(depth):
    return (depth <= SAFE_DEPTH).astype(jnp.int32).reshape(1)


def kernel(x, positions, attn_norm, w_in, b_forget, a_qk_norm, a_lambda, a_head_norm, rel_bias,
           b_qk_norm, c_q_a_norm, c_kv_a_norm, c_w_uq, c_w_ukv, c_qk_norm_nope, c_qk_norm_rope,
           w_out, ffn_norm, dense_w_gate, dense_w_up, dense_w_down, moe_router, moe_w_gate,
           moe_w_up, moe_w_down):
    batch, seq, d_model = x.shape
    depth = w_in.shape[0]
    tokens = batch * seq
    n_slots = d_model // LANES
    a_heads = n_slots // 4
    b_heads = n_slots // 4
    c_heads = n_slots // 2
    a_qk = a_qk_norm.shape[-1]
    b_dim = b_qk_norm.shape[-1]
    c_q_lora = c_q_a_norm.shape[-1]
    c_kv_lora = c_kv_a_norm.shape[-1]
    c_nope = c_qk_norm_nope.shape[-1]
    c_rope = c_qk_norm_rope.shape[-1]
    c_v = c_w_ukv.shape[-1] // c_heads - c_nope
    n_exp = moe_router.shape[-1]
    assert 2 * a_qk == LANES and b_dim == LANES and c_nope == LANES and c_v == LANES and 2 * c_rope == LANES

    t = _pick(seq, (512, 256, 128))
    tm = _pick(tokens, (1024, 512, 256))
    tm_n = _pick(tokens, (256, 128))
    hp_a, hp_b, hp_c = 1, 2, 2

    sizes = (a_heads * 2 * a_qk, a_heads * 2 * a_qk, a_heads * LANES, b_heads * b_dim, b_heads * b_dim,
             b_heads * b_dim, b_heads, c_q_lora, c_kv_lora, c_rope)
    offs = [0]
    for s in sizes:
        offs.append(offs[-1] + s)
    o_aq, o_ak, o_av, o_bq, o_bk, o_bv, o_f, o_cq, o_ckv, o_kpe, o_end = offs

    xf = x.reshape(tokens, d_model)
    cos_t, sin_t = _rope_tables(positions, half=c_rope // 2, tm=tm)
    bias_tiles = _bias_tiles(rel_bias, t=t)
    bias_log2 = (rel_bias.astype(F32) - rel_bias[-1:].astype(F32)) * LOG2E
    bias_max = jnp.max(bias_log2, axis=0)
    qx_a = jnp.zeros((t, LANES), F32).at[:, 0].set(1.0).astype(BF16)

    row_spec = lambda width: pl.BlockSpec((1, width), lambda i, j: (0, j))
    fixed_row = pl.BlockSpec((1, LANES), lambda i, j: (0, 0))
    tok_lane = pl.BlockSpec((tm, LANES), lambda i, j: (i, 0))

    def vt_call(h_bf, w, nh, name, a_cols=None, w_cols=None):
        tn = _pick(nh * LANES, (512, 256, 128))
        return _matmul(
            h_bf, w, a_cols=a_cols, w_cols=w_cols, tm=tm, tn=tn,
            epilogue=functools.partial(_ep_transpose_heads, t=t),
            out_shape=jax.ShapeDtypeStruct((nh, tokens // t, LANES + ONES_ROWS, t), BF16),
            out_specs=pl.BlockSpec((tn // LANES, tm // t, LANES + ONES_ROWS, t), lambda i, j: (j, i, 0, 0)),
            name=name)

    pending = None
    for layer in range(depth):
        lambda_init = 0.8 - 0.6 * math.exp(-0.3 * layer)
        w = w_in[layer].astype(BF16)

        if pending is None:
            h = _rms_norm(xf, attn_norm[layer], tm=tm_n, name="attn_norm")
        else:
            xf, h = _add_rms_norm(xf, pending, attn_norm[layer], tm=tm_n, name="attn_norm")
            pending = None

        a_scale = a_qk ** -0.5 * LOG2E
        gain_a = jnp.concatenate([jnp.tile(a_qk_norm[layer, 0] * a_scale, 2 * a_heads),
                                  jnp.tile(a_qk_norm[layer, 1], 2 * a_heads)]).reshape(1, -1)
        tn = 512
        qk_a = _matmul(h, w, w_cols=(o_aq, o_av - o_aq), tm=tm, tn=tn,
                       epilogue=functools.partial(_ep_group_norm, group=a_qk),
                       aux=(gain_a,), aux_specs=(row_spec(tn),),
                       out_shape=jax.ShapeDtypeStruct((tokens, o_av - o_aq), BF16),
                       out_specs=pl.BlockSpec((tm, tn), lambda i, j: (i, j)), name="proj_a_qk")
        vt_a = vt_call(h, w, a_heads, "proj_a_v", w_cols=(o_av, o_bq - o_av))
        hn = jnp.broadcast_to(a_head_norm[layer].reshape(LANES, 1), (LANES, t))
        bound_a = a_qk * jnp.max(jnp.abs(a_qk_norm[layer, 0] * a_scale)) * jnp.max(jnp.abs(a_qk_norm[layer, 1]))
        shift_a = _round_up_bf16(bound_a + bias_max)
        safe_a = _safe_flag(2.0 * bound_a + jnp.max(bias_max - bias_log2[0]))
        kx_a = jnp.zeros((a_heads, t, LANES), F32).at[:, :, 0].set(-shift_a[:, None]).astype(BF16)
        o_a = _attention(
            "diff", safe_a, qk_a, qk_a, vt_a, batch=batch, seq=seq, n_heads=a_heads, t=t, hp=hp_a, dq=LANES,
            q_col0=0, k_col0=a_heads,
            extra=(qx_a, kx_a, bias_tiles, a_lambda[layer], hn),
            extra_specs=(pl.BlockSpec((t, LANES), lambda b, hh, i: (0, 0)),
                         pl.BlockSpec((hp_a, t, LANES), lambda b, hh, i: (hh, 0, 0)),
                         pl.BlockSpec((hp_a, 2, t, t), lambda b, hh, i: (hh, 0, 0, 0)),
                         pl.BlockSpec((4, a_qk), lambda b, hh, i: (0, 0)),
                         pl.BlockSpec((LANES, t), lambda b, hh, i: (0, 0))),
            body_kwargs=dict(qk_dim=a_qk, lambda_init=lambda_init), name="attn_diff")

        b_scale = b_dim ** -0.5 * LOG2E
        gain_b = jnp.concatenate([jnp.tile(b_qk_norm[layer, 0] * b_scale, b_heads),
                                  jnp.tile(b_qk_norm[layer, 1], b_heads)]).reshape(1, -1)
        qk_b = _matmul(h, w, w_cols=(o_bq, o_bv - o_bq), tm=tm, tn=tn,
                       epilogue=functools.partial(_ep_group_norm, group=b_dim),
                       aux=(gain_b,), aux_specs=(row_spec(tn),),
                       out_shape=jax.ShapeDtypeStruct((tokens, o_bv - o_bq), BF16),
                       out_specs=pl.BlockSpec((tm, tn), lambda i, j: (i, j)), name="proj_b_qk")
        vt_b = vt_call(h, w, b_heads, "proj_b_v", w_cols=(o_bv, o_f - o_bv))

        w_tail = jnp.concatenate([w[:, o_kpe:o_end], w[:, o_f:o_cq],
                                  jnp.zeros((d_model, LANES - c_rope - b_heads), BF16)], axis=1)
        gain_tail = jnp.concatenate([c_qk_norm_rope[layer, 1], jnp.zeros((LANES - c_rope,), F32)]).reshape(1, LANES)
        bias_tail = jnp.zeros((LANES,), F32).at[c_rope:c_rope + b_heads].set(b_forget[layer]).reshape(1, LANES)
        c_scale = (c_nope + c_rope) ** -0.5 * LOG2E
        gq_n = c_qk_norm_nope[layer, 0] * c_scale
        gq_r = c_qk_norm_rope[layer, 0] * c_scale
        gk_n = c_qk_norm_nope[layer, 1]
        gk_r = c_qk_norm_rope[layer, 1]
        bound_c = (jnp.sqrt(c_nope * jnp.max(gq_n * gq_n) + c_rope * jnp.max(gq_r * gq_r))
                   * jnp.sqrt(c_nope * jnp.max(gk_n * gk_n) + c_rope * jnp.max(gk_r * gk_r)))
        shift_c = _round_up_bf16(bound_c)
        safe_c = _safe_flag(2.0 * bound_c)
        shift_col = jnp.zeros((1, LANES), F32).at[0, c_rope].set(-shift_c)
        one_col = jnp.zeros((1, LANES), F32).at[0, c_rope].set(1.0)
        kpe, logf = _matmul(
            h, w_tail, tm=tm, tn=LANES,
            epilogue=functools.partial(_ep_tail, rope_dim=c_rope, n_gate=b_heads),
            aux=(gain_tail, bias_tail, cos_t, sin_t, shift_col),
            aux_specs=(fixed_row, fixed_row, tok_lane, tok_lane, fixed_row),
            out_shape=[jax.ShapeDtypeStruct((tokens, LANES), BF16), jax.ShapeDtypeStruct((tokens, LANES), F32)],
            out_specs=[tok_lane, tok_lane], name="proj_tail")
        bound_b = b_dim * jnp.max(jnp.abs(b_qk_norm[layer, 0] * b_scale)) * jnp.max(jnp.abs(b_qk_norm[layer, 1]))
        safe_b = _safe_flag(2.0 * bound_b)
        x_b, ends = _forget_columns(logf, bound_b, batch=batch, seq=seq, n_heads=b_heads, lane0=c_rope, tb=t)
        nq = seq // t
        nbh = b_heads // hp_b
        e_b = ends[:, 0, c_rope:c_rope + b_heads].reshape(batch, nq, b_heads)
        e_q = jnp.concatenate([jnp.zeros((batch, 1, b_heads), F32), e_b[:, :-1]], axis=1)
        dead = (e_q[:, :, None, :] - e_b[:, None, :, :]) < -SKIP_LOG2
        dead = dead & (jnp.arange(nq)[None, None, :, None] + 1 < jnp.arange(nq)[None, :, None, None])
        first_b = jnp.sum(dead.astype(jnp.int32), axis=2)
        first_b = jnp.min(first_b.reshape(batch, nq, nbh, hp_b), axis=3)
        first_b = jnp.transpose(first_b, (0, 2, 1)).reshape(-1)
        o_b = _attention(
            "forget", safe_b, qk_b, qk_b, vt_b, batch=batch, seq=seq, n_heads=b_heads, t=t, hp=hp_b, dq=LANES,
            q_col0=0, k_col0=b_heads,
            extra=(x_b, x_b),
            extra_specs=(pl.BlockSpec((t, hp_b * LANES), lambda b, hh, i: (b * nq + i, hh)),
                         pl.BlockSpec((seq, hp_b * LANES), lambda b, hh, i: (b, nbh + hh))),
            first_block=first_b, name="attn_forget")

        n_lat = c_q_lora + c_kv_lora
        gain_lat = jnp.concatenate([c_q_a_norm[layer], c_kv_a_norm[layer]]).reshape(1, n_lat)
        tm_lat = _pick(tokens, (512, 256))
        lat = _matmul(h, w[:, o_cq:o_kpe], tm=tm_lat, tn=n_lat,
                      epilogue=functools.partial(_ep_latent_norms, n_q=c_q_lora),
                      aux=(gain_lat,), aux_specs=(row_spec(n_lat),),
                      out_shape=jax.ShapeDtypeStruct((tokens, n_lat), BF16),
                      out_specs=pl.BlockSpec((tm_lat, n_lat), lambda i, j: (i, j)), name="proj_c_latent")
        wq = c_w_uq[layer].astype(BF16).reshape(c_q_lora, c_heads, c_nope + c_rope)
        wq = jnp.concatenate([wq, jnp.zeros((c_q_lora, c_heads, 2 * LANES - c_nope - c_rope), BF16)], axis=2)
        wq = wq.reshape(c_q_lora, c_heads * 2 * LANES)
        gq_nope = gq_n.reshape(1, LANES)
        gq_rope = jnp.concatenate([gq_r, jnp.zeros((LANES - c_rope,), F32)]).reshape(1, LANES)
        tn_q = 512
        lat_q_cols, lat_kv_cols = (0, c_q_lora), (c_q_lora, c_kv_lora)
        q_c = _matmul(lat, wq, a_cols=lat_q_cols, tm=tm, tn=tn_q, epilogue=_ep_latent_q,
                      aux=(gq_nope, gq_rope, cos_t, sin_t, one_col),
                      aux_specs=(fixed_row, fixed_row, tok_lane, tok_lane, fixed_row),
                      out_shape=jax.ShapeDtypeStruct((tokens, c_heads * 2 * LANES), BF16),
                      out_specs=pl.BlockSpec((tm, tn_q), lambda i, j: (i, j)), name="proj_c_q")
        wkv = c_w_ukv[layer].astype(BF16).reshape(c_kv_lora, c_heads, c_nope + c_v)
        wk = wkv[:, :, :c_nope].reshape(c_kv_lora, c_heads * c_nope)
        wv = wkv[:, :, c_nope:].reshape(c_kv_lora, c_heads * c_v)
        gk_nope = c_qk_norm_nope[layer, 1].reshape(1, LANES)
        tn_k = 256
        k_c = _matmul(lat, wk, a_cols=lat_kv_cols, tm=tm, tn=tn_k, epilogue=_ep_latent_k,
                      aux=(gk_nope, kpe), aux_specs=(fixed_row, tok_lane),
                      out_shape=jax.ShapeDtypeStruct((tokens, c_heads * 2 * LANES), BF16),
                      out_specs=pl.BlockSpec((tm, 2 * tn_k), lambda i, j: (i, j)), name="proj_c_k")
        vt_c = vt_call(lat, wv, c_heads, "proj_c_v", a_cols=lat_kv_cols)
        o_c = _attention(
            "latent", safe_c, q_c, k_c, vt_c, batch=batch, seq=seq, n_heads=c_heads, t=t, hp=hp_c, dq=2 * LANES,
            q_col0=0, k_col0=0, name="attn_latent")

        tn_o = 512
        xf = _out_proj((o_a, o_b, o_c), w_out[layer].astype(BF16), xf, tm=tm, tn=tn_o)

        idx = layer // 2
        th = 256
        if layer % 2 == 0:
            h2 = _rms_norm(xf, ffn_norm[layer], tm=tm_n, name="ffn_norm")
            act = _swiglu_up(h2, dense_w_gate[idx].astype(BF16), dense_w_up[idx].astype(BF16),
                             tm=tm, th=th, name="ffn_up")
            tm_d = _pick(tokens, (512, 256))
            res_d = pl.BlockSpec((tm_d, tn_o), lambda i, j: (i, j))
            xf = _matmul(act, dense_w_down[idx].astype(BF16), tm=tm_d, tn=tn_o, epilogue=_ep_residual,
                         aux=(xf,), aux_specs=(res_d,),
                         out_shape=jax.ShapeDtypeStruct((tokens, d_model), F32), out_specs=res_d,
                         name="ffn_down")
        else:
            h2, route, counts = _rms_norm_router(xf, ffn_norm[layer], moe_router[idx], tm=tm_n,
                                                 name="ffn_norm_router")
            tmo = _pick(tokens, (512, 256, 128))
            n_rows = TOP_K * tokens + n_exp * tmo
            cnt = counts[0, :n_exp].astype(jnp.int32)
            padded = (cnt + tmo - 1) // tmo * tmo
            ends = jnp.cumsum(padded)
            starts = ends - padded
            n_valid = (ends[-1] // tmo).reshape(1).astype(jnp.int32)
            tile_expert = jnp.minimum(
                jnp.searchsorted(ends, jnp.arange(n_rows // tmo, dtype=jnp.int32) * tmo, side="right"),
                n_exp - 1).astype(jnp.int32)
            e1 = route[:, ROUTE_I1].astype(jnp.int32)
            e2 = route[:, ROUTE_I2].astype(jnp.int32)
            pos1 = starts[e1] + route[:, ROUTE_R1].astype(jnp.int32)
            pos2 = starts[e2] + route[:, ROUTE_R2].astype(jnp.int32)
            tm_r = _pick(tokens, (512, 256, 128))
            per_tile = lambda a, b: jnp.concatenate(
                [a.reshape(tokens // tm_r, 1, tm_r), b.reshape(tokens // tm_r, 1, tm_r)], axis=2)
            pos = per_tile(pos1, pos2)
            gates = per_tile(route[:, ROUTE_G1], route[:, ROUTE_G2])

            tok = jnp.arange(tokens, dtype=jnp.int32)
            src = jnp.zeros((n_rows,), jnp.int32).at[jnp.concatenate([pos1, pos2])].set(
                jnp.concatenate([tok, tok]), unique_indices=True)
            xs = _dispatch(_row_view(h2), src.reshape(n_rows // tm_r, 1, tm_r), tm=tm_r).reshape(n_rows, d_model)
            act = _grouped_matmul(_grouped_up_body, xs,
                                  (moe_w_gate[idx].astype(BF16), moe_w_up[idx].astype(BF16)),
                                  tile_expert, n_valid, tmo=tmo, tn=512, name="moe_up")
            ys = _grouped_matmul(_grouped_down_body, act, (moe_w_down[idx].astype(BF16),),
                                 tile_expert, n_valid, tmo=tmo, tn=1024, name="moe_down")
            pending = _combine(_row_view(ys), pos, gates, tm=tm_r).reshape(tokens, d_model)

    if pending is not None:
        xf = _residual_add(xf, pending, tm=tm_n)
    return xf.reshape(batch, seq, d_model)
```

```python
import functools
import math

import jax
import jax.numpy as jnp
from jax import lax
from jax.experimental import pallas as pl
from jax.experimental.pallas import tpu as pltpu

F32 = jnp.float32
BF16 = jnp.bfloat16

LANES = 128
ONES_ROWS = 16
LOG2E = 1.4426950408889634
NEG = -1e30
EPS = 1e-6
VMEM_LIMIT = 52 * 1024 * 1024

NUM_BUCKETS = 32
MAX_DISTANCE = 128
ROPE_BASE = 10000.0
TOP_K = 2

_HI = lax.Precision.HIGHEST


def _cparams(n_axes):
    return pltpu.CompilerParams(
        dimension_semantics=("arbitrary",) * n_axes, vmem_limit_bytes=VMEM_LIMIT)


def _mm_body(a_ref, w_ref, *rest, epilogue, n_aux):
    acc = jnp.dot(a_ref[...], w_ref[...], preferred_element_type=F32)
    epilogue(acc, rest[:n_aux], rest[n_aux:])


def _matmul(a, w, *, tm, tn, epilogue, out_shape, out_specs, aux=(), aux_specs=(), name,
            a_cols=None, w_cols=None):
    m = a.shape[0]
    a0, k = a_cols if a_cols is not None else (0, a.shape[1])
    w0, n = w_cols if w_cols is not None else (0, w.shape[1])
    if a0 % k:
        a, a0 = a[:, a0:a0 + k], 0
    if w0 % tn:
        w, w0 = w[:, w0:w0 + n], 0
    assert k == w.shape[0]
    assert m % tm == 0 and n % tn == 0, (m, n, tm, tn)
    ab, wb = a0 // k, w0 // tn
    return pl.pallas_call(
        functools.partial(_mm_body, epilogue=epilogue, n_aux=len(aux)),
        grid=(m // tm, n // tn),
        in_specs=[pl.BlockSpec((tm, k), lambda i, j: (i, ab)),
                  pl.BlockSpec((k, tn), lambda i, j: (0, wb + j)), *aux_specs],
        out_specs=out_specs, out_shape=out_shape,
        compiler_params=_cparams(2), name=name)(a, w, *aux)


def _sumsq_lanes(x):
    return jnp.sum(x * x, axis=-1, keepdims=True)


def _group_ones(group):
    r = lax.broadcasted_iota(jnp.int32, (LANES, LANES), 0)
    c = lax.broadcasted_iota(jnp.int32, (LANES, LANES), 1)
    same = (r < group) == (c < group) if group < LANES else (r >= 0)
    return jnp.where(same, 1.0, 0.0).astype(BF16)


def _group_sumsq(x, ones):
    return jnp.dot((x * x).astype(BF16), ones, preferred_element_type=F32)


def _ep_residual(acc, aux, outs):
    outs[0][...] = aux[0][...] + acc


def _out_proj_body(*refs, widths):
    part_refs = refs[:len(widths)]
    w_ref, x_ref, o_ref = refs[len(widths):]
    acc = x_ref[...]
    row = 0
    for p_ref, width in zip(part_refs, widths):
        acc = acc + jnp.dot(p_ref[...], w_ref[row:row + width, :], preferred_element_type=F32)
        row += width
    o_ref[...] = acc


def _out_proj(parts, w, x, *, tm, tn):
    m, n = x.shape
    widths = tuple(p.shape[1] for p in parts)
    res = pl.BlockSpec((tm, tn), lambda i, j: (i, j))
    return pl.pallas_call(
        functools.partial(_out_proj_body, widths=widths), grid=(m // tm, n // tn),
        in_specs=[pl.BlockSpec((tm, wd), lambda i, j: (i, 0)) for wd in widths]
        + [pl.BlockSpec((w.shape[0], tn), lambda i, j: (0, j)), res],
        out_specs=res, out_shape=jax.ShapeDtypeStruct((m, n), F32),
        compiler_params=_cparams(2), name="out_proj")(*parts, w, x)


def _swiglu_up_body(a_ref, wg_ref, wu_ref, o_ref):
    a = a_ref[...]
    g = jnp.dot(a, wg_ref[...], preferred_element_type=F32)
    u = jnp.dot(a, wu_ref[...], preferred_element_type=F32)
    o_ref[...] = (g * (1.0 / (1.0 + jnp.exp(-g))) * u).astype(o_ref.dtype)


def _swiglu_up(a, wg, wu, *, tm, th, name):
    m, k = a.shape
    f = wg.shape[1]
    w_spec = pl.BlockSpec((k, th), lambda i, j: (0, j))
    return pl.pallas_call(
        _swiglu_up_body, grid=(m // tm, f // th),
        in_specs=[pl.BlockSpec((tm, k), lambda i, j: (i, 0)), w_spec, w_spec],
        out_specs=pl.BlockSpec((tm, th), lambda i, j: (i, j)),
        out_shape=jax.ShapeDtypeStruct((m, f), BF16),
        compiler_params=_cparams(2), name=name)(a, wg, wu)


def _ep_group_norm(acc, aux, outs, *, group):
    gain = aux[0][...]
    tn = acc.shape[1]
    ones = _group_ones(group)
    for c in range(tn // LANES):
        x = acc[:, c * LANES:(c + 1) * LANES]
        g = gain[:, c * LANES:(c + 1) * LANES]
        r = lax.rsqrt(_group_sumsq(x, ones) * (1.0 / group) + EPS)
        outs[0][:, c * LANES:(c + 1) * LANES] = (x * r * g).astype(outs[0].dtype)


def _ep_transpose_heads(acc, aux, outs, *, t):
    tm, tn = acc.shape
    ones = jnp.ones((ONES_ROWS, t), outs[0].dtype)
    for hh in range(tn // LANES):
        for s in range(tm // t):
            blk = acc[s * t:(s + 1) * t, hh * LANES:(hh + 1) * LANES]
            outs[0][hh, s, :LANES, :] = blk.T.astype(outs[0].dtype)
            outs[0][hh, s, LANES:, :] = ones


def _ep_latent_norms(acc, aux, outs, *, n_q):
    gain = aux[0][...]
    n = acc.shape[1]
    xq = acc[:, :n_q]
    xk = acc[:, n_q:]
    rq = lax.rsqrt(_sumsq_lanes(xq) * (1.0 / n_q) + EPS)
    rk = lax.rsqrt(_sumsq_lanes(xk) * (1.0 / (n - n_q)) + EPS)
    outs[0][:, :n_q] = (xq * rq * gain[:, :n_q]).astype(outs[0].dtype)
    outs[0][:, n_q:] = (xk * rk * gain[:, n_q:]).astype(outs[0].dtype)


def _rotate_half_matrix(half):
    src = lax.broadcasted_iota(jnp.int32, (LANES, LANES), 0)
    dst = lax.broadcasted_iota(jnp.int32, (LANES, LANES), 1)
    neg = (dst < half) & (src == dst + half)
    pos = (dst >= half) & (dst < 2 * half) & (src == dst - half)
    return jnp.where(neg, -1.0, jnp.where(pos, 1.0, 0.0)).astype(BF16)


def _rope_half_block(y, cos, sin, half):
    lane = lax.broadcasted_iota(jnp.int32, (1, LANES), 1)
    ra = pltpu.roll(y, half, axis=1)
    rb = pltpu.roll(y, LANES - half, axis=1)
    rot = jnp.where(lane < half, -rb, ra)
    return y * cos + rot * sin


def _ep_tail(acc, aux, outs, *, rope_dim, n_gate):
    gain = aux[0][...]
    bias = aux[1][...]
    cos = aux[2][...]
    sin = aux[3][...]
    shift_col = aux[4][...]
    lane = lax.broadcasted_iota(jnp.int32, (1, LANES), 1)
    is_pe = lane < rope_dim
    xpe = jnp.where(is_pe, acc, 0.0)
    r = lax.rsqrt(_sumsq_lanes(xpe) * (1.0 / rope_dim) + EPS)
    y = xpe * r * gain
    outs[0][...] = (_rope_half_block(y, cos, sin, rope_dim // 2) + shift_col).astype(outs[0].dtype)
    z = acc + bias
    logsig = -(jnp.maximum(-z, 0.0) + jnp.log(1.0 + jnp.exp(-jnp.abs(z))))
    is_gate = (lane >= rope_dim) & (lane < rope_dim + n_gate)
    outs[1][...] = jnp.where(is_gate, logsig, 0.0)


HEAD_W = 2 * LANES


def _ep_latent_q(acc, aux, outs, *, rope_dim):
    g_nope = aux[0][...]
    g_rope = aux[1][...]
    cos = aux[2][...]
    sin = aux[3][...]
    one_col = aux[4][...]
    ones = _group_ones(LANES)
    rot_mat = _rotate_half_matrix(rope_dim // 2)
    for hh in range(acc.shape[1] // HEAD_W):
        xn = acc[:, hh * HEAD_W:hh * HEAD_W + LANES]
        xr = acc[:, hh * HEAD_W + LANES:(hh + 1) * HEAD_W]
        rn = lax.rsqrt(_group_sumsq(xn, ones) * (1.0 / LANES) + EPS)
        outs[0][:, hh * HEAD_W:hh * HEAD_W + LANES] = (xn * rn * g_nope).astype(outs[0].dtype)
        rr = lax.rsqrt(_group_sumsq(xr, ones) * (1.0 / rope_dim) + EPS)
        y = xr * rr * g_rope
        rot = jnp.dot(y.astype(BF16), rot_mat, preferred_element_type=F32)
        outs[0][:, hh * HEAD_W + LANES:(hh + 1) * HEAD_W] = (y * cos + rot * sin + one_col).astype(outs[0].dtype)


def _ep_latent_k(acc, aux, outs):
    g_nope = aux[0][...]
    kpe = aux[1][...]
    for hh in range(acc.shape[1] // LANES):
        xn = acc[:, hh * LANES:(hh + 1) * LANES]
        rn = lax.rsqrt(_sumsq_lanes(xn) * (1.0 / LANES) + EPS)
        outs[0][:, hh * HEAD_W:hh * HEAD_W + LANES] = (xn * rn * g_nope).astype(outs[0].dtype)
        outs[0][:, hh * HEAD_W + LANES:(hh + 1) * HEAD_W] = kpe


def _norm_body(x_ref, g_ref, h_ref):
    x = x_ref[...]
    d = x.shape[1]
    r = lax.rsqrt(_sumsq_lanes(x) * (1.0 / d) + EPS)
    h_ref[...] = (x * r * g_ref[...]).astype(h_ref.dtype)


def _rms_norm(x, gain, *, tm, name):
    t, d = x.shape
    return pl.pallas_call(
        _norm_body, grid=(t // tm,),
        in_specs=[pl.BlockSpec((tm, d), lambda i: (i, 0)), pl.BlockSpec((1, d), lambda i: (0, 0))],
        out_specs=pl.BlockSpec((tm, d), lambda i: (i, 0)),
        out_shape=jax.ShapeDtypeStruct((t, d), BF16),
        compiler_params=_cparams(1), name=name)(x, gain.reshape(1, d))


def _add_norm_body(x_ref, d_ref, g_ref, xo_ref, h_ref):
    x = x_ref[...] + d_ref[...].astype(F32)
    xo_ref[...] = x
    r = lax.rsqrt(_sumsq_lanes(x) * (1.0 / x.shape[1]) + EPS)
    h_ref[...] = (x * r * g_ref[...]).astype(h_ref.dtype)


def _add_rms_norm(x, delta, gain, *, tm, name):
    t, d = x.shape
    tile = pl.BlockSpec((tm, d), lambda i: (i, 0))
    return pl.pallas_call(
        _add_norm_body, grid=(t // tm,),
        in_specs=[tile, tile, pl.BlockSpec((1, d), lambda i: (0, 0))],
        out_specs=[tile, tile],
        out_shape=[jax.ShapeDtypeStruct((t, d), F32), jax.ShapeDtypeStruct((t, d), BF16)],
        compiler_params=_cparams(1), name=name)(x, delta, gain.reshape(1, d))


def _add_body(x_ref, d_ref, o_ref):
    o_ref[...] = x_ref[...] + d_ref[...].astype(F32)


def _residual_add(x, delta, *, tm):
    t, d = x.shape
    tile = pl.BlockSpec((tm, d), lambda i: (i, 0))
    return pl.pallas_call(
        _add_body, grid=(t // tm,), in_specs=[tile, tile], out_specs=tile,
        out_shape=jax.ShapeDtypeStruct((t, d), F32),
        compiler_params=_cparams(1), name="residual_add")(x, delta)


ROUTE_I1, ROUTE_I2, ROUTE_G1, ROUTE_G2, ROUTE_R1, ROUTE_R2 = range(6)


def _norm_router_body(x_ref, g_ref, wr_ref, h_ref, route_ref, count_ref, carry_ref, *, n_exp):
    @pl.when(pl.program_id(0) == 0)
    def _():
        carry_ref[...] = jnp.zeros_like(carry_ref)

    x = x_ref[...]
    tm, d = x.shape
    r = lax.rsqrt(_sumsq_lanes(x) * (1.0 / d) + EPS)
    h = x * r * g_ref[...]
    h_ref[...] = h.astype(h_ref.dtype)
    logits = jnp.dot(h, wr_ref[...], preferred_element_type=F32, precision=_HI)
    lane = lax.broadcasted_iota(jnp.int32, logits.shape, 1).astype(F32)
    lg = jnp.where(lane < n_exp, logits, -jnp.inf)
    m1 = jnp.max(lg, axis=-1, keepdims=True)
    i1 = jnp.min(jnp.where(lg == m1, lane, float(LANES)), axis=-1, keepdims=True)
    lg2 = jnp.where(lane == i1, -jnp.inf, lg)
    m2 = jnp.max(lg2, axis=-1, keepdims=True)
    i2 = jnp.min(jnp.where(lg2 == m2, lane, float(LANES)), axis=-1, keepdims=True)
    e2 = jnp.exp(m2 - m1)
    g1 = 1.0 / (1.0 + e2)
    g2 = e2 * g1

    chosen = jnp.where((lane == i1) | (lane == i2), 1.0, 0.0)
    rr = lax.broadcasted_iota(jnp.int32, (tm, tm), 0)
    cc = lax.broadcasted_iota(jnp.int32, (tm, tm), 1)
    strict = jnp.where(rr > cc, 1.0, 0.0).astype(BF16)
    before = jnp.dot(strict, chosen.astype(BF16), preferred_element_type=F32) + carry_ref[...]
    rank1 = jnp.sum(jnp.where(lane == i1, before, 0.0), axis=-1, keepdims=True)
    rank2 = jnp.sum(jnp.where(lane == i2, before, 0.0), axis=-1, keepdims=True)
    carry_ref[...] += jnp.sum(chosen, axis=0, keepdims=True)
    count_ref[...] = carry_ref[...]

    rec = jnp.zeros((tm, LANES), F32)
    for k, v in ((ROUTE_I1, i1), (ROUTE_I2, i2), (ROUTE_G1, g1), (ROUTE_G2, g2), (ROUTE_R1, rank1), (ROUTE_R2, rank2)):
        rec = jnp.where(lane == float(k), v, rec)
    route_ref[...] = rec


def _rms_norm_router(x, gain, router, *, tm, name):
    t, d = x.shape
    n_exp = router.shape[1]
    wr = jnp.zeros((d, LANES), F32).at[:, :n_exp].set(router.astype(F32))
    return pl.pallas_call(
        functools.partial(_norm_router_body, n_exp=n_exp), grid=(t // tm,),
        in_specs=[pl.BlockSpec((tm, d), lambda i: (i, 0)), pl.BlockSpec((1, d), lambda i: (0, 0)),
                  pl.BlockSpec((d, LANES), lambda i: (0, 0))],
        out_specs=[pl.BlockSpec((tm, d), lambda i: (i, 0)),
                   pl.BlockSpec((tm, LANES), lambda i: (i, 0)),
                   pl.BlockSpec((1, LANES), lambda i: (0, 0))],
        out_shape=[jax.ShapeDtypeStruct((t, d), BF16),
                   jax.ShapeDtypeStruct((t, LANES), F32),
                   jax.ShapeDtypeStruct((1, LANES), F32)],
        scratch_shapes=[pltpu.VMEM((1, LANES), F32)],
        compiler_params=_cparams(1), name=name)(x, gain.reshape(1, d), wr)


def _row_view(a):
    rows, d = a.shape
    return a.reshape(rows, d // LANES, LANES)


def _dispatch_body(src_ref, h_ref, o_ref, sem, *, tm):
    def issue(pair, carry):
        for prio in range(2):
            r = 2 * pair + prio
            pltpu.make_async_copy(h_ref.at[src_ref[0, 0, r]], o_ref.at[r], sem).start(priority=prio)
        return carry

    def drain(r, carry):
        pltpu.make_async_copy(h_ref.at[0], o_ref.at[0], sem).wait()
        return carry

    lax.fori_loop(0, tm // 2, issue, 0)
    lax.fori_loop(0, tm, drain, 0)


def _dispatch(h_rows, src, *, tm):
    n_tiles = src.shape[0]
    chunks = h_rows.shape[1]
    return pl.pallas_call(
        functools.partial(_dispatch_body, tm=tm), grid=(n_tiles,),
        in_specs=[pl.BlockSpec((1, 1, tm), lambda i: (i, 0, 0), memory_space=pltpu.SMEM),
                  pl.BlockSpec(memory_space=pl.ANY)],
        out_specs=pl.BlockSpec((tm, chunks, LANES), lambda i: (i, 0, 0)),
        out_shape=jax.ShapeDtypeStruct((n_tiles * tm, chunks, LANES), h_rows.dtype),
        scratch_shapes=[pltpu.SemaphoreType.DMA(())],
        compiler_params=_cparams(1), name="moe_dispatch")(src, h_rows)


def _combine_body(pos_ref, gate_ref, ys_ref, o_ref, buf, sem, *, tm):
    def issue(r, carry):
        pltpu.make_async_copy(ys_ref.at[pos_ref[0, 0, r]], buf.at[0, r], sem).start(priority=0)
        pltpu.make_async_copy(ys_ref.at[pos_ref[0, 0, tm + r]], buf.at[1, r], sem).start(priority=1)
        return carry

    def drain(r, carry):
        pltpu.make_async_copy(ys_ref.at[0], buf.at[0, 0], sem).wait()
        pltpu.make_async_copy(ys_ref.at[0], buf.at[1, 0], sem).wait()
        return carry

    def mix(r, carry):
        o_ref[r] = (gate_ref[0, 0, r] * buf[0, r].astype(F32)
                    + gate_ref[0, 0, tm + r] * buf[1, r].astype(F32)).astype(o_ref.dtype)
        return carry

    lax.fori_loop(0, tm, issue, 0)
    lax.fori_loop(0, tm, drain, 0)
    lax.fori_loop(0, tm, mix, 0)


def _combine(ys_rows, pos, gates, *, tm):
    n_tiles = pos.shape[0]
    chunks = ys_rows.shape[1]
    smem_tile = pl.BlockSpec((1, 1, 2 * tm), lambda i: (i, 0, 0), memory_space=pltpu.SMEM)
    return pl.pallas_call(
        functools.partial(_combine_body, tm=tm), grid=(n_tiles,),
        in_specs=[smem_tile, smem_tile, pl.BlockSpec(memory_space=pl.ANY)],
        out_specs=pl.BlockSpec((tm, chunks, LANES), lambda i: (i, 0, 0)),
        out_shape=jax.ShapeDtypeStruct((n_tiles * tm, chunks, LANES), ys_rows.dtype),
        scratch_shapes=[pltpu.VMEM((2, tm, chunks, LANES), ys_rows.dtype), pltpu.SemaphoreType.DMA(())],
        compiler_params=_cparams(1), name="moe_combine")(pos, gates, ys_rows)


def _grouped_up_body(te_ref, nv_ref, a_ref, wg_ref, wu_ref, o_ref):
    @pl.when(pl.program_id(1) < nv_ref[0])
    def _():
        a = a_ref[...]
        g = jnp.dot(a, wg_ref[...], preferred_element_type=F32)
        u = jnp.dot(a, wu_ref[...], preferred_element_type=F32)
        o_ref[...] = (g * (1.0 / (1.0 + jnp.exp(-g))) * u).astype(o_ref.dtype)

    @pl.when(pl.program_id(1) >= nv_ref[0])
    def _():
        o_ref[...] = jnp.zeros_like(o_ref)


def _grouped_down_body(te_ref, nv_ref, a_ref, w_ref, o_ref):
    @pl.when(pl.program_id(1) < nv_ref[0])
    def _():
        o_ref[...] = jnp.dot(a_ref[...], w_ref[...], preferred_element_type=F32).astype(o_ref.dtype)

    @pl.when(pl.program_id(1) >= nv_ref[0])
    def _():
        o_ref[...] = jnp.zeros_like(o_ref)


def _grouped_matmul(body, a, weights, tile_expert, n_valid, *, tmo, tn, name):
    r, k = a.shape
    n = weights[0].shape[2]

    def row(j, i, te, nv):
        return jnp.minimum(i, nv[0] - 1)

    grid_spec = pltpu.PrefetchScalarGridSpec(
        num_scalar_prefetch=2, grid=(n // tn, r // tmo),
        in_specs=[pl.BlockSpec((tmo, k), lambda j, i, te, nv: (row(j, i, te, nv), 0))]
        + [pl.BlockSpec((None, k, tn), lambda j, i, te, nv: (te[row(j, i, te, nv)], 0, j)) for _ in weights],
        out_specs=pl.BlockSpec((tmo, tn), lambda j, i, te, nv: (i, j)))
    return pl.pallas_call(
        body, grid_spec=grid_spec, out_shape=jax.ShapeDtypeStruct((r, n), BF16),
        compiler_params=_cparams(2), name=name)(tile_expert, n_valid, a, *weights)


def _rope_body(pos_ref, f_ref, cos_ref, sin_ref):
    ang = pos_ref[...] * f_ref[...]
    live = f_ref[...] > 0.0
    cos_ref[...] = jnp.where(live, jnp.cos(ang), 0.0)
    sin_ref[...] = jnp.where(live, jnp.sin(ang), 0.0)


def _rope_tables(positions, *, half, tm):
    t = positions.size
    inv_freq = ROPE_BASE ** (-jnp.arange(half, dtype=F32) / half)
    f_row = jnp.concatenate([inv_freq, inv_freq, jnp.zeros((LANES - 2 * half,), F32)]).reshape(1, LANES)
    pos_rep = jnp.broadcast_to(positions.astype(F32).reshape(t, 1), (t, LANES))
    spec = pl.BlockSpec((tm, LANES), lambda i: (i, 0))
    return pl.pallas_call(
        _rope_body, grid=(t // tm,),
        in_specs=[spec, pl.BlockSpec((1, LANES), lambda i: (0, 0))],
        out_specs=[spec, spec],
        out_shape=[jax.ShapeDtypeStruct((t, LANES), F32)] * 2,
        compiler_params=_cparams(1), name="rope_tables")(pos_rep, f_row)


def _bias_body(rb_ref, o_ref, *, t):
    h = pl.program_id(0)
    d = pl.program_id(1)
    j = lax.broadcasted_iota(jnp.int32, (t, t), 0)
    i = lax.broadcasted_iota(jnp.int32, (t, t), 1)
    rel = d * t + i - j
    n = jnp.maximum(rel, 0)
    max_exact = NUM_BUCKETS // 2
    large = max_exact + (
        jnp.log(jnp.maximum(n, max_exact).astype(F32) / max_exact)
        / math.log(MAX_DISTANCE / max_exact) * (NUM_BUCKETS - max_exact)
    ).astype(jnp.int32)
    large = jnp.minimum(large, NUM_BUCKETS - 1)
    bucket = jnp.where(n < max_exact, n, large)
    far = rb_ref[NUM_BUCKETS - 1, h]
    val = jnp.zeros((t, t), F32)
    for b in range(NUM_BUCKETS - 1):
        val = jnp.where(bucket == b, rb_ref[b, h] - far, val)
    o_ref[...] = jnp.where(rel >= 0, val * LOG2E, NEG)


def _bias_tiles(rel_bias, *, t):
    assert t + 1 >= MAX_DISTANCE, "tiles beyond the first sub-diagonal must be in the last bucket"
    nh = rel_bias.shape[1]
    return pl.pallas_call(
        functools.partial(_bias_body, t=t), grid=(nh, 2),
        in_specs=[pl.BlockSpec(memory_space=pltpu.SMEM)],
        out_specs=pl.BlockSpec((None, None, t, t), lambda h, d: (h, d, 0, 0)),
        out_shape=jax.ShapeDtypeStruct((nh, 2, t, t), F32),
        compiler_params=_cparams(2), name="t5_bias_tiles")(rel_bias.astype(F32))


N_PIECES = 3


def _split_pieces(x):
    pieces = []
    rest = x
    for _ in range(N_PIECES):
        p = rest.astype(BF16).astype(F32)
        pieces.append(p)
        rest = rest - p
    return pieces


def _cum_body(shift_ref, lf_ref, o_ref, end_ref, carry_ref, *, n_heads, lane0):
    @pl.when(pl.program_id(1) == 0)
    def _():
        carry_ref[...] = jnp.zeros_like(carry_ref)

    x = lf_ref[...]
    tb = x.shape[0]
    r = lax.broadcasted_iota(jnp.int32, (tb, tb), 0)
    c = lax.broadcasted_iota(jnp.int32, (tb, tb), 1)
    tri = jnp.where(r >= c, 1.0, 0.0).astype(F32)
    cum = jnp.dot(tri, x, preferred_element_type=F32, precision=_HI) + carry_ref[...]
    carry_ref[...] = cum[tb - 1:tb, :]
    cum2 = cum * LOG2E
    end_ref[0] = cum2[tb - 1:tb, :]
    lane = lax.broadcasted_iota(jnp.int32, (tb, LANES), 1)
    shift = shift_ref[0]
    for hh in range(n_heads):
        col = cum2[:, lane0 + hh:lane0 + hh + 1]
        qcols = jnp.where(lane < N_PIECES, 1.0, 0.0)
        kcols = jnp.where((lane >= N_PIECES) & (lane < 2 * N_PIECES), 1.0, 0.0)
        for n, (pq, pk) in enumerate(zip(_split_pieces(col - shift), _split_pieces(-col))):
            qcols = jnp.where(lane == N_PIECES + n, pq, qcols)
            kcols = jnp.where(lane == n, pk, kcols)
        o_ref[:, hh * LANES:(hh + 1) * LANES] = qcols.astype(o_ref.dtype)
        o_ref[:, (n_heads + hh) * LANES:(n_heads + hh + 1) * LANES] = kcols.astype(o_ref.dtype)


def _forget_columns(logf, shift, *, batch, seq, n_heads, lane0, tb):
    nb = seq // tb
    width = 2 * n_heads * LANES
    return pl.pallas_call(
        functools.partial(_cum_body, n_heads=n_heads, lane0=lane0), grid=(batch, nb),
        in_specs=[pl.BlockSpec(memory_space=pltpu.SMEM),
                  pl.BlockSpec((tb, LANES), lambda b, i: (b * nb + i, 0))],
        out_specs=[pl.BlockSpec((tb, width), lambda b, i: (b * nb + i, 0)),
                   pl.BlockSpec((1, 1, LANES), lambda b, i: (b * nb + i, 0, 0))],
        out_shape=[jax.ShapeDtypeStruct((batch * seq, width), BF16),
                   jax.ShapeDtypeStruct((batch * nb, 1, LANES), F32)],
        scratch_shapes=[pltpu.VMEM((1, LANES), F32)],
        compiler_params=_cparams(2), name="forget_cumsum")(shift.reshape(1).astype(F32), logf)


def _online_block(s, vb, m_ref, acc_ref):
    m_prev = m_ref[...]
    m_new = jnp.maximum(m_prev, jnp.max(s, axis=0, keepdims=True))
    alpha = jnp.exp2(m_prev - m_new)
    p = jnp.exp2(s - m_new).astype(BF16)
    acc_ref[...] = alpha * acc_ref[...] + jnp.dot(vb, p, preferred_element_type=F32)
    m_ref[...] = m_new


def _shifted_block(s, vb, acc_ref):
    p = jnp.exp2(s).astype(BF16)
    acc_ref[...] += jnp.dot(vb, p, preferred_element_type=F32)


def _causal_mask(t):
    j = lax.broadcasted_iota(jnp.int32, (t, t), 0)
    i = lax.broadcasted_iota(jnp.int32, (t, t), 1)
    return j <= i


_NT = (((1,), (1,)), ((), ()))
FAR_UNROLL = 4


def _init_state(m_ref, acc_ref):
    m_ref[...] = jnp.full(m_ref.shape, NEG, F32)
    acc_ref[...] = jnp.zeros(acc_ref.shape, F32)


def _attn_body(*refs, mode, t, hp, dq, dv, qk_dim=None, lambda_init=None):
    safe_ref, start_ref, q_ref, k_ref, vt_ref = refs[:5]
    if mode == "latent":
        o_ref, qc_ref, m_ref, acc_ref = refs[5:]
    elif mode == "forget":
        qx_ref, kx_ref, o_ref, qc_ref, m_ref, acc_ref = refs[5:]
    else:
        qx_ref, kx_ref, bias_ref, lam_ref, hn_ref, o_ref, qc_ref, m_ref, acc_ref = refs[5:]
    n_map = 2 if mode == "diff" else 1
    qi = pl.program_id(2)
    step = (pl.program_id(0) * pl.num_programs(1) + pl.program_id(1)) * pl.num_programs(2) + qi
    _init_state(m_ref, acc_ref)

    for hh in range(hp):
        q = q_ref[:, hh * dq:(hh + 1) * dq]
        if mode == "latent":
            qc_ref[hh] = q
        elif mode == "forget":
            qc_ref[hh] = jnp.concatenate([q, qx_ref[:, hh * LANES:(hh + 1) * LANES]], axis=1)
        else:
            lane = lax.broadcasted_iota(jnp.int32, (t, dq), 1)
            zero = jnp.zeros_like(q)
            qc_ref[2 * hh] = jnp.concatenate([jnp.where(lane < qk_dim, q, zero), qx_ref[...]], axis=1)
            qc_ref[2 * hh + 1] = jnp.concatenate([jnp.where(lane < qk_dim, zero, q), qx_ref[...]], axis=1)

    def keys(kj, hh):
        start = pl.multiple_of(kj * t, t)
        kb = k_ref[pl.ds(start, t), hh * dq:(hh + 1) * dq]
        if mode == "forget":
            kb = jnp.concatenate([kb, kx_ref[pl.ds(start, t), hh * LANES:(hh + 1) * LANES]], axis=1)
        elif mode == "diff":
            kb = jnp.concatenate([kb, kx_ref[hh]], axis=1)
        return kb

    def run(update, skip_dead):
        def scores(kj, kind, hh, mm):
            s = lax.dot_general(keys(kj, hh), qc_ref[n_map * hh + mm], _NT, preferred_element_type=F32)
            if mode == "diff":
                if kind != "far":
                    s = s + bias_ref[hh, 0 if kind == "diag" else 1]
            elif kind == "diag":
                s = jnp.where(_causal_mask(t), s, NEG)
            return s

        def blocks(items):
            work = [(kj, kind, hh, mm) for kj, kind in items for hh in range(hp) for mm in range(n_map)]
            s_next = scores(*work[0])
            for n, (kj, kind, hh, mm) in enumerate(work):
                s_cur = s_next
                if n + 1 < len(work):
                    s_next = scores(*work[n + 1])
                update(s_cur, vt_ref[hh, kj], n_map * hh + mm)

        n_far = jnp.maximum(qi - 1, 0) if mode == "diff" else qi
        k0 = jnp.minimum(start_ref[step], n_far) if skip_dead else 0
        near = [(qi, "diag")] + ([(qi - 1, "sub")] if mode == "diff" else [])
        n_tail = FAR_UNROLL - len(near)
        if skip_dead:
            merge = (n_far - k0 >= n_tail) & (qi >= len(near) - 1)
        else:
            merge = qi < 0
        n_loop = jnp.where(merge, n_far - k0 - n_tail, n_far - k0)
        n_groups = n_loop // FAR_UNROLL

        def far_group(g, carry):
            blocks([(k0 + g * FAR_UNROLL + u, "far") for u in range(FAR_UNROLL)])
            return carry

        def far_single(kj, carry):
            blocks([(kj, "far")])
            return carry

        lax.fori_loop(0, n_groups, far_group, 0)
        lax.fori_loop(k0 + n_groups * FAR_UNROLL, k0 + n_loop, far_single, 0)

        if skip_dead:
            @pl.when(merge)
            def _():
                blocks([(n_far - 1 - u, "far") for u in range(n_tail)] + near[::-1])

        @pl.when(jnp.logical_not(merge))
        def _():
            if mode == "diff":
                @pl.when(qi >= 1)
                def _():
                    blocks([(qi - 1, "sub")])
            blocks([(qi, "diag")])

    @pl.when(safe_ref[0] != 0)
    def _():
        run(lambda s, vb, c: _shifted_block(s, vb, acc_ref.at[c]), True)

    @pl.when(safe_ref[0] == 0)
    def _():
        run(lambda s, vb, c: _online_block(s, vb, m_ref.at[c], acc_ref.at[c]), False)

    if mode == "diff":
        lp = lam_ref[...]
        lam = (jnp.exp(jnp.sum(lp[0:1] * lp[1:2], axis=-1, keepdims=True))
               - jnp.exp(jnp.sum(lp[2:3] * lp[3:4], axis=-1, keepdims=True)) + lambda_init)
    for hh in range(hp):
        if mode == "diff":
            a0 = acc_ref[2 * hh]
            a1 = acc_ref[2 * hh + 1]
            o = a0[:dv] * (1.0 / a0[dv:dv + 1]) - lam * (a1[:dv] * (1.0 / a1[dv:dv + 1]))
            r = lax.rsqrt(jnp.sum(o * o, axis=0, keepdims=True) * (1.0 / dv) + EPS)
            o = o * r * hn_ref[...] * (1.0 - lambda_init)
        else:
            a = acc_ref[hh]
            o = a[:dv] * (1.0 / a[dv:dv + 1])
        o_ref[:, hh * dv:(hh + 1) * dv] = o.T.astype(o_ref.dtype)


def _attention(mode, safe, q_arr, k_arr, vt_arr, *, batch, seq, n_heads, t, hp, dq, q_col0, k_col0,
               extra=(), extra_specs=(), body_kwargs=None, first_block=None, name):
    nq = seq // t
    dvp = vt_arr.shape[2]
    dv = dvp - ONES_ROWS
    tokens = batch * seq
    assert n_heads % hp == 0 and q_col0 % hp == 0 and k_col0 % hp == 0
    qc, kc = q_col0 // hp, k_col0 // hp
    if first_block is None:
        first_block = jnp.zeros((batch * (n_heads // hp) * nq,), jnp.int32)
    in_specs = [
        pl.BlockSpec(memory_space=pltpu.SMEM),
        pl.BlockSpec(memory_space=pltpu.SMEM),
        pl.BlockSpec((t, hp * dq), lambda b, h, i: (b * nq + i, qc + h)),
        pl.BlockSpec((seq, hp * dq), lambda b, h, i: (b, kc + h)),
        pl.BlockSpec((hp, nq, dvp, t), lambda b, h, i: (h, b, 0, 0)),
        *extra_specs,
    ]
    n_chain = 2 * hp if mode == "diff" else hp
    scratch = [pltpu.VMEM((n_chain, t, 2 * LANES), BF16),
               pltpu.VMEM((n_chain, 1, t), F32), pltpu.VMEM((n_chain, dvp, t), F32)]
    body = functools.partial(_attn_body, mode=mode, t=t, hp=hp, dq=dq, dv=dv, **(body_kwargs or {}))
    return pl.pallas_call(
        body, grid=(batch, n_heads // hp, nq), in_specs=in_specs,
        out_specs=pl.BlockSpec((t, hp * dv), lambda b, h, i: (b * nq + i, h)),
        out_shape=jax.ShapeDtypeStruct((tokens, n_heads * dv), BF16),
        scratch_shapes=scratch, compiler_params=_cparams(3), name=name)(
            safe, first_block, q_arr, k_arr, vt_arr, *extra)


def _pick(n, prefs):
    for p in prefs:
        if n % p == 0:
            return p
    raise ValueError((n, prefs))


SKIP_LOG2 = 160.0
SAFE_DEPTH = 110.0


def _round_up_bf16(c):
    return (c * (1.0 + 2.0 ** -6)).astype(BF16).astype(F32)


def _safe_flag(depth):
    return (depth <= SAFE_DEPTH).astype(jnp.int32).reshape(1)


def kernel(x, positions, attn_norm, w_in, b_forget, a_qk_norm, a_lambda, a_head_norm, rel_bias,
           b_qk_norm, c_q_a_norm, c_kv_a_norm, c_w_uq, c_w_ukv, c_qk_norm_nope, c_qk_norm_rope,
           w_out, ffn_norm, dense_w_gate, dense_w_up, dense_w_down, moe_router, moe_w_gate,
           moe_w_up, moe_w_down):
    batch, seq, d_model = x.shape
    depth = w_in.shape[0]
    tokens = batch * seq
    n_slots = d_model // LANES
    a_heads = n_slots // 4
    b_heads = n_slots // 4
    c_heads = n_slots // 2
    a_qk = a_qk_norm.shape[-1]
    b_dim = b_qk_norm.shape[-1]
    c_q_lora = c_q_a_norm.shape[-1]
    c_kv_lora = c_kv_a_norm.shape[-1]
    c_nope = c_qk_norm_nope.shape[-1]
    c_rope = c_qk_norm_rope.shape[-1]
    c_v = c_w_ukv.shape[-1] // c_heads - c_nope
    n_exp = moe_router.shape[-1]
    assert 2 * a_qk == LANES and b_dim == LANES and c_nope == LANES and c_v == LANES and 2 * c_rope == LANES

    t = _pick(seq, (512, 256, 128))
    tm = _pick(tokens, (1024, 512, 256))
    tm_n = _pick(tokens, (256, 128))
    hp_a, hp_b, hp_c = 1, 2, 2

    sizes = (a_heads * 2 * a_qk, a_heads * 2 * a_qk, a_heads * LANES, b_heads * b_dim, b_heads * b_dim,
             b_heads * b_dim, b_heads, c_q_lora, c_kv_lora, c_rope)
    offs = [0]
    for s in sizes:
        offs.append(offs[-1] + s)
    o_aq, o_ak, o_av, o_bq, o_bk, o_bv, o_f, o_cq, o_ckv, o_kpe, o_end = offs

    xf = x.reshape(tokens, d_model)
    cos_t, sin_t = _rope_tables(positions, half=c_rope // 2, tm=tm)
    bias_tiles = _bias_tiles(rel_bias, t=t)
    bias_log2 = (rel_bias.astype(F32) - rel_bias[-1:].astype(F32)) * LOG2E
    bias_max = jnp.max(bias_log2, axis=0)
    qx_a = jnp.zeros((t, LANES), F32).at[:, 0].set(1.0).astype(BF16)

    row_spec = lambda width: pl.BlockSpec((1, width), lambda i, j: (0, j))
    fixed_row = pl.BlockSpec((1, LANES), lambda i, j: (0, 0))
    tok_lane = pl.BlockSpec((tm, LANES), lambda i, j: (i, 0))

    def vt_call(h_bf, w, nh, name, a_cols=None, w_cols=None):
        tn = _pick(nh * LANES, (512, 256, 128))
        return _matmul(
            h_bf, w, a_cols=a_cols, w_cols=w_cols, tm=tm, tn=tn,
            epilogue=functools.partial(_ep_transpose_heads, t=t),
            out_shape=jax.ShapeDtypeStruct((nh, tokens // t, LANES + ONES_ROWS, t), BF16),
            out_specs=pl.BlockSpec((tn // LANES, tm // t, LANES + ONES_ROWS, t), lambda i, j: (j, i, 0, 0)),
            name=name)

    pending = None
    for layer in range(depth):
        lambda_init = 0.8 - 0.6 * math.exp(-0.3 * layer)
        w = w_in[layer].astype(BF16)

        if pending is None:
            h = _rms_norm(xf, attn_norm[layer], tm=tm_n, name="attn_norm")
        else:
            xf, h = _add_rms_norm(xf, pending, attn_norm[layer], tm=tm_n, name="attn_norm")
            pending = None

        a_scale = a_qk ** -0.5 * LOG2E
        gain_a = jnp.concatenate([jnp.tile(a_qk_norm[layer, 0] * a_scale, 2 * a_heads),
                                  jnp.tile(a_qk_norm[layer, 1], 2 * a_heads)]).reshape(1, -1)
        tn = 512
        qk_a = _matmul(h, w, w_cols=(o_aq, o_av - o_aq), tm=tm, tn=tn,
                       epilogue=functools.partial(_ep_group_norm, group=a_qk),
                       aux=(gain_a,), aux_specs=(row_spec(tn),),
                       out_shape=jax.ShapeDtypeStruct((tokens, o_av - o_aq), BF16),
                       out_specs=pl.BlockSpec((tm, tn), lambda i, j: (i, j)), name="proj_a_qk")
        vt_a = vt_call(h, w, a_heads, "proj_a_v", w_cols=(o_av, o_bq - o_av))
        hn = jnp.broadcast_to(a_head_norm[layer].reshape(LANES, 1), (LANES, t))
        bound_a = a_qk * jnp.max(jnp.abs(a_qk_norm[layer, 0] * a_scale)) * jnp.max(jnp.abs(a_qk_norm[layer, 1]))
        shift_a = _round_up_bf16(bound_a + bias_max)
        safe_a = _safe_flag(2.0 * bound_a + jnp.max(bias_max - bias_log2[0]))
        kx_a = jnp.zeros((a_heads, t, LANES), F32).at[:, :, 0].set(-shift_a[:, None]).astype(BF16)
        o_a = _attention(
            "diff", safe_a, qk_a, qk_a, vt_a, batch=batch, seq=seq, n_heads=a_heads, t=t, hp=hp_a, dq=LANES,
            q_col0=0, k_col0=a_heads,
            extra=(qx_a, kx_a, bias_tiles, a_lambda[layer], hn),
            extra_specs=(pl.BlockSpec((t, LANES), lambda b, hh, i: (0, 0)),
                         pl.BlockSpec((hp_a, t, LANES), lambda b, hh, i: (hh, 0, 0)),
                         pl.BlockSpec((hp_a, 2, t, t), lambda b, hh, i: (hh, 0, 0, 0)),
                         pl.BlockSpec((4, a_qk), lambda b, hh, i: (0, 0)),
                         pl.BlockSpec((LANES, t), lambda b, hh, i: (0, 0))),
            body_kwargs=dict(qk_dim=a_qk, lambda_init=lambda_init), name="attn_diff")

        b_scale = b_dim ** -0.5 * LOG2E
        gain_b = jnp.concatenate([jnp.tile(b_qk_norm[layer, 0] * b_scale, b_heads),
                                  jnp.tile(b_qk_norm[layer, 1], b_heads)]).reshape(1, -1)
        qk_b = _matmul(h, w, w_cols=(o_bq, o_bv - o_bq), tm=tm, tn=tn,
                       epilogue=functools.partial(_ep_group_norm, group=b_dim),
                       aux=(gain_b,), aux_specs=(row_spec(tn),),
                       out_shape=jax.ShapeDtypeStruct((tokens, o_bv - o_bq), BF16),
                       out_specs=pl.BlockSpec((tm, tn), lambda i, j: (i, j)), name="proj_b_qk")
        vt_b = vt_call(h, w, b_heads, "proj_b_v", w_cols=(o_bv, o_f - o_bv))

        w_tail = jnp.concatenate([w[:, o_kpe:o_end], w[:, o_f:o_cq],
                                  jnp.zeros((d_model, LANES - c_rope - b_heads), BF16)], axis=1)
        gain_tail = jnp.concatenate([c_qk_norm_rope[layer, 1], jnp.zeros((LANES - c_rope,), F32)]).reshape(1, LANES)
        bias_tail = jnp.zeros((LANES,), F32).at[c_rope:c_rope + b_heads].set(b_forget[layer]).reshape(1, LANES)
        c_scale = (c_nope + c_rope) ** -0.5 * LOG2E
        gq_n = c_qk_norm_nope[layer, 0] * c_scale
        gq_r = c_qk_norm_rope[layer, 0] * c_scale
        gk_n = c_qk_norm_nope[layer, 1]
        gk_r = c_qk_norm_rope[layer, 1]
        bound_c = (jnp.sqrt(c_nope * jnp.max(gq_n * gq_n) + c_rope * jnp.max(gq_r * gq_r))
                   * jnp.sqrt(c_nope * jnp.max(gk_n * gk_n) + c_rope * jnp.max(gk_r * gk_r)))
        shift_c = _round_up_bf16(bound_c)
        safe_c = _safe_flag(2.0 * bound_c)
        shift_col = jnp.zeros((1, LANES), F32).at[0, c_rope].set(-shift_c)
        one_col = jnp.zeros((1, LANES), F32).at[0, c_rope].set(1.0)
        kpe, logf = _matmul(
            h, w_tail, tm=tm, tn=LANES,
            epilogue=functools.partial(_ep_tail, rope_dim=c_rope, n_gate=b_heads),
            aux=(gain_tail, bias_tail, cos_t, sin_t, shift_col),
            aux_specs=(fixed_row, fixed_row, tok_lane, tok_lane, fixed_row),
            out_shape=[jax.ShapeDtypeStruct((tokens, LANES), BF16), jax.ShapeDtypeStruct((tokens, LANES), F32)],
            out_specs=[tok_lane, tok_lane], name="proj_tail")
        bound_b = b_dim * jnp.max(jnp.abs(b_qk_norm[layer, 0] * b_scale)) * jnp.max(jnp.abs(b_qk_norm[layer, 1]))
        safe_b = _safe_flag(2.0 * bound_b)
        x_b, ends = _forget_columns(logf, bound_b, batch=batch, seq=seq, n_heads=b_heads, lane0=c_rope, tb=t)
        nq = seq // t
        nbh = b_heads // hp_b
        e_b = ends[:, 0, c_rope:c_rope + b_heads].reshape(batch, nq, b_heads)
        e_q = jnp.concatenate([jnp.zeros((batch, 1, b_heads), F32), e_b[:, :-1]], axis=1)
        dead = (e_q[:, :, None, :] - e_b[:, None, :, :]) < -SKIP_LOG2
        dead = dead & (jnp.arange(nq)[None, None, :, None] + 1 < jnp.arange(nq)[None, :, None, None])
        first_b = jnp.sum(dead.astype(jnp.int32), axis=2)
        first_b = jnp.min(first_b.reshape(batch, nq, nbh, hp_b), axis=3)
        first_b = jnp.transpose(first_b, (0, 2, 1)).reshape(-1)
        o_b = _attention(
            "forget", safe_b, qk_b, qk_b, vt_b, batch=batch, seq=seq, n_heads=b_heads, t=t, hp=hp_b, dq=LANES,
            q_col0=0, k_col0=b_heads,
            extra=(x_b, x_b),
            extra_specs=(pl.BlockSpec((t, hp_b * LANES), lambda b, hh, i: (b * nq + i, hh)),
                         pl.BlockSpec((seq, hp_b * LANES), lambda b, hh, i: (b, nbh + hh))),
            first_block=first_b, name="attn_forget")

        n_lat = c_q_lora + c_kv_lora
        gain_lat = jnp.concatenate([c_q_a_norm[layer], c_kv_a_norm[layer]]).reshape(1, n_lat)
        tm_lat = _pick(tokens, (512, 256))
        lat = _matmul(h, w[:, o_cq:o_kpe], tm=tm_lat, tn=n_lat,
                      epilogue=functools.partial(_ep_latent_norms, n_q=c_q_lora),
                      aux=(gain_lat,), aux_specs=(row_spec(n_lat),),
                      out_shape=jax.ShapeDtypeStruct((tokens, n_lat), BF16),
                      out_specs=pl.BlockSpec((tm_lat, n_lat), lambda i, j: (i, j)), name="proj_c_latent")
        wq = c_w_uq[layer].astype(BF16).reshape(c_q_lora, c_heads, c_nope + c_rope)
        wq = jnp.concatenate([wq, jnp.zeros((c_q_lora, c_heads, 2 * LANES - c_nope - c_rope), BF16)], axis=2)
        wq = wq.reshape(c_q_lora, c_heads * 2 * LANES)
        gq_nope = gq_n.reshape(1, LANES)
        gq_rope = jnp.concatenate([gq_r, jnp.zeros((LANES - c_rope,), F32)]).reshape(1, LANES)
        tn_q = 512
        lat_q_cols, lat_kv_cols = (0, c_q_lora), (c_q_lora, c_kv_lora)
        q_c = _matmul(lat, wq, a_cols=lat_q_cols, tm=tm, tn=tn_q, epilogue=functools.partial(_ep_latent_q, rope_dim=c_rope),
                      aux=(gq_nope, gq_rope, cos_t, sin_t, one_col),
                      aux_specs=(fixed_row, fixed_row, tok_lane, tok_lane, fixed_row),
                      out_shape=jax.ShapeDtypeStruct((tokens, c_heads * 2 * LANES), BF16),
                      out_specs=pl.BlockSpec((tm, tn_q), lambda i, j: (i, j)), name="proj_c_q")
        wkv = c_w_ukv[layer].astype(BF16).reshape(c_kv_lora, c_heads, c_nope + c_v)
        wk = wkv[:, :, :c_nope].reshape(c_kv_lora, c_heads * c_nope)
        wv = wkv[:, :, c_nope:].reshape(c_kv_lora, c_heads * c_v)
        gk_nope = c_qk_norm_nope[layer, 1].reshape(1, LANES)
        tn_k = 256
        k_c = _matmul(lat, wk, a_cols=lat_kv_cols, tm=tm, tn=tn_k, epilogue=_ep_latent_k,
                      aux=(gk_nope, kpe), aux_specs=(fixed_row, tok_lane),
                      out_shape=jax.ShapeDtypeStruct((tokens, c_heads * 2 * LANES), BF16),
                      out_specs=pl.BlockSpec((tm, 2 * tn_k), lambda i, j: (i, j)), name="proj_c_k")
        vt_c = vt_call(lat, wv, c_heads, "proj_c_v", a_cols=lat_kv_cols)
        o_c = _attention(
            "latent", safe_c, q_c, k_c, vt_c, batch=batch, seq=seq, n_heads=c_heads, t=t, hp=hp_c, dq=2 * LANES,
            q_col0=0, k_col0=0, name="attn_latent")

        tn_o = 512
        xf = _out_proj((o_a, o_b, o_c), w_out[layer].astype(BF16), xf, tm=tm, tn=tn_o)

        idx = layer // 2
        th = _pick(dense_w_gate.shape[-1], (512, 256))
        if layer % 2 == 0:
            h2 = _rms_norm(xf, ffn_norm[layer], tm=tm_n, name="ffn_norm")
            act = _swiglu_up(h2, dense_w_gate[idx].astype(BF16), dense_w_up[idx].astype(BF16),
                             tm=tm, th=th, name="ffn_up")
            tm_d = _pick(tokens, (512, 256))
            res_d = pl.BlockSpec((tm_d, tn_o), lambda i, j: (i, j))
            xf = _matmul(act, dense_w_down[idx].astype(BF16), tm=tm_d, tn=tn_o, epilogue=_ep_residual,
                         aux=(xf,), aux_specs=(res_d,),
                         out_shape=jax.ShapeDtypeStruct((tokens, d_model), F32), out_specs=res_d,
                         name="ffn_down")
        else:
            h2, route, counts = _rms_norm_router(xf, ffn_norm[layer], moe_router[idx], tm=tm_n,
                                                 name="ffn_norm_router")
            tmo = _pick(tokens, (512, 256, 128))
            n_rows = TOP_K * tokens + n_exp * tmo
            cnt = counts[0, :n_exp].astype(jnp.int32)
            padded = (cnt + tmo - 1) // tmo * tmo
            ends = jnp.cumsum(padded)
            starts = ends - padded
            n_valid = (ends[-1] // tmo).reshape(1).astype(jnp.int32)
            tile_expert = jnp.minimum(
                jnp.searchsorted(ends, jnp.arange(n_rows // tmo, dtype=jnp.int32) * tmo, side="right"),
                n_exp - 1).astype(jnp.int32)
            e1 = route[:, ROUTE_I1].astype(jnp.int32)
            e2 = route[:, ROUTE_I2].astype(jnp.int32)
            pos1 = starts[e1] + route[:, ROUTE_R1].astype(jnp.int32)
            pos2 = starts[e2] + route[:, ROUTE_R2].astype(jnp.int32)
            tm_r = _pick(tokens, (512, 256, 128))
            per_tile = lambda a, b: jnp.concatenate(
                [a.reshape(tokens // tm_r, 1, tm_r), b.reshape(tokens // tm_r, 1, tm_r)], axis=2)
            pos = per_tile(pos1, pos2)
            gates = per_tile(route[:, ROUTE_G1], route[:, ROUTE_G2])

            tok = jnp.arange(tokens, dtype=jnp.int32)
            src = jnp.zeros((n_rows,), jnp.int32).at[jnp.concatenate([pos1, pos2])].set(
                jnp.concatenate([tok, tok]), unique_indices=True)
            xs = _dispatch(_row_view(h2), src.reshape(n_rows // tm_r, 1, tm_r), tm=tm_r).reshape(n_rows, d_model)
            act = _grouped_matmul(_grouped_up_body, xs,
                                  (moe_w_gate[idx].astype(BF16), moe_w_up[idx].astype(BF16)),
                                  tile_expert, n_valid, tmo=tmo, tn=512, name="moe_up")
            ys = _grouped_matmul(_grouped_down_body, act, (moe_w_down[idx].astype(BF16),),
                                 tile_expert, n_valid, tmo=tmo, tn=1024, name="moe_down")
            pending = _combine(_row_view(ys), pos, gates, tm=tm_r).reshape(tokens, d_model)

    if pending is not None:
        xf = _residual_add(xf, pending, tm=tm_n)
    return xf.reshape(batch, seq, d_model)
```

```python
import functools
import math

import jax
import jax.numpy as jnp
from jax import lax
from jax.experimental import pallas as pl
from jax.experimental.pallas import tpu as pltpu

F32 = jnp.float32
BF16 = jnp.bfloat16

LANES = 128
ONES_ROWS = 16
LOG2E = 1.4426950408889634
NEG = -1e30
EPS = 1e-6
VMEM_LIMIT = 52 * 1024 * 1024

NUM_BUCKETS = 32
MAX_DISTANCE = 128
ROPE_BASE = 10000.0
TOP_K = 2

_HI = lax.Precision.HIGHEST


def _cparams(n_axes):
    return pltpu.CompilerParams(
        dimension_semantics=("arbitrary",) * n_axes, vmem_limit_bytes=VMEM_LIMIT)


def _mm_body(a_ref, w_ref, *rest, epilogue, n_aux):
    acc = jnp.dot(a_ref[...], w_ref[...], preferred_element_type=F32)
    epilogue(acc, rest[:n_aux], rest[n_aux:])


def _matmul(a, w, *, tm, tn, epilogue, out_shape, out_specs, aux=(), aux_specs=(), name,
            a_cols=None, w_cols=None):
    m = a.shape[0]
    a0, k = a_cols if a_cols is not None else (0, a.shape[1])
    w0, n = w_cols if w_cols is not None else (0, w.shape[1])
    if a0 % k:
        a, a0 = a[:, a0:a0 + k], 0
    if w0 % tn:
        w, w0 = w[:, w0:w0 + n], 0
    assert k == w.shape[0]
    assert m % tm == 0 and n % tn == 0, (m, n, tm, tn)
    ab, wb = a0 // k, w0 // tn
    return pl.pallas_call(
        functools.partial(_mm_body, epilogue=epilogue, n_aux=len(aux)),
        grid=(m // tm, n // tn),
        in_specs=[pl.BlockSpec((tm, k), lambda i, j: (i, ab)),
                  pl.BlockSpec((k, tn), lambda i, j: (0, wb + j)), *aux_specs],
        out_specs=out_specs, out_shape=out_shape,
        compiler_params=_cparams(2), name=name)(a, w, *aux)


def _sumsq_lanes(x):
    return jnp.sum(x * x, axis=-1, keepdims=True)


def _group_ones(group):
    r = lax.broadcasted_iota(jnp.int32, (LANES, LANES), 0)
    c = lax.broadcasted_iota(jnp.int32, (LANES, LANES), 1)
    same = (r < group) == (c < group) if group < LANES else (r >= 0)
    return jnp.where(same, 1.0, 0.0).astype(BF16)


def _group_sumsq(x, ones):
    return jnp.dot((x * x).astype(BF16), ones, preferred_element_type=F32)


def _ep_residual(acc, aux, outs):
    outs[0][...] = aux[0][...] + acc


def _out_proj_body(*refs, widths):
    part_refs = refs[:len(widths)]
    w_ref, x_ref, o_ref = refs[len(widths):]
    acc = x_ref[...]
    row = 0
    for p_ref, width in zip(part_refs, widths):
        acc = acc + jnp.dot(p_ref[...], w_ref[row:row + width, :], preferred_element_type=F32)
        row += width
    o_ref[...] = acc


def _out_proj(parts, w, x, *, tm, tn):
    m, n = x.shape
    widths = tuple(p.shape[1] for p in parts)
    res = pl.BlockSpec((tm, tn), lambda i, j: (i, j))
    return pl.pallas_call(
        functools.partial(_out_proj_body, widths=widths), grid=(m // tm, n // tn),
        in_specs=[pl.BlockSpec((tm, wd), lambda i, j: (i, 0)) for wd in widths]
        + [pl.BlockSpec((w.shape[0], tn), lambda i, j: (0, j)), res],
        out_specs=res, out_shape=jax.ShapeDtypeStruct((m, n), F32),
        compiler_params=_cparams(2), name="out_proj")(*parts, w, x)


def _swiglu_up_body(a_ref, wg_ref, wu_ref, o_ref):
    a = a_ref[...]
    g = jnp.dot(a, wg_ref[...], preferred_element_type=F32)
    u = jnp.dot(a, wu_ref[...], preferred_element_type=F32)
    o_ref[...] = (g * (1.0 / (1.0 + jnp.exp(-g))) * u).astype(o_ref.dtype)


def _swiglu_up(a, wg, wu, *, tm, th, name):
    m, k = a.shape
    f = wg.shape[1]
    w_spec = pl.BlockSpec((k, th), lambda i, j: (0, j))
    return pl.pallas_call(
        _swiglu_up_body, grid=(m // tm, f // th),
        in_specs=[pl.BlockSpec((tm, k), lambda i, j: (i, 0)), w_spec, w_spec],
        out_specs=pl.BlockSpec((tm, th), lambda i, j: (i, j)),
        out_shape=jax.ShapeDtypeStruct((m, f), BF16),
        compiler_params=_cparams(2), name=name)(a, wg, wu)


def _ep_group_norm(acc, aux, outs, *, group):
    gain = aux[0][...]
    tn = acc.shape[1]
    ones = _group_ones(group)
    for c in range(tn // LANES):
        x = acc[:, c * LANES:(c + 1) * LANES]
        g = gain[:, c * LANES:(c + 1) * LANES]
        r = lax.rsqrt(_group_sumsq(x, ones) * (1.0 / group) + EPS)
        outs[0][:, c * LANES:(c + 1) * LANES] = (x * r * g).astype(outs[0].dtype)


def _ep_transpose_heads(acc, aux, outs, *, t):
    tm, tn = acc.shape
    ones = jnp.ones((ONES_ROWS, t), outs[0].dtype)
    for hh in range(tn // LANES):
        for s in range(tm // t):
            blk = acc[s * t:(s + 1) * t, hh * LANES:(hh + 1) * LANES]
            outs[0][hh, s, :LANES, :] = blk.T.astype(outs[0].dtype)
            outs[0][hh, s, LANES:, :] = ones


def _ep_latent_norms(acc, aux, outs, *, n_q):
    gain = aux[0][...]
    n = acc.shape[1]
    xq = acc[:, :n_q]
    xk = acc[:, n_q:]
    rq = lax.rsqrt(_sumsq_lanes(xq) * (1.0 / n_q) + EPS)
    rk = lax.rsqrt(_sumsq_lanes(xk) * (1.0 / (n - n_q)) + EPS)
    outs[0][:, :n_q] = (xq * rq * gain[:, :n_q]).astype(outs[0].dtype)
    outs[0][:, n_q:] = (xk * rk * gain[:, n_q:]).astype(outs[0].dtype)


def _rotate_half_matrix(half):
    src = lax.broadcasted_iota(jnp.int32, (LANES, LANES), 0)
    dst = lax.broadcasted_iota(jnp.int32, (LANES, LANES), 1)
    neg = (dst < half) & (src == dst + half)
    pos = (dst >= half) & (dst < 2 * half) & (src == dst - half)
    return jnp.where(neg, -1.0, jnp.where(pos, 1.0, 0.0)).astype(BF16)


def _rope_half_block(y, cos, sin, half):
    lane = lax.broadcasted_iota(jnp.int32, (1, LANES), 1)
    ra = pltpu.roll(y, half, axis=1)
    rb = pltpu.roll(y, LANES - half, axis=1)
    rot = jnp.where(lane < half, -rb, ra)
    return y * cos + rot * sin


def _ep_tail(acc, aux, outs, *, rope_dim, n_gate):
    gain = aux[0][...]
    bias = aux[1][...]
    cos = aux[2][...]
    sin = aux[3][...]
    shift_col = aux[4][...]
    lane = lax.broadcasted_iota(jnp.int32, (1, LANES), 1)
    is_pe = lane < rope_dim
    xpe = jnp.where(is_pe, acc, 0.0)
    r = lax.rsqrt(_sumsq_lanes(xpe) * (1.0 / rope_dim) + EPS)
    y = xpe * r * gain
    outs[0][...] = (_rope_half_block(y, cos, sin, rope_dim // 2) + shift_col).astype(outs[0].dtype)
    z = acc + bias
    logsig = -(jnp.maximum(-z, 0.0) + jnp.log(1.0 + jnp.exp(-jnp.abs(z))))
    is_gate = (lane >= rope_dim) & (lane < rope_dim + n_gate)
    outs[1][...] = jnp.where(is_gate, logsig, 0.0)


HEAD_W = 2 * LANES


def _ep_latent_q(acc, aux, outs, *, rope_dim):
    g_nope = aux[0][...]
    g_rope = aux[1][...]
    cos = aux[2][...]
    sin = aux[3][...]
    one_col = aux[4][...]
    ones = _group_ones(LANES)
    rot_mat = _rotate_half_matrix(rope_dim // 2)
    for hh in range(acc.shape[1] // HEAD_W):
        xn = acc[:, hh * HEAD_W:hh * HEAD_W + LANES]
        xr = acc[:, hh * HEAD_W + LANES:(hh + 1) * HEAD_W]
        rn = lax.rsqrt(_group_sumsq(xn, ones) * (1.0 / LANES) + EPS)
        outs[0][:, hh * HEAD_W:hh * HEAD_W + LANES] = (xn * rn * g_nope).astype(outs[0].dtype)
        rr = lax.rsqrt(_group_sumsq(xr, ones) * (1.0 / rope_dim) + EPS)
        y = xr * rr * g_rope
        rot = jnp.dot(y.astype(BF16), rot_mat, preferred_element_type=F32)
        outs[0][:, hh * HEAD_W + LANES:(hh + 1) * HEAD_W] = (y * cos + rot * sin + one_col).astype(outs[0].dtype)


def _ep_latent_k(acc, aux, outs):
    g_nope = aux[0][...]
    kpe = aux[1][...]
    for hh in range(acc.shape[1] // LANES):
        xn = acc[:, hh * LANES:(hh + 1) * LANES]
        rn = lax.rsqrt(_sumsq_lanes(xn) * (1.0 / LANES) + EPS)
        outs[0][:, hh * HEAD_W:hh * HEAD_W + LANES] = (xn * rn * g_nope).astype(outs[0].dtype)
        outs[0][:, hh * HEAD_W + LANES:(hh + 1) * HEAD_W] = kpe


def _norm_body(x_ref, g_ref, h_ref):
    x = x_ref[...]
    d = x.shape[1]
    r = lax.rsqrt(_sumsq_lanes(x) * (1.0 / d) + EPS)
    h_ref[...] = (x * r * g_ref[...]).astype(h_ref.dtype)


def _rms_norm(x, gain, *, tm, name):
    t, d = x.shape
    return pl.pallas_call(
        _norm_body, grid=(t // tm,),
        in_specs=[pl.BlockSpec((tm, d), lambda i: (i, 0)), pl.BlockSpec((1, d), lambda i: (0, 0))],
        out_specs=pl.BlockSpec((tm, d), lambda i: (i, 0)),
        out_shape=jax.ShapeDtypeStruct((t, d), BF16),
        compiler_params=_cparams(1), name=name)(x, gain.reshape(1, d))


def _add_norm_body(x_ref, d_ref, g_ref, xo_ref, h_ref):
    x = x_ref[...] + d_ref[...].astype(F32)
    xo_ref[...] = x
    r = lax.rsqrt(_sumsq_lanes(x) * (1.0 / x.shape[1]) + EPS)
    h_ref[...] = (x * r * g_ref[...]).astype(h_ref.dtype)


def _add_rms_norm(x, delta, gain, *, tm, name):
    t, d = x.shape
    tile = pl.BlockSpec((tm, d), lambda i: (i, 0))
    return pl.pallas_call(
        _add_norm_body, grid=(t // tm,),
        in_specs=[tile, tile, pl.BlockSpec((1, d), lambda i: (0, 0))],
        out_specs=[tile, tile],
        out_shape=[jax.ShapeDtypeStruct((t, d), F32), jax.ShapeDtypeStruct((t, d), BF16)],
        compiler_params=_cparams(1), name=name)(x, delta, gain.reshape(1, d))


def _add_body(x_ref, d_ref, o_ref):
    o_ref[...] = x_ref[...] + d_ref[...].astype(F32)


def _residual_add(x, delta, *, tm):
    t, d = x.shape
    tile = pl.BlockSpec((tm, d), lambda i: (i, 0))
    return pl.pallas_call(
        _add_body, grid=(t // tm,), in_specs=[tile, tile], out_specs=tile,
        out_shape=jax.ShapeDtypeStruct((t, d), F32),
        compiler_params=_cparams(1), name="residual_add")(x, delta)


ROUTE_I1, ROUTE_I2, ROUTE_G1, ROUTE_G2, ROUTE_R1, ROUTE_R2 = range(6)


def _norm_router_body(x_ref, g_ref, wr_ref, h_ref, route_ref, count_ref, carry_ref, *, n_exp):
    @pl.when(pl.program_id(0) == 0)
    def _():
        carry_ref[...] = jnp.zeros_like(carry_ref)

    x = x_ref[...]
    tm, d = x.shape
    r = lax.rsqrt(_sumsq_lanes(x) * (1.0 / d) + EPS)
    h = x * r * g_ref[...]
    h_ref[...] = h.astype(h_ref.dtype)
    logits = jnp.dot(h, wr_ref[...], preferred_element_type=F32, precision=_HI)
    lane = lax.broadcasted_iota(jnp.int32, logits.shape, 1).astype(F32)
    lg = jnp.where(lane < n_exp, logits, -jnp.inf)
    m1 = jnp.max(lg, axis=-1, keepdims=True)
    i1 = jnp.min(jnp.where(lg == m1, lane, float(LANES)), axis=-1, keepdims=True)
    lg2 = jnp.where(lane == i1, -jnp.inf, lg)
    m2 = jnp.max(lg2, axis=-1, keepdims=True)
    i2 = jnp.min(jnp.where(lg2 == m2, lane, float(LANES)), axis=-1, keepdims=True)
    e2 = jnp.exp(m2 - m1)
    g1 = 1.0 / (1.0 + e2)
    g2 = e2 * g1

    chosen = jnp.where((lane == i1) | (lane == i2), 1.0, 0.0)
    rr = lax.broadcasted_iota(jnp.int32, (tm, tm), 0)
    cc = lax.broadcasted_iota(jnp.int32, (tm, tm), 1)
    strict = jnp.where(rr > cc, 1.0, 0.0).astype(BF16)
    before = jnp.dot(strict, chosen.astype(BF16), preferred_element_type=F32) + carry_ref[...]
    rank1 = jnp.sum(jnp.where(lane == i1, before, 0.0), axis=-1, keepdims=True)
    rank2 = jnp.sum(jnp.where(lane == i2, before, 0.0), axis=-1, keepdims=True)
    carry_ref[...] += jnp.sum(chosen, axis=0, keepdims=True)
    count_ref[...] = carry_ref[...]

    rec = jnp.zeros((tm, LANES), F32)
    for k, v in ((ROUTE_I1, i1), (ROUTE_I2, i2), (ROUTE_G1, g1), (ROUTE_G2, g2), (ROUTE_R1, rank1), (ROUTE_R2, rank2)):
        rec = jnp.where(lane == float(k), v, rec)
    route_ref[...] = rec


def _rms_norm_router(x, gain, router, *, tm, name):
    t, d = x.shape
    n_exp = router.shape[1]
    wr = jnp.zeros((d, LANES), F32).at[:, :n_exp].set(router.astype(F32))
    return pl.pallas_call(
        functools.partial(_norm_router_body, n_exp=n_exp), grid=(t // tm,),
        in_specs=[pl.BlockSpec((tm, d), lambda i: (i, 0)), pl.BlockSpec((1, d), lambda i: (0, 0)),
                  pl.BlockSpec((d, LANES), lambda i: (0, 0))],
        out_specs=[pl.BlockSpec((tm, d), lambda i: (i, 0)),
                   pl.BlockSpec((tm, LANES), lambda i: (i, 0)),
                   pl.BlockSpec((1, LANES), lambda i: (0, 0))],
        out_shape=[jax.ShapeDtypeStruct((t, d), BF16),
                   jax.ShapeDtypeStruct((t, LANES), F32),
                   jax.ShapeDtypeStruct((1, LANES), F32)],
        scratch_shapes=[pltpu.VMEM((1, LANES), F32)],
        compiler_params=_cparams(1), name=name)(x, gain.reshape(1, d), wr)


def _row_view(a):
    rows, d = a.shape
    return a.reshape(rows, d // LANES, LANES)


def _dispatch_body(src_ref, h_ref, o_ref, sem, *, tm):
    def issue(pair, carry):
        for prio in range(2):
            r = 2 * pair + prio
            pltpu.make_async_copy(h_ref.at[src_ref[0, 0, r]], o_ref.at[r], sem).start(priority=prio)
        return carry

    def drain(r, carry):
        pltpu.make_async_copy(h_ref.at[0], o_ref.at[0], sem).wait()
        return carry

    lax.fori_loop(0, tm // 2, issue, 0)
    lax.fori_loop(0, tm, drain, 0)


def _dispatch(h_rows, src, *, tm):
    n_tiles = src.shape[0]
    chunks = h_rows.shape[1]
    return pl.pallas_call(
        functools.partial(_dispatch_body, tm=tm), grid=(n_tiles,),
        in_specs=[pl.BlockSpec((1, 1, tm), lambda i: (i, 0, 0), memory_space=pltpu.SMEM),
                  pl.BlockSpec(memory_space=pl.ANY)],
        out_specs=pl.BlockSpec((tm, chunks, LANES), lambda i: (i, 0, 0)),
        out_shape=jax.ShapeDtypeStruct((n_tiles * tm, chunks, LANES), h_rows.dtype),
        scratch_shapes=[pltpu.SemaphoreType.DMA(())],
        compiler_params=_cparams(1), name="moe_dispatch")(src, h_rows)


def _combine_body(pos_ref, gate_ref, ys_ref, o_ref, buf, sem, *, tm):
    def issue(r, carry):
        pltpu.make_async_copy(ys_ref.at[pos_ref[0, 0, r]], buf.at[0, r], sem).start(priority=0)
        pltpu.make_async_copy(ys_ref.at[pos_ref[0, 0, tm + r]], buf.at[1, r], sem).start(priority=1)
        return carry

    def drain(r, carry):
        pltpu.make_async_copy(ys_ref.at[0], buf.at[0, 0], sem).wait()
        pltpu.make_async_copy(ys_ref.at[0], buf.at[1, 0], sem).wait()
        return carry

    def mix(r, carry):
        o_ref[r] = (gate_ref[0, 0, r] * buf[0, r].astype(F32)
                    + gate_ref[0, 0, tm + r] * buf[1, r].astype(F32)).astype(o_ref.dtype)
        return carry

    lax.fori_loop(0, tm, issue, 0)
    lax.fori_loop(0, tm, drain, 0)
    lax.fori_loop(0, tm, mix, 0)


def _combine(ys_rows, pos, gates, *, tm):
    n_tiles = pos.shape[0]
    chunks = ys_rows.shape[1]
    smem_tile = pl.BlockSpec((1, 1, 2 * tm), lambda i: (i, 0, 0), memory_space=pltpu.SMEM)
    return pl.pallas_call(
        functools.partial(_combine_body, tm=tm), grid=(n_tiles,),
        in_specs=[smem_tile, smem_tile, pl.BlockSpec(memory_space=pl.ANY)],
        out_specs=pl.BlockSpec((tm, chunks, LANES), lambda i: (i, 0, 0)),
        out_shape=jax.ShapeDtypeStruct((n_tiles * tm, chunks, LANES), ys_rows.dtype),
        scratch_shapes=[pltpu.VMEM((2, tm, chunks, LANES), ys_rows.dtype), pltpu.SemaphoreType.DMA(())],
        compiler_params=_cparams(1), name="moe_combine")(pos, gates, ys_rows)


def _grouped_up_body(te_ref, nv_ref, a_ref, wg_ref, wu_ref, o_ref):
    @pl.when(pl.program_id(1) < nv_ref[0])
    def _():
        a = a_ref[...]
        g = jnp.dot(a, wg_ref[...], preferred_element_type=F32)
        u = jnp.dot(a, wu_ref[...], preferred_element_type=F32)
        o_ref[...] = (g * (1.0 / (1.0 + jnp.exp(-g))) * u).astype(o_ref.dtype)

    @pl.when(pl.program_id(1) >= nv_ref[0])
    def _():
        o_ref[...] = jnp.zeros_like(o_ref)


def _grouped_down_body(te_ref, nv_ref, a_ref, w_ref, o_ref):
    @pl.when(pl.program_id(1) < nv_ref[0])
    def _():
        o_ref[...] = jnp.dot(a_ref[...], w_ref[...], preferred_element_type=F32).astype(o_ref.dtype)

    @pl.when(pl.program_id(1) >= nv_ref[0])
    def _():
        o_ref[...] = jnp.zeros_like(o_ref)


def _grouped_matmul(body, a, weights, tile_expert, n_valid, *, tmo, tn, name):
    r, k = a.shape
    n = weights[0].shape[2]

    def row(j, i, te, nv):
        return jnp.minimum(i, nv[0] - 1)

    grid_spec = pltpu.PrefetchScalarGridSpec(
        num_scalar_prefetch=2, grid=(n // tn, r // tmo),
        in_specs=[pl.BlockSpec((tmo, k), lambda j, i, te, nv: (row(j, i, te, nv), 0))]
        + [pl.BlockSpec((None, k, tn), lambda j, i, te, nv: (te[row(j, i, te, nv)], 0, j)) for _ in weights],
        out_specs=pl.BlockSpec((tmo, tn), lambda j, i, te, nv: (i, j)))
    return pl.pallas_call(
        body, grid_spec=grid_spec, out_shape=jax.ShapeDtypeStruct((r, n), BF16),
        compiler_params=_cparams(2), name=name)(tile_expert, n_valid, a, *weights)


def _rope_body(pos_ref, f_ref, cos_ref, sin_ref):
    ang = pos_ref[...] * f_ref[...]
    live = f_ref[...] > 0.0
    cos_ref[...] = jnp.where(live, jnp.cos(ang), 0.0)
    sin_ref[...] = jnp.where(live, jnp.sin(ang), 0.0)


def _rope_tables(positions, *, half, tm):
    t = positions.size
    inv_freq = ROPE_BASE ** (-jnp.arange(half, dtype=F32) / half)
    f_row = jnp.concatenate([inv_freq, inv_freq, jnp.zeros((LANES - 2 * half,), F32)]).reshape(1, LANES)
    pos_rep = jnp.broadcast_to(positions.astype(F32).reshape(t, 1), (t, LANES))
    spec = pl.BlockSpec((tm, LANES), lambda i: (i, 0))
    return pl.pallas_call(
        _rope_body, grid=(t // tm,),
        in_specs=[spec, pl.BlockSpec((1, LANES), lambda i: (0, 0))],
        out_specs=[spec, spec],
        out_shape=[jax.ShapeDtypeStruct((t, LANES), F32)] * 2,
        compiler_params=_cparams(1), name="rope_tables")(pos_rep, f_row)


def _bias_body(rb_ref, o_ref, *, t):
    h = pl.program_id(0)
    d = pl.program_id(1)
    j = lax.broadcasted_iota(jnp.int32, (t, t), 0)
    i = lax.broadcasted_iota(jnp.int32, (t, t), 1)
    rel = d * t + i - j
    n = jnp.maximum(rel, 0)
    max_exact = NUM_BUCKETS // 2
    large = max_exact + (
        jnp.log(jnp.maximum(n, max_exact).astype(F32) / max_exact)
        / math.log(MAX_DISTANCE / max_exact) * (NUM_BUCKETS - max_exact)
    ).astype(jnp.int32)
    large = jnp.minimum(large, NUM_BUCKETS - 1)
    bucket = jnp.where(n < max_exact, n, large)
    far = rb_ref[NUM_BUCKETS - 1, h]
    val = jnp.zeros((t, t), F32)
    for b in range(NUM_BUCKETS - 1):
        val = jnp.where(bucket == b, rb_ref[b, h] - far, val)
    o_ref[...] = jnp.where(rel >= 0, val * LOG2E, NEG)


def _bias_tiles(rel_bias, *, t):
    assert t + 1 >= MAX_DISTANCE, "tiles beyond the first sub-diagonal must be in the last bucket"
    nh = rel_bias.shape[1]
    return pl.pallas_call(
        functools.partial(_bias_body, t=t), grid=(nh, 2),
        in_specs=[pl.BlockSpec(memory_space=pltpu.SMEM)],
        out_specs=pl.BlockSpec((None, None, t, t), lambda h, d: (h, d, 0, 0)),
        out_shape=jax.ShapeDtypeStruct((nh, 2, t, t), F32),
        compiler_params=_cparams(2), name="t5_bias_tiles")(rel_bias.astype(F32))


N_PIECES = 3


def _split_pieces(x):
    pieces = []
    rest = x
    for _ in range(N_PIECES):
        p = rest.astype(BF16).astype(F32)
        pieces.append(p)
        rest = rest - p
    return pieces


def _cum_body(shift_ref, lf_ref, o_ref, end_ref, carry_ref, *, n_heads, lane0):
    @pl.when(pl.program_id(1) == 0)
    def _():
        carry_ref[...] = jnp.zeros_like(carry_ref)

    x = lf_ref[...]
    tb = x.shape[0]
    r = lax.broadcasted_iota(jnp.int32, (tb, tb), 0)
    c = lax.broadcasted_iota(jnp.int32, (tb, tb), 1)
    tri = jnp.where(r >= c, 1.0, 0.0).astype(F32)
    cum = jnp.dot(tri, x, preferred_element_type=F32, precision=_HI) + carry_ref[...]
    carry_ref[...] = cum[tb - 1:tb, :]
    cum2 = cum * LOG2E
    end_ref[0] = cum2[tb - 1:tb, :]
    lane = lax.broadcasted_iota(jnp.int32, (tb, LANES), 1)
    shift = shift_ref[0]
    for hh in range(n_heads):
        col = cum2[:, lane0 + hh:lane0 + hh + 1]
        qcols = jnp.where(lane < N_PIECES, 1.0, 0.0)
        kcols = jnp.where((lane >= N_PIECES) & (lane < 2 * N_PIECES), 1.0, 0.0)
        for n, (pq, pk) in enumerate(zip(_split_pieces(col - shift), _split_pieces(-col))):
            qcols = jnp.where(lane == N_PIECES + n, pq, qcols)
            kcols = jnp.where(lane == n, pk, kcols)
        o_ref[:, hh * LANES:(hh + 1) * LANES] = qcols.astype(o_ref.dtype)
        o_ref[:, (n_heads + hh) * LANES:(n_heads + hh + 1) * LANES] = kcols.astype(o_ref.dtype)


def _forget_columns(logf, shift, *, batch, seq, n_heads, lane0, tb):
    nb = seq // tb
    width = 2 * n_heads * LANES
    return pl.pallas_call(
        functools.partial(_cum_body, n_heads=n_heads, lane0=lane0), grid=(batch, nb),
        in_specs=[pl.BlockSpec(memory_space=pltpu.SMEM),
                  pl.BlockSpec((tb, LANES), lambda b, i: (b * nb + i, 0))],
        out_specs=[pl.BlockSpec((tb, width), lambda b, i: (b * nb + i, 0)),
                   pl.BlockSpec((1, 1, LANES), lambda b, i: (b * nb + i, 0, 0))],
        out_shape=[jax.ShapeDtypeStruct((batch * seq, width), BF16),
                   jax.ShapeDtypeStruct((batch * nb, 1, LANES), F32)],
        scratch_shapes=[pltpu.VMEM((1, LANES), F32)],
        compiler_params=_cparams(2), name="forget_cumsum")(shift.reshape(1).astype(F32), logf)


def _online_block(s, vb, m_ref, acc_ref):
    m_prev = m_ref[...]
    m_new = jnp.maximum(m_prev, jnp.max(s, axis=0, keepdims=True))
    alpha = jnp.exp2(m_prev - m_new)
    p = jnp.exp2(s - m_new).astype(BF16)
    acc_ref[...] = alpha * acc_ref[...] + jnp.dot(vb, p, preferred_element_type=F32)
    m_ref[...] = m_new


def _shifted_block(s, vb, acc_ref):
    p = jnp.exp2(s).astype(BF16)
    acc_ref[...] += jnp.dot(vb, p, preferred_element_type=F32)


def _causal_mask(t):
    j = lax.broadcasted_iota(jnp.int32, (t, t), 0)
    i = lax.broadcasted_iota(jnp.int32, (t, t), 1)
    return j <= i


_NT = (((1,), (1,)), ((), ()))
FAR_UNROLL = 4


def _init_state(m_ref, acc_ref):
    m_ref[...] = jnp.full(m_ref.shape, NEG, F32)
    acc_ref[...] = jnp.zeros(acc_ref.shape, F32)


def _attn_body(*refs, mode, t, hp, dq, dv, qk_dim=None, lambda_init=None):
    safe_ref, start_ref, q_ref, k_ref, vt_ref = refs[:5]
    if mode == "latent":
        o_ref, qc_ref, m_ref, acc_ref = refs[5:]
    elif mode == "forget":
        qx_ref, kx_ref, o_ref, qc_ref, m_ref, acc_ref = refs[5:]
    else:
        qx_ref, kx_ref, bias_ref, lam_ref, hn_ref, o_ref, qc_ref, m_ref, acc_ref = refs[5:]
    n_map = 2 if mode == "diff" else 1
    qi = pl.program_id(2)
    step = (pl.program_id(0) * pl.num_programs(1) + pl.program_id(1)) * pl.num_programs(2) + qi
    _init_state(m_ref, acc_ref)

    for hh in range(hp):
        q = q_ref[:, hh * dq:(hh + 1) * dq]
        if mode == "latent":
            qc_ref[hh] = q
        elif mode == "forget":
            qc_ref[hh] = jnp.concatenate([q, qx_ref[:, hh * LANES:(hh + 1) * LANES]], axis=1)
        else:
            lane = lax.broadcasted_iota(jnp.int32, (t, dq), 1)
            zero = jnp.zeros_like(q)
            qc_ref[2 * hh] = jnp.concatenate([jnp.where(lane < qk_dim, q, zero), qx_ref[...]], axis=1)
            qc_ref[2 * hh + 1] = jnp.concatenate([jnp.where(lane < qk_dim, zero, q), qx_ref[...]], axis=1)

    def keys(kj, hh):
        start = pl.multiple_of(kj * t, t)
        kb = k_ref[pl.ds(start, t), hh * dq:(hh + 1) * dq]
        if mode == "forget":
            kb = jnp.concatenate([kb, kx_ref[pl.ds(start, t), hh * LANES:(hh + 1) * LANES]], axis=1)
        elif mode == "diff":
            kb = jnp.concatenate([kb, kx_ref[hh]], axis=1)
        return kb

    def run(update, skip_dead):
        def scores(kj, kind, hh, mm):
            s = lax.dot_general(keys(kj, hh), qc_ref[n_map * hh + mm], _NT, preferred_element_type=F32)
            if mode == "diff":
                if kind != "far":
                    s = s + bias_ref[hh, 0 if kind == "diag" else 1]
            elif kind == "diag":
                s = jnp.where(_causal_mask(t), s, NEG)
            return s

        def blocks(items):
            work = [(kj, kind, hh, mm) for kj, kind in items for hh in range(hp) for mm in range(n_map)]
            s_next = scores(*work[0])
            for n, (kj, kind, hh, mm) in enumerate(work):
                s_cur = s_next
                if n + 1 < len(work):
                    s_next = scores(*work[n + 1])
                update(s_cur, vt_ref[hh, kj], n_map * hh + mm)

        n_far = jnp.maximum(qi - 1, 0) if mode == "diff" else qi
        k0 = jnp.minimum(start_ref[step], n_far) if skip_dead else 0
        near = [(qi, "diag")] + ([(qi - 1, "sub")] if mode == "diff" else [])
        n_tail = FAR_UNROLL - len(near)
        if skip_dead:
            merge = (n_far - k0 >= n_tail) & (qi >= len(near) - 1)
        else:
            merge = qi < 0
        n_loop = jnp.where(merge, n_far - k0 - n_tail, n_far - k0)
        n_groups = n_loop // FAR_UNROLL

        def far_group(g, carry):
            blocks([(k0 + g * FAR_UNROLL + u, "far") for u in range(FAR_UNROLL)])
            return carry

        def far_single(kj, carry):
            blocks([(kj, "far")])
            return carry

        lax.fori_loop(0, n_groups, far_group, 0)
        lax.fori_loop(k0 + n_groups * FAR_UNROLL, k0 + n_loop, far_single, 0)

        if skip_dead:
            @pl.when(merge)
            def _():
                blocks([(n_far - 1 - u, "far") for u in range(n_tail)] + near[::-1])

        @pl.when(jnp.logical_not(merge))
        def _():
            if mode == "diff":
                @pl.when(qi >= 1)
                def _():
                    blocks([(qi - 1, "sub")])
            blocks([(qi, "diag")])

    @pl.when(safe_ref[0] != 0)
    def _():
        run(lambda s, vb, c: _shifted_block(s, vb, acc_ref.at[c]), True)

    @pl.when(safe_ref[0] == 0)
    def _():
        run(lambda s, vb, c: _online_block(s, vb, m_ref.at[c], acc_ref.at[c]), False)

    if mode == "diff":
        lp = lam_ref[...]
        lam = (jnp.exp(jnp.sum(lp[0:1] * lp[1:2], axis=-1, keepdims=True))
               - jnp.exp(jnp.sum(lp[2:3] * lp[3:4], axis=-1, keepdims=True)) + lambda_init)
    for hh in range(hp):
        if mode == "diff":
            a0 = acc_ref[2 * hh]
            a1 = acc_ref[2 * hh + 1]
            o = a0[:dv] * (1.0 / a0[dv:dv + 1]) - lam * (a1[:dv] * (1.0 / a1[dv:dv + 1]))
            r = lax.rsqrt(jnp.sum(o * o, axis=0, keepdims=True) * (1.0 / dv) + EPS)
            o = o * r * hn_ref[...] * (1.0 - lambda_init)
        else:
            a = acc_ref[hh]
            o = a[:dv] * (1.0 / a[dv:dv + 1])
        o_ref[:, hh * dv:(hh + 1) * dv] = o.T.astype(o_ref.dtype)


def _attention(mode, safe, q_arr, k_arr, vt_arr, *, batch, seq, n_heads, t, hp, dq, q_col0, k_col0,
               extra=(), extra_specs=(), body_kwargs=None, first_block=None, name):
    nq = seq // t
    dvp = vt_arr.shape[2]
    dv = dvp - ONES_ROWS
    tokens = batch * seq
    assert n_heads % hp == 0 and q_col0 % hp == 0 and k_col0 % hp == 0
    qc, kc = q_col0 // hp, k_col0 // hp
    if first_block is None:
        first_block = jnp.zeros((batch * (n_heads // hp) * nq,), jnp.int32)
    in_specs = [
        pl.BlockSpec(memory_space=pltpu.SMEM),
        pl.BlockSpec(memory_space=pltpu.SMEM),
        pl.BlockSpec((t, hp * dq), lambda b, h, i: (b * nq + i, qc + h)),
        pl.BlockSpec((seq, hp * dq), lambda b, h, i: (b, kc + h)),
        pl.BlockSpec((hp, nq, dvp, t), lambda b, h, i: (h, b, 0, 0)),
        *extra_specs,
    ]
    n_chain = 2 * hp if mode == "diff" else hp
    scratch = [pltpu.VMEM((n_chain, t, 2 * LANES), BF16),
               pltpu.VMEM((n_chain, 1, t), F32), pltpu.VMEM((n_chain, dvp, t), F32)]
    body = functools.partial(_attn_body, mode=mode, t=t, hp=hp, dq=dq, dv=dv, **(body_kwargs or {}))
    return pl.pallas_call(
        body, grid=(batch, n_heads // hp, nq), in_specs=in_specs,
        out_specs=pl.BlockSpec((t, hp * dv), lambda b, h, i: (b * nq + i, h)),
        out_shape=jax.ShapeDtypeStruct((tokens, n_heads * dv), BF16),
        scratch_shapes=scratch, compiler_params=_cparams(3), name=name)(
            safe, first_block, q_arr, k_arr, vt_arr, *extra)


def _pick(n, prefs):
    for p in prefs:
        if n % p == 0:
            return p
    raise ValueError((n, prefs))


SKIP_LOG2 = 160.0
SAFE_DEPTH = 110.0


def _round_up_bf16(c):
    return (c * (1.0 + 2.0 ** -6)).astype(BF16).astype(F32)


def _safe_flag(depth):
    return (depth <= SAFE_DEPTH).astype(jnp.int32).reshape(1)


def kernel(x, positions, attn_norm, w_in, b_forget, a_qk_norm, a_lambda, a_head_norm, rel_bias,
           b_qk_norm, c_q_a_norm, c_kv_a_norm, c_w_uq, c_w_ukv, c_qk_norm_nope, c_qk_norm_rope,
           w_out, ffn_norm, dense_w_gate, dense_w_up, dense_w_down, moe_router, moe_w_gate,
           moe_w_up, moe_w_down):
    batch, seq, d_model = x.shape
    depth = w_in.shape[0]
    tokens = batch * seq
    n_slots = d_model // LANES
    a_heads = n_slots // 4
    b_heads = n_slots // 4
    c_heads = n_slots // 2
    a_qk = a_qk_norm.shape[-1]
    b_dim = b_qk_norm.shape[-1]
    c_q_lora = c_q_a_norm.shape[-1]
    c_kv_lora = c_kv_a_norm.shape[-1]
    c_nope = c_qk_norm_nope.shape[-1]
    c_rope = c_qk_norm_rope.shape[-1]
    c_v = c_w_ukv.shape[-1] // c_heads - c_nope
    n_exp = moe_router.shape[-1]
    assert 2 * a_qk == LANES and b_dim == LANES and c_nope == LANES and c_v == LANES and 2 * c_rope == LANES

    t = _pick(seq, (512, 256, 128))
    tm = _pick(tokens, (1024, 512, 256))
    tm_n = _pick(tokens, (256, 128))
    hp_a, hp_b, hp_c = 1, 2, 2

    sizes = (a_heads * 2 * a_qk, a_heads * 2 * a_qk, a_heads * LANES, b_heads * b_dim, b_heads * b_dim,
             b_heads * b_dim, b_heads, c_q_lora, c_kv_lora, c_rope)
    offs = [0]
    for s in sizes:
        offs.append(offs[-1] + s)
    o_aq, o_ak, o_av, o_bq, o_bk, o_bv, o_f, o_cq, o_ckv, o_kpe, o_end = offs

    xf = x.reshape(tokens, d_model)
    cos_t, sin_t = _rope_tables(positions, half=c_rope // 2, tm=tm)
    bias_tiles = _bias_tiles(rel_bias, t=t)
    bias_log2 = (rel_bias.astype(F32) - rel_bias[-1:].astype(F32)) * LOG2E
    bias_max = jnp.max(bias_log2, axis=0)
    qx_a = jnp.zeros((t, LANES), F32).at[:, 0].set(1.0).astype(BF16)

    row_spec = lambda width: pl.BlockSpec((1, width), lambda i, j: (0, j))
    fixed_row = pl.BlockSpec((1, LANES), lambda i, j: (0, 0))
    tok_lane = pl.BlockSpec((tm, LANES), lambda i, j: (i, 0))

    def vt_call(h_bf, w, nh, name, a_cols=None, w_cols=None):
        tn = _pick(nh * LANES, (512, 256, 128))
        return _matmul(
            h_bf, w, a_cols=a_cols, w_cols=w_cols, tm=tm, tn=tn,
            epilogue=functools.partial(_ep_transpose_heads, t=t),
            out_shape=jax.ShapeDtypeStruct((nh, tokens // t, LANES + ONES_ROWS, t), BF16),
            out_specs=pl.BlockSpec((tn // LANES, tm // t, LANES + ONES_ROWS, t), lambda i, j: (j, i, 0, 0)),
            name=name)

    pending = None
    for layer in range(depth):
        lambda_init = 0.8 - 0.6 * math.exp(-0.3 * layer)
        w = w_in[layer].astype(BF16)

        if pending is None:
            h = _rms_norm(xf, attn_norm[layer], tm=tm_n, name="attn_norm")
        else:
            xf, h = _add_rms_norm(xf, pending, attn_norm[layer], tm=tm_n, name="attn_norm")
            pending = None

        a_scale = a_qk ** -0.5 * LOG2E
        gain_a = jnp.concatenate([jnp.tile(a_qk_norm[layer, 0] * a_scale, 2 * a_heads),
                                  jnp.tile(a_qk_norm[layer, 1], 2 * a_heads)]).reshape(1, -1)
        tn = _pick(o_av - o_aq, (1024, 512))
        qk_a = _matmul(h, w, w_cols=(o_aq, o_av - o_aq), tm=tm, tn=tn,
                       epilogue=functools.partial(_ep_group_norm, group=a_qk),
                       aux=(gain_a,), aux_specs=(row_spec(tn),),
                       out_shape=jax.ShapeDtypeStruct((tokens, o_av - o_aq), BF16),
                       out_specs=pl.BlockSpec((tm, tn), lambda i, j: (i, j)), name="proj_a_qk")
        vt_a = vt_call(h, w, a_heads, "proj_a_v", w_cols=(o_av, o_bq - o_av))
        hn = jnp.broadcast_to(a_head_norm[layer].reshape(LANES, 1), (LANES, t))
        bound_a = a_qk * jnp.max(jnp.abs(a_qk_norm[layer, 0] * a_scale)) * jnp.max(jnp.abs(a_qk_norm[layer, 1]))
        shift_a = _round_up_bf16(bound_a + bias_max)
        safe_a = _safe_flag(2.0 * bound_a + jnp.max(bias_max - bias_log2[0]))
        kx_a = jnp.zeros((a_heads, t, LANES), F32).at[:, :, 0].set(-shift_a[:, None]).astype(BF16)
        o_a = _attention(
            "diff", safe_a, qk_a, qk_a, vt_a, batch=batch, seq=seq, n_heads=a_heads, t=t, hp=hp_a, dq=LANES,
            q_col0=0, k_col0=a_heads,
            extra=(qx_a, kx_a, bias_tiles, a_lambda[layer], hn),
            extra_specs=(pl.BlockSpec((t, LANES), lambda b, hh, i: (0, 0)),
                         pl.BlockSpec((hp_a, t, LANES), lambda b, hh, i: (hh, 0, 0)),
                         pl.BlockSpec((hp_a, 2, t, t), lambda b, hh, i: (hh, 0, 0, 0)),
                         pl.BlockSpec((4, a_qk), lambda b, hh, i: (0, 0)),
                         pl.BlockSpec((LANES, t), lambda b, hh, i: (0, 0))),
            body_kwargs=dict(qk_dim=a_qk, lambda_init=lambda_init), name="attn_diff")

        b_scale = b_dim ** -0.5 * LOG2E
        gain_b = jnp.concatenate([jnp.tile(b_qk_norm[layer, 0] * b_scale, b_heads),
                                  jnp.tile(b_qk_norm[layer, 1], b_heads)]).reshape(1, -1)
        qk_b = _matmul(h, w, w_cols=(o_bq, o_bv - o_bq), tm=tm, tn=tn,
                       epilogue=functools.partial(_ep_group_norm, group=b_dim),
                       aux=(gain_b,), aux_specs=(row_spec(tn),),
                       out_shape=jax.ShapeDtypeStruct((tokens, o_bv - o_bq), BF16),
                       out_specs=pl.BlockSpec((tm, tn), lambda i, j: (i, j)), name="proj_b_qk")
        vt_b = vt_call(h, w, b_heads, "proj_b_v", w_cols=(o_bv, o_f - o_bv))

        w_tail = jnp.concatenate([w[:, o_kpe:o_end], w[:, o_f:o_cq],
                                  jnp.zeros((d_model, LANES - c_rope - b_heads), BF16)], axis=1)
        gain_tail = jnp.concatenate([c_qk_norm_rope[layer, 1], jnp.zeros((LANES - c_rope,), F32)]).reshape(1, LANES)
        bias_tail = jnp.zeros((LANES,), F32).at[c_rope:c_rope + b_heads].set(b_forget[layer]).reshape(1, LANES)
        c_scale = (c_nope + c_rope) ** -0.5 * LOG2E
        gq_n = c_qk_norm_nope[layer, 0] * c_scale
        gq_r = c_qk_norm_rope[layer, 0] * c_scale
        gk_n = c_qk_norm_nope[layer, 1]
        gk_r = c_qk_norm_rope[layer, 1]
        bound_c = (jnp.sqrt(c_nope * jnp.max(gq_n * gq_n) + c_rope * jnp.max(gq_r * gq_r))
                   * jnp.sqrt(c_nope * jnp.max(gk_n * gk_n) + c_rope * jnp.max(gk_r * gk_r)))
        shift_c = _round_up_bf16(bound_c)
        safe_c = _safe_flag(2.0 * bound_c)
        shift_col = jnp.zeros((1, LANES), F32).at[0, c_rope].set(-shift_c)
        one_col = jnp.zeros((1, LANES), F32).at[0, c_rope].set(1.0)
        kpe, logf = _matmul(
            h, w_tail, tm=tm, tn=LANES,
            epilogue=functools.partial(_ep_tail, rope_dim=c_rope, n_gate=b_heads),
            aux=(gain_tail, bias_tail, cos_t, sin_t, shift_col),
            aux_specs=(fixed_row, fixed_row, tok_lane, tok_lane, fixed_row),
            out_shape=[jax.ShapeDtypeStruct((tokens, LANES), BF16), jax.ShapeDtypeStruct((tokens, LANES), F32)],
            out_specs=[tok_lane, tok_lane], name="proj_tail")
        bound_b = b_dim * jnp.max(jnp.abs(b_qk_norm[layer, 0] * b_scale)) * jnp.max(jnp.abs(b_qk_norm[layer, 1]))
        safe_b = _safe_flag(2.0 * bound_b)
        x_b, ends = _forget_columns(logf, bound_b, batch=batch, seq=seq, n_heads=b_heads, lane0=c_rope, tb=t)
        nq = seq // t
        nbh = b_heads // hp_b
        e_b = ends[:, 0, c_rope:c_rope + b_heads].reshape(batch, nq, b_heads)
        e_q = jnp.concatenate([jnp.zeros((batch, 1, b_heads), F32), e_b[:, :-1]], axis=1)
        dead = (e_q[:, :, None, :] - e_b[:, None, :, :]) < -SKIP_LOG2
        dead = dead & (jnp.arange(nq)[None, None, :, None] + 1 < jnp.arange(nq)[None, :, None, None])
        first_b = jnp.sum(dead.astype(jnp.int32), axis=2)
        first_b = jnp.min(first_b.reshape(batch, nq, nbh, hp_b), axis=3)
        first_b = jnp.transpose(first_b, (0, 2, 1)).reshape(-1)
        o_b = _attention(
            "forget", safe_b, qk_b, qk_b, vt_b, batch=batch, seq=seq, n_heads=b_heads, t=t, hp=hp_b, dq=LANES,
            q_col0=0, k_col0=b_heads,
            extra=(x_b, x_b),
            extra_specs=(pl.BlockSpec((t, hp_b * LANES), lambda b, hh, i: (b * nq + i, hh)),
                         pl.BlockSpec((seq, hp_b * LANES), lambda b, hh, i: (b, nbh + hh))),
            first_block=first_b, name="attn_forget")

        n_lat = c_q_lora + c_kv_lora
        gain_lat = jnp.concatenate([c_q_a_norm[layer], c_kv_a_norm[layer]]).reshape(1, n_lat)
        tm_lat = _pick(tokens, (512, 256))
        lat = _matmul(h, w[:, o_cq:o_kpe], tm=tm_lat, tn=n_lat,
                      epilogue=functools.partial(_ep_latent_norms, n_q=c_q_lora),
                      aux=(gain_lat,), aux_specs=(row_spec(n_lat),),
                      out_shape=jax.ShapeDtypeStruct((tokens, n_lat), BF16),
                      out_specs=pl.BlockSpec((tm_lat, n_lat), lambda i, j: (i, j)), name="proj_c_latent")
        wq = c_w_uq[layer].astype(BF16).reshape(c_q_lora, c_heads, c_nope + c_rope)
        wq = jnp.concatenate([wq, jnp.zeros((c_q_lora, c_heads, 2 * LANES - c_nope - c_rope), BF16)], axis=2)
        wq = wq.reshape(c_q_lora, c_heads * 2 * LANES)
        gq_nope = gq_n.reshape(1, LANES)
        gq_rope = jnp.concatenate([gq_r, jnp.zeros((LANES - c_rope,), F32)]).reshape(1, LANES)
        tn_q = 512
        lat_q_cols, lat_kv_cols = (0, c_q_lora), (c_q_lora, c_kv_lora)
        q_c = _matmul(lat, wq, a_cols=lat_q_cols, tm=tm, tn=tn_q, epilogue=functools.partial(_ep_latent_q, rope_dim=c_rope),
                      aux=(gq_nope, gq_rope, cos_t, sin_t, one_col),
                      aux_specs=(fixed_row, fixed_row, tok_lane, tok_lane, fixed_row),
                      out_shape=jax.ShapeDtypeStruct((tokens, c_heads * 2 * LANES), BF16),
                      out_specs=pl.BlockSpec((tm, tn_q), lambda i, j: (i, j)), name="proj_c_q")
        wkv = c_w_ukv[layer].astype(BF16).reshape(c_kv_lora, c_heads, c_nope + c_v)
        wk = wkv[:, :, :c_nope].reshape(c_kv_lora, c_heads * c_nope)
        wv = wkv[:, :, c_nope:].reshape(c_kv_lora, c_heads * c_v)
        gk_nope = c_qk_norm_nope[layer, 1].reshape(1, LANES)
        tn_k = 256
        k_c = _matmul(lat, wk, a_cols=lat_kv_cols, tm=tm, tn=tn_k, epilogue=_ep_latent_k,
                      aux=(gk_nope, kpe), aux_specs=(fixed_row, tok_lane),
                      out_shape=jax.ShapeDtypeStruct((tokens, c_heads * 2 * LANES), BF16),
                      out_specs=pl.BlockSpec((tm, 2 * tn_k), lambda i, j: (i, j)), name="proj_c_k")
        vt_c = vt_call(lat, wv, c_heads, "proj_c_v", a_cols=lat_kv_cols)
        o_c = _attention(
            "latent", safe_c, q_c, k_c, vt_c, batch=batch, seq=seq, n_heads=c_heads, t=t, hp=hp_c, dq=2 * LANES,
            q_col0=0, k_col0=0, name="attn_latent")

        tn_o = 512
        xf = _out_proj((o_a, o_b, o_c), w_out[layer].astype(BF16), xf, tm=tm, tn=tn_o)

        idx = layer // 2
        th = _pick(dense_w_gate.shape[-1], (512, 256))
        if layer % 2 == 0:
            h2 = _rms_norm(xf, ffn_norm[layer], tm=tm_n, name="ffn_norm")
            act = _swiglu_up(h2, dense_w_gate[idx].astype(BF16), dense_w_up[idx].astype(BF16),
                             tm=tm, th=th, name="ffn_up")
            tm_d = _pick(tokens, (512, 256))
            res_d = pl.BlockSpec((tm_d, tn_o), lambda i, j: (i, j))
            xf = _matmul(act, dense_w_down[idx].astype(BF16), tm=tm_d, tn=tn_o, epilogue=_ep_residual,
                         aux=(xf,), aux_specs=(res_d,),
                         out_shape=jax.ShapeDtypeStruct((tokens, d_model), F32), out_specs=res_d,
                         name="ffn_down")
        else:
            h2, route, counts = _rms_norm_router(xf, ffn_norm[layer], moe_router[idx], tm=tm_n,
                                                 name="ffn_norm_router")
            tmo = _pick(tokens, (512, 256, 128))
            n_rows = TOP_K * tokens + n_exp * tmo
            cnt = counts[0, :n_exp].astype(jnp.int32)
            padded = (cnt + tmo - 1) // tmo * tmo
            ends = jnp.cumsum(padded)
            starts = ends - padded
            n_valid = (ends[-1] // tmo).reshape(1).astype(jnp.int32)
            tile_expert = jnp.minimum(
                jnp.searchsorted(ends, jnp.arange(n_rows // tmo, dtype=jnp.int32) * tmo, side="right"),
                n_exp - 1).astype(jnp.int32)
            e1 = route[:, ROUTE_I1].astype(jnp.int32)
            e2 = route[:, ROUTE_I2].astype(jnp.int32)
            pos1 = starts[e1] + route[:, ROUTE_R1].astype(jnp.int32)
            pos2 = starts[e2] + route[:, ROUTE_R2].astype(jnp.int32)
            tm_r = _pick(tokens, (512, 256, 128))
            per_tile = lambda a, b: jnp.concatenate(
                [a.reshape(tokens // tm_r, 1, tm_r), b.reshape(tokens // tm_r, 1, tm_r)], axis=2)
            pos = per_tile(pos1, pos2)
            gates = per_tile(route[:, ROUTE_G1], route[:, ROUTE_G2])

            tok = jnp.arange(tokens, dtype=jnp.int32)
            src = jnp.zeros((n_rows,), jnp.int32).at[jnp.concatenate([pos1, pos2])].set(
                jnp.concatenate([tok, tok]), unique_indices=True)
            xs = _dispatch(_row_view(h2), src.reshape(n_rows // tm_r, 1, tm_r), tm=tm_r).reshape(n_rows, d_model)
            act = _grouped_matmul(_grouped_up_body, xs,
                                  (moe_w_gate[idx].astype(BF16), moe_w_up[idx].astype(BF16)),
                                  tile_expert, n_valid, tmo=tmo, tn=512, name="moe_up")
            ys = _grouped_matmul(_grouped_down_body, act, (moe_w_down[idx].astype(BF16),),
                                 tile_expert, n_valid, tmo=tmo, tn=1024, name="moe_down")
            pending = _combine(_row_view(ys), pos, gates, tm=tm_r).reshape(tokens, d_model)

    if pending is not None:
        xf = _residual_add(xf, pending, tm=tm_n)
    return xf.reshape(batch, seq, d_model)
```
